```python
import jax, jax.numpy as jnp
from jax import lax
import numpy as np

D_MODEL = 2048
BATCH = 32
SEQ = 256
DEPTH = 1
DEC_BATCH = 4
DEC_SEQ = 1024
PAST_LEN = 256

GRID_W = 64
HEAD_DIM = 128
N_HEADS_A = 8
N_KV_A = 2
N_HEADS_B = 8
WINDOW_A = 128
BLOCK_A = 128
NA_ROWS = 8
NA_COLS = 16
NA_COL_BLOCK = 16
NA_COL_SPAN = NA_COL_BLOCK + NA_COLS
N_EXPERTS = 16
EC_CAPACITY = 2
D_FF = D_MODEL
Q_BLOCK = 128
ROPE_THETA = 10000.0
LN_EPS = 1e-5
NEG_INF = -1e30
Q_A_COLS = N_HEADS_A * HEAD_DIM
KV_A_COLS = N_KV_A * HEAD_DIM
Q_B_COLS = N_HEADS_B * HEAD_DIM
IN_COLS = Q_A_COLS + 2 * KV_A_COLS + 3 * Q_B_COLS
SPLIT_POINTS = (Q_A_COLS, Q_A_COLS + KV_A_COLS, Q_A_COLS + 2 * KV_A_COLS,
                Q_A_COLS + 2 * KV_A_COLS + Q_B_COLS, Q_A_COLS + 2 * KV_A_COLS + 2 * Q_B_COLS)
MIX_WIDTH = Q_A_COLS + Q_B_COLS
DEEPNORM_ALPHA = (2.0 * DEPTH) ** 0.25
DEEPNORM_BETA = (8.0 * DEPTH) ** -0.25

kernel_name = "hybrid_diffusion_window_natten_ec_step"


def layer_norm(x, g, b):
    xf = x.astype(jnp.float32)
    mu = jnp.mean(xf, axis=-1, keepdims=True)
    var = jnp.mean(jnp.square(xf - mu), axis=-1, keepdims=True)
    y = (xf - mu) * lax.rsqrt(var + LN_EPS)
    return (y * g.astype(jnp.float32) + b.astype(jnp.float32)).astype(x.dtype)


def post_norm(x, update, g, b):
    return layer_norm(DEEPNORM_ALPHA * x + update, g, b)


def modulation(cond, w_mod, b_mod):
    m = jax.nn.silu(cond) @ w_mod + b_mod
    return jnp.split(m, 6, axis=-1)


def project_in(h, w_in):
    B, S, _ = h.shape
    qa, ka, va, qb, kb, vb = jnp.split(h @ w_in, list(SPLIT_POINTS), axis=-1)
    return (qa.reshape(B, S, N_HEADS_A, HEAD_DIM), ka.reshape(B, S, N_KV_A, HEAD_DIM),
            va.reshape(B, S, N_KV_A, HEAD_DIM), qb.reshape(B, S, N_HEADS_B, HEAD_DIM),
            kb.reshape(B, S, N_HEADS_B, HEAD_DIM), vb.reshape(B, S, N_HEADS_B, HEAD_DIM))


def axial_rope(x):
    T = x.shape[1]
    t = jnp.arange(T)
    row = (t // GRID_W).astype(jnp.float32)
    col = (t % GRID_W).astype(jnp.float32)
    n_freq = HEAD_DIM // 4
    inv_freq = ROPE_THETA ** (-jnp.arange(n_freq, dtype=jnp.float32) / n_freq)
    ang = jnp.concatenate([row[:, None] * inv_freq, col[:, None] * inv_freq], axis=-1)
    cos = jnp.cos(ang)[None, :, None, :].astype(x.dtype)
    sin = jnp.sin(ang)[None, :, None, :].astype(x.dtype)
    x1 = x[..., 0::2]
    x2 = x[..., 1::2]
    return jnp.stack([x1 * cos - x2 * sin, x1 * sin + x2 * cos], axis=-1).reshape(x.shape)


def context_attention(q, k, v, sink):
    B, S, H, D = q.shape
    KV = k.shape[2]
    G = H // KV
    n_qb = S // Q_BLOCK
    scale = HEAD_DIM ** -0.5
    q_blocks = q.reshape(B, n_qb, Q_BLOCK, KV, G, D).transpose(1, 0, 2, 3, 4, 5)

    def attend_block(qb):
        s = jnp.einsum('bqkgd,blkd->bkgql', qb, k).astype(jnp.float32) * scale
        if sink is not None:
            sk = jnp.broadcast_to(sink.reshape(KV, G)[None, :, :, None, None].astype(jnp.float32),
                                  s.shape[:-1] + (1,))
            p = jax.nn.softmax(jnp.concatenate([s, sk], axis=-1), axis=-1)[..., :-1]
        else:
            p = jax.nn.softmax(s, axis=-1)
        return jnp.einsum('bkgql,blkd->bqkgd', p.astype(v.dtype), v)

    o = lax.map(attend_block, q_blocks)
    return o.transpose(1, 0, 2, 3, 4, 5).reshape(B, S, H * D)


def window_attention_latent(q, k, v, ck, cv, sink):
    B, T, H, D = q.shape
    G = H // N_KV_A
    L = ck.shape[1]
    nb = T // BLOCK_A
    scale = HEAD_DIM ** -0.5
    qb = q.reshape(B, nb, BLOCK_A, N_KV_A, G, D)
    pad = ((0, 0), (BLOCK_A, BLOCK_A), (0, 0), (0, 0))
    kp = jnp.pad(k, pad).reshape(B, nb + 2, BLOCK_A, N_KV_A, D)
    vp = jnp.pad(v, pad).reshape(B, nb + 2, BLOCK_A, N_KV_A, D)
    kw = jnp.concatenate([kp[:, :-2], kp[:, 1:-1], kp[:, 2:]], axis=2)
    vw = jnp.concatenate([vp[:, :-2], vp[:, 1:-1], vp[:, 2:]], axis=2)
    qpos = jnp.arange(nb)[:, None] * BLOCK_A + jnp.arange(BLOCK_A)[None, :]
    kpos = jnp.arange(nb)[:, None] * BLOCK_A - BLOCK_A + jnp.arange(3 * BLOCK_A)[None, :]
    valid = ((jnp.abs(qpos[:, :, None] - kpos[:, None, :]) <= WINDOW_A)
             & (kpos[:, None, :] >= 0) & (kpos[:, None, :] < T))
    s_loc = jnp.einsum('bnqkgd,bnjkd->bnkgqj', qb, kw).astype(jnp.float32) * scale
    s_loc = jnp.where(valid[None, :, None, None], s_loc, NEG_INF)
    s_ctx = jnp.einsum('bnqkgd,blkd->bnkgql', qb, ck).astype(jnp.float32) * scale
    s_sink = jnp.broadcast_to(sink.reshape(N_KV_A, G)[None, None, :, :, None, None].astype(jnp.float32),
                              s_loc.shape[:-1] + (1,))
    p = jax.nn.softmax(jnp.concatenate([s_loc, s_ctx, s_sink], axis=-1), axis=-1)
    n_loc = 3 * BLOCK_A
    p_loc = p[..., :n_loc].astype(v.dtype)
    p_ctx = p[..., n_loc:n_loc + L].astype(v.dtype)
    o = (jnp.einsum('bnkgqj,bnjkd->bnqkgd', p_loc, vw)
         + jnp.einsum('bnkgql,blkd->bnqkgd', p_ctx, cv))
    return o.reshape(B, T, H * D)


def neighborhood_attention_latent(q, k, v, ck, cv, rel_bias):
    B, T, H, D = q.shape
    L = ck.shape[1]
    rows = T // GRID_W
    kh = min(NA_ROWS, rows)
    ncb = GRID_W // NA_COL_BLOCK
    scale = HEAD_DIM ** -0.5
    r = jnp.arange(rows)
    row_idx = jnp.clip(r - kh // 2, 0, rows - kh)[:, None] + jnp.arange(kh)[None, :]
    span_start = jnp.clip(jnp.arange(ncb) * NA_COL_BLOCK - NA_COLS // 2, 0, GRID_W - NA_COL_SPAN)
    kcol = span_start[:, None] + jnp.arange(NA_COL_SPAN)[None, :]
    qcol = jnp.arange(ncb)[:, None] * NA_COL_BLOCK + jnp.arange(NA_COL_BLOCK)[None, :]
    w_start = jnp.clip(qcol - NA_COLS // 2, 0, GRID_W - NA_COLS)
    valid = ((kcol[:, None, :] >= w_start[:, :, None])
             & (kcol[:, None, :] < w_start[:, :, None] + NA_COLS))
    dr_idx = row_idx - r[:, None] + NA_ROWS - 1
    dc_idx = jnp.clip(kcol[:, None, :] - qcol[:, :, None], -(NA_COLS - 1), NA_COLS - 1) + NA_COLS - 1
    bias = rel_bias[:, dr_idx[:, None, None, :, None], dc_idx[None, :, :, None, :]]
    kg = k.reshape(B, rows, GRID_W, H, D)
    vg = v.reshape(B, rows, GRID_W, H, D)
    k_blk = kg[:, row_idx][:, :, :, kcol]
    v_blk = vg[:, row_idx][:, :, :, kcol]
    qg = q.reshape(B, rows, ncb, NA_COL_BLOCK, H, D)
    s_loc = (jnp.einsum('brjqhd,brkjchd->bhrjqkc', qg, k_blk).astype(jnp.float32) * scale
             + bias[None].astype(jnp.float32))
    s_loc = jnp.where(valid[:, :, None, :], s_loc, NEG_INF)
    n_loc = kh * NA_COL_SPAN
    s_loc = s_loc.reshape(B, H, rows, ncb, NA_COL_BLOCK, n_loc)
    s_ctx = jnp.einsum('brjqhd,blhd->bhrjql', qg, ck).astype(jnp.float32) * scale
    p = jax.nn.softmax(jnp.concatenate([s_loc, s_ctx], axis=-1), axis=-1)
    p_loc = p[..., :n_loc].reshape(B, H, rows, ncb, NA_COL_BLOCK, kh, NA_COL_SPAN).astype(v.dtype)
    p_ctx = p[..., n_loc:].astype(v.dtype)
    o = (jnp.einsum('bhrjqkc,brkjchd->brjqhd', p_loc, v_blk)
         + jnp.einsum('bhrjql,blhd->brjqhd', p_ctx, cv))
    return o.reshape(B, T, H * D)


def expert_choice_ffn(h, w_router, w_gate, w_up, w_down):
    B, S, D = h.shape
    cap = EC_CAPACITY * S // N_EXPERTS
    aff = jax.nn.softmax(jnp.einsum('bsd,de->bse', h, w_router).astype(jnp.float32), axis=-1)
    g, idx = lax.top_k(jnp.swapaxes(aff, 1, 2), cap)
    xs = jax.vmap(lambda hb, ib: hb[ib])(h, idx)
    a = jnp.einsum('becd,edf->becf', xs, w_gate)
    u = jnp.einsum('becd,edf->becf', xs, w_up)
    y = jnp.einsum('becf,efd->becd', jax.nn.silu(a) * u, w_down) * g[..., None].astype(h.dtype)
    return jax.vmap(lambda yb, ib: jnp.zeros((S, D), yb.dtype).at[ib.reshape(-1)].add(yb.reshape(-1, D)))(y, idx)


def channel_sublayer(x, shift, scale, gate, w_router, w_gate, w_up, w_down, g, b):
    h = x * (1 + scale) + shift
    return post_norm(x, gate * expert_choice_ffn(h, w_router, w_gate, w_up, w_down), g, b)


def context_layer(x, mod, w_in, w_o, sink, w_router, w_gate, w_up, w_down, ln1_g, ln1_b, ln2_g, ln2_b):
    shift1, scale1, gate1, shift2, scale2, gate2 = mod
    h = x * (1 + scale1) + shift1
    qa, ka, va, qb, kb, vb = project_in(h, w_in)
    oa = context_attention(qa, ka, va, sink)
    ob = context_attention(qb, kb, vb, None)
    attn = jnp.concatenate([oa, ob], axis=-1) @ w_o
    x = post_norm(x, gate1 * attn, ln1_g, ln1_b)
    x = channel_sublayer(x, shift2, scale2, gate2, w_router, w_gate, w_up, w_down, ln2_g, ln2_b)
    return x, ka, va, kb, vb


def latent_layer(x, mod, ck_a, cv_a, ck_b, cv_b, w_in, w_o, sink, rel_bias, w_router, w_gate, w_up,
                 w_down, ln1_g, ln1_b, ln2_g, ln2_b):
    shift1, scale1, gate1, shift2, scale2, gate2 = mod
    h = x * (1 + scale1) + shift1
    qa, ka, va, qb, kb, vb = project_in(h, w_in)
    oa = window_attention_latent(axial_rope(qa), axial_rope(ka), va, ck_a, cv_a, sink)
    ob = neighborhood_attention_latent(qb, kb, vb, ck_b, cv_b, rel_bias)
    attn = jnp.concatenate([oa, ob], axis=-1) @ w_o
    x = post_norm(x, gate1 * attn, ln1_g, ln1_b)
    return channel_sublayer(x, shift2, scale2, gate2, w_router, w_gate, w_up, w_down, ln2_g, ln2_b)


def setup_inputs(seed: int = 0) -> dict:
    key = jax.random.key(seed)
    ks = jax.random.split(key, 32)
    f32 = jnp.float32

    def nrm(k, shape, s):
        return jax.random.normal(k, shape, f32) * s

    d_in = D_MODEL ** -0.5
    beta = DEEPNORM_BETA
    w_in = jnp.concatenate([
        nrm(ks[10], (DEPTH, D_MODEL, Q_A_COLS), d_in),
        nrm(ks[11], (DEPTH, D_MODEL, KV_A_COLS), d_in),
        nrm(ks[12], (DEPTH, D_MODEL, KV_A_COLS), d_in * beta),
        nrm(ks[13], (DEPTH, D_MODEL, Q_B_COLS), d_in),
        nrm(ks[14], (DEPTH, D_MODEL, Q_B_COLS), d_in),
        nrm(ks[15], (DEPTH, D_MODEL, Q_B_COLS), d_in * beta)], axis=-1)
    return {
        "x_prompt": nrm(ks[0], (BATCH, SEQ, D_MODEL), 1.0),
        "x_sample": nrm(ks[1], (DEC_BATCH, DEC_SEQ, D_MODEL), 1.0),
        "cache_k_a": nrm(ks[2], (DEC_BATCH, DEPTH, PAST_LEN, N_KV_A, HEAD_DIM), 1.0),
        "cache_v_a": nrm(ks[3], (DEC_BATCH, DEPTH, PAST_LEN, N_KV_A, HEAD_DIM), beta),
        "cache_k_b": nrm(ks[4], (DEC_BATCH, DEPTH, PAST_LEN, N_HEADS_B, HEAD_DIM), 1.0),
        "cache_v_b": nrm(ks[5], (DEC_BATCH, DEPTH, PAST_LEN, N_HEADS_B, HEAD_DIM), beta),
        "c": nrm(ks[6], (DEC_BATCH, D_MODEL), 1.0),
        "c_ctx": nrm(ks[7], (D_MODEL,), 1.0),
        "w_mod": nrm(ks[8], (DEPTH, D_MODEL, 6 * D_MODEL), 0.5 * d_in),
        "b_mod": nrm(ks[9], (DEPTH, 6 * D_MODEL), 0.02),
        "w_in": w_in,
        "w_o": nrm(ks[16], (DEPTH, MIX_WIDTH, D_MODEL), MIX_WIDTH ** -0.5 * beta),
        "sink_a": nrm(ks[17], (DEPTH, N_HEADS_A), 0.5),
        "rel_bias_b": nrm(ks[18], (DEPTH, N_HEADS_B, 2 * NA_ROWS - 1, 2 * NA_COLS - 1), 0.1),
        "w_router": nrm(ks[19], (DEPTH, D_MODEL, N_EXPERTS), d_in),
        "w_gate": nrm(ks[20], (DEPTH, N_EXPERTS, D_MODEL, D_FF), d_in),
        "w_up": nrm(ks[21], (DEPTH, N_EXPERTS, D_MODEL, D_FF), d_in),
        "w_down": nrm(ks[22], (DEPTH, N_EXPERTS, D_FF, D_MODEL), D_FF ** -0.5 * beta),
        "ln1_g": 1.0 + nrm(ks[23], (DEPTH, D_MODEL), 0.02),
        "ln1_b": nrm(ks[24], (DEPTH, D_MODEL), 0.02),
        "ln2_g": 1.0 + nrm(ks[25], (DEPTH, D_MODEL), 0.02),
        "ln2_b": nrm(ks[26], (DEPTH, D_MODEL), 0.02),
    }


def reference(x_prompt, x_sample, cache_k_a, cache_v_a, cache_k_b, cache_v_b, c, c_ctx, w_mod, b_mod,
              w_in, w_o, sink_a, rel_bias_b, w_router, w_gate, w_up, w_down, ln1_g, ln1_b, ln2_g, ln2_b):
    y_p = x_prompt
    y_s = x_sample
    ka_list, va_list, kb_list, vb_list = [], [], [], []
    for l in range(DEPTH):
        mod_ctx = modulation(c_ctx, w_mod[l], b_mod[l])
        mod_lat = modulation(c[:, None, :], w_mod[l], b_mod[l])
        y_p, ka, va, kb, vb = context_layer(
            y_p, mod_ctx, w_in[l], w_o[l], sink_a[l], w_router[l], w_gate[l], w_up[l], w_down[l],
            ln1_g[l], ln1_b[l], ln2_g[l], ln2_b[l])
        ka_list.append(ka)
        va_list.append(va)
        kb_list.append(kb)
        vb_list.append(vb)
        y_s = latent_layer(
            y_s, mod_lat, cache_k_a[:, l], cache_v_a[:, l], cache_k_b[:, l], cache_v_b[:, l],
            w_in[l], w_o[l], sink_a[l], rel_bias_b[l], w_router[l], w_gate[l], w_up[l], w_down[l],
            ln1_g[l], ln1_b[l], ln2_g[l], ln2_b[l])
    new_k_a = jnp.stack(ka_list, axis=1)
    new_v_a = jnp.stack(va_list, axis=1)
    new_k_b = jnp.stack(kb_list, axis=1)
    new_v_b = jnp.stack(vb_list, axis=1)
    return (y_p, y_s, new_k_a, new_v_a, new_k_b, new_v_b)
```

```python
import functools

import jax
import jax.numpy as jnp
from jax import lax
from jax.experimental import pallas as pl
from jax.experimental.pallas import tpu as pltpu

D_MODEL = 2048
BATCH = 32
SEQ = 256
DEC_BATCH = 4
DEC_SEQ = 1024
PAST_LEN = 256
GRID_W = 64
HEAD_DIM = 128
N_HEADS_A = 8
N_KV_A = 2
GROUP_A = N_HEADS_A // N_KV_A
N_HEADS_B = 8
WINDOW_A = 128
BLOCK_A = 128
NA_ROWS = 8
NA_COLS = 16
N_EXPERTS = 16
EC_CAPACITY = 2
D_FF = D_MODEL
ROPE_THETA = 10000.0
LN_EPS = 1e-5
NEG_INF = -1e30
DEPTH = 1
Q_A_COLS = N_HEADS_A * HEAD_DIM
KV_A_COLS = N_KV_A * HEAD_DIM
Q_B_COLS = N_HEADS_B * HEAD_DIM
IN_COLS = Q_A_COLS + 2 * KV_A_COLS + 3 * Q_B_COLS
COL_STARTS = (0, Q_A_COLS, Q_A_COLS + KV_A_COLS, Q_A_COLS + 2 * KV_A_COLS,
              Q_A_COLS + 2 * KV_A_COLS + Q_B_COLS, Q_A_COLS + 2 * KV_A_COLS + 2 * Q_B_COLS, IN_COLS)
DEEPNORM_ALPHA = (2.0 * DEPTH) ** 0.25
ATTN_SCALE = HEAD_DIM ** -0.5

N_CTX_TOK = BATCH * SEQ
N_LAT_TOK = DEC_BATCH * DEC_SEQ
GRID_ROWS = DEC_SEQ // GRID_W
CAP_CTX = EC_CAPACITY * SEQ // N_EXPERTS
CAP_LAT = EC_CAPACITY * DEC_SEQ // N_EXPERTS
N_COND = 1 + DEC_BATCH
N_MOD = 6

V7X_LANES = 128
V7X_SUBLANES = 8
V7X_VMEM_BYTES = 64 * 1024 * 1024
MIB = 1024 * 1024

F32 = jnp.float32
BF16 = jnp.bfloat16
NT_DIMS = (((1,), (1,)), ((), ()))

TOKEN_TILE = 256
MOD_COL_TILE = 1024
FFN_COL_TILE = 512
RANK_CHUNK = 256
NA_GROUP_ROWS = 4
NA_WIN_ROWS = 12
NA_REL_ROWS = 2 * NA_ROWS - 1
NA_REL_COLS = 2 * NA_COLS - 1


def _params(vmem_mib, n_axes):
    return pltpu.CompilerParams(dimension_semantics=("arbitrary",) * n_axes,
                                vmem_limit_bytes=vmem_mib * MIB)


def _dot(a, b):
    return jnp.dot(a, b, preferred_element_type=F32)


def _dot_nt(a, b):
    return lax.dot_general(a, b, NT_DIMS, preferred_element_type=F32)


def _layer_norm(z, g, b):
    mu = jnp.mean(z, axis=-1, keepdims=True)
    zc = z - mu
    var = jnp.mean(zc * zc, axis=-1, keepdims=True)
    return zc * lax.rsqrt(var + LN_EPS) * g + b


def _mod_kernel(ct_ref, w_ref, b_ref, o_ref):
    tn = w_ref.shape[1]

    def body(kc, accs):
        k0 = pl.multiple_of(kc * V7X_SUBLANES, V7X_SUBLANES)
        wk = w_ref[pl.ds(k0, V7X_SUBLANES), :]
        sk = jax.nn.silu(ct_ref[pl.ds(k0, V7X_SUBLANES), :])
        return tuple(acc + wk * sk[:, r:r + 1] for r, acc in enumerate(accs))

    accs = lax.fori_loop(0, D_MODEL // V7X_SUBLANES, body,
                         tuple(jnp.zeros((V7X_SUBLANES, tn), F32) for _ in range(N_COND)), unroll=4)
    rows = [jnp.sum(acc, axis=0, keepdims=True) for acc in accs]
    rows.append(jnp.zeros((V7X_SUBLANES - N_COND, tn), F32))
    o_ref[...] = jnp.concatenate(rows, axis=0) + b_ref[...]


def _modulation(cond_t, w_mod, b_mod):
    n_out = N_MOD * D_MODEL
    return pl.pallas_call(
        _mod_kernel,
        grid=(n_out // MOD_COL_TILE,),
        in_specs=[pl.BlockSpec((D_MODEL, V7X_SUBLANES), lambda j: (0, 0)),
                  pl.BlockSpec((D_MODEL, MOD_COL_TILE), lambda j: (0, j)),
                  pl.BlockSpec((1, MOD_COL_TILE), lambda j: (0, j))],
        out_specs=pl.BlockSpec((V7X_SUBLANES, MOD_COL_TILE), lambda j: (0, j)),
        out_shape=jax.ShapeDtypeStruct((V7X_SUBLANES, n_out), F32),
        compiler_params=_params(40, 1),
        name="modulation",
    )(cond_t, w_mod, b_mod)


def _mod_spec(which, latent):
    steps_per_request = DEC_SEQ // TOKEN_TILE
    if latent:
        return pl.BlockSpec((None, 1, D_MODEL), lambda i: ((1 + i // steps_per_request) * N_MOD + which, 0, 0))
    return pl.BlockSpec((None, 1, D_MODEL), lambda i: (which, 0, 0))


def _rope(x, cos, sin, even):
    swapped = jnp.where(even, pltpu.roll(x, HEAD_DIM - 1, 1), pltpu.roll(x, 1, 1))
    return x * cos + swapped * sin


def _inproj_kernel(*refs, latent):
    if latent:
        (x_ref, shift_ref, scale_ref, w_ref, cos_ref, sin_ref,
         qa_o, ka_o, va_o, qb_o, kb_o, vb_o) = refs
    else:
        (x_ref, shift_ref, scale_ref, w_ref,
         qa_o, ka_o, va_o, qb_o, kb_o, vb_o, kaf_o, vaf_o, kbf_o, vbf_o) = refs
    h = (x_ref[...] * (1.0 + scale_ref[...]) + shift_ref[...]).astype(BF16)

    def proj(piece):
        return _dot(h, w_ref[:, COL_STARTS[piece]:COL_STARTS[piece + 1]])

    if latent:
        cos = cos_ref[...]
        sin = sin_ref[...]
        even = (lax.broadcasted_iota(jnp.int32, cos.shape, 1) & 1) == 0

    def heads(y, o_ref, n_heads, scale, rope):
        for hd in range(n_heads):
            cols = slice(hd * HEAD_DIM, (hd + 1) * HEAD_DIM)
            yh = y[:, cols]
            if rope:
                yh = _rope(yh, cos, sin, even)
            if scale != 1.0:
                yh = yh * scale
            o_ref[:, cols] = yh.astype(BF16)

    qa = proj(0)
    heads(qa, qa_o, N_HEADS_A, ATTN_SCALE, latent)
    ka = proj(1)
    heads(ka, ka_o, N_KV_A, 1.0, latent)
    va = proj(2)
    va_o[...] = va.astype(BF16)
    qb = proj(3)
    qb_o[...] = (qb * ATTN_SCALE).astype(BF16)
    kb = proj(4)
    kb_o[...] = kb.astype(BF16)
    vb = proj(5)
    vb_o[...] = vb.astype(BF16)
    if not latent:
        kaf_o[...] = ka
        vaf_o[...] = va
        kbf_o[...] = kb
        vbf_o[...] = vb


def _input_projection(x, mod3, w_in_bf16, rope_tables, latent):
    n_tok = x.shape[0]
    tm = TOKEN_TILE
    row = lambda i: (i, 0)
    in_specs = [pl.BlockSpec((tm, D_MODEL), row), _mod_spec(0, latent), _mod_spec(1, latent),
                pl.BlockSpec((D_MODEL, IN_COLS), lambda i: (0, 0), pipeline_mode=pl.Buffered(1))]
    args = [x, mod3, mod3, w_in_bf16]
    widths = [Q_A_COLS, KV_A_COLS, KV_A_COLS, Q_B_COLS, Q_B_COLS, Q_B_COLS]
    out_specs = [pl.BlockSpec((tm, w), row) for w in widths]
    out_shape = [jax.ShapeDtypeStruct((n_tok, w), BF16) for w in widths]
    if latent:
        steps_per_request = DEC_SEQ // tm
        in_specs += [pl.BlockSpec((tm, HEAD_DIM), lambda i: (i % steps_per_request, 0))] * 2
        args += list(rope_tables)
    else:
        kv_widths = [KV_A_COLS, KV_A_COLS, Q_B_COLS, Q_B_COLS]
        out_specs += [pl.BlockSpec((tm, w), row) for w in kv_widths]
        out_shape += [jax.ShapeDtypeStruct((n_tok, w), F32) for w in kv_widths]
    return pl.pallas_call(
        functools.partial(_inproj_kernel, latent=latent),
        grid=(n_tok // tm,),
        in_specs=in_specs, out_specs=out_specs, out_shape=out_shape,
        compiler_params=_params(52, 1),
        name="inproj_latent" if latent else "inproj_context",
    )(*args)


def _softmax_av(scores, values, sink=None):
    m = functools.reduce(jnp.maximum, [jnp.max(s, axis=-1, keepdims=True) for s in scores])
    if sink is not None:
        m = jnp.maximum(m, sink)
    probs = [jnp.exp(s - m) for s in scores]
    denom = functools.reduce(jnp.add, [jnp.sum(p, axis=-1, keepdims=True) for p in probs])
    if sink is not None:
        denom = denom + jnp.exp(sink - m)
    out = functools.reduce(jnp.add, [_dot(p.astype(BF16), v) for p, v in zip(probs, values)])
    return out / denom


def _stack_heads(q_ref, first_head, n_heads):
    return jnp.concatenate([q_ref[:, (first_head + g) * HEAD_DIM:(first_head + g + 1) * HEAD_DIM]
                            for g in range(n_heads)], axis=0)


def _sink_column(sink_ref, first_head, n_heads, rows):
    return jnp.concatenate([jnp.full((rows, 1), sink_ref[first_head + g], F32) for g in range(n_heads)], axis=0)


def _ctx_attn_kernel(sink_ref, qa_ref, ka_ref, va_ref, qb_ref, kb_ref, vb_ref, oa_ref, ob_ref):
    for kv in range(N_KV_A):
        cols = slice(kv * HEAD_DIM, (kv + 1) * HEAD_DIM)
        q = _stack_heads(qa_ref, kv * GROUP_A, GROUP_A)
        sink = _sink_column(sink_ref, kv * GROUP_A, GROUP_A, SEQ)
        o = _softmax_av([_dot_nt(q, ka_ref[:, cols])], [va_ref[:, cols]], sink)
        for g in range(GROUP_A):
            hd = kv * GROUP_A + g
            oa_ref[:, hd * HEAD_DIM:(hd + 1) * HEAD_DIM] = o[g * SEQ:(g + 1) * SEQ].astype(BF16)
    for hd in range(N_HEADS_B):
        cols = slice(hd * HEAD_DIM, (hd + 1) * HEAD_DIM)
        o = _softmax_av([_dot_nt(qb_ref[:, cols], kb_ref[:, cols])], [vb_ref[:, cols]])
        ob_ref[:, cols] = o.astype(BF16)


def _context_attention(sink, qa, ka, va, qb, kb, vb):
    row = lambda b: (b, 0)
    widths = [Q_A_COLS, KV_A_COLS, KV_A_COLS, Q_B_COLS, Q_B_COLS, Q_B_COLS]
    return pl.pallas_call(
        _ctx_attn_kernel,
        grid=(BATCH,),
        in_specs=[pl.BlockSpec(memory_space=pltpu.SMEM)] + [pl.BlockSpec((SEQ, w), row) for w in widths],
        out_specs=[pl.BlockSpec((SEQ, Q_A_COLS), row), pl.BlockSpec((SEQ, Q_B_COLS), row)],
        out_shape=[jax.ShapeDtypeStruct((N_CTX_TOK, Q_A_COLS), BF16),
                   jax.ShapeDtypeStruct((N_CTX_TOK, Q_B_COLS), BF16)],
        compiler_params=_params(32, 1),
        name="context_attention",
    )(sink, qa, ka, va, qb, kb, vb)


def _window_attn_kernel(sink_ref, q_ref, k_ref, v_ref, ck_ref, cv_ref, o_ref):
    kv = pl.program_id(1)
    n = pl.program_id(2)
    span = 3 * BLOCK_A
    start = pl.multiple_of(jnp.clip((n - 1) * BLOCK_A, 0, DEC_SEQ - span), BLOCK_A)
    q = _stack_heads(q_ref, 0, GROUP_A)
    rows = GROUP_A * BLOCK_A
    sink = jnp.concatenate([jnp.full((BLOCK_A, 1), sink_ref[kv * GROUP_A + g], F32) for g in range(GROUP_A)],
                           axis=0)
    qpos = n * BLOCK_A + (lax.broadcasted_iota(jnp.int32, (rows, span), 0) & (BLOCK_A - 1))
    kpos = start + lax.broadcasted_iota(jnp.int32, (rows, span), 1)
    valid = jnp.abs(qpos - kpos) <= WINDOW_A
    s_loc = jnp.where(valid, _dot_nt(q, k_ref[pl.ds(start, span), :]), NEG_INF)
    s_ctx = _dot_nt(q, ck_ref[...].astype(BF16))
    o = _softmax_av([s_loc, s_ctx], [v_ref[pl.ds(start, span), :], cv_ref[...].astype(BF16)], sink)
    for g in range(GROUP_A):
        o_ref[:, g * HEAD_DIM:(g + 1) * HEAD_DIM] = o[g * BLOCK_A:(g + 1) * BLOCK_A].astype(BF16)


def _window_attention(sink, qa, ka, va, ck, cv):
    nb = DEC_SEQ // BLOCK_A
    q_spec = pl.BlockSpec((BLOCK_A, GROUP_A * HEAD_DIM), lambda b, kv, n: (b * nb + n, kv))
    kv_spec = pl.BlockSpec((DEC_SEQ, HEAD_DIM), lambda b, kv, n: (b, kv))
    c_spec = pl.BlockSpec((PAST_LEN, HEAD_DIM), lambda b, kv, n: (b, kv))
    return pl.pallas_call(
        _window_attn_kernel,
        grid=(DEC_BATCH, N_KV_A, nb),
        in_specs=[pl.BlockSpec(memory_space=pltpu.SMEM), q_spec, kv_spec, kv_spec, c_spec, c_spec],
        out_specs=q_spec,
        out_shape=jax.ShapeDtypeStruct((N_LAT_TOK, Q_A_COLS), BF16),
        compiler_params=_params(32, 3),
        name="window_attention",
    )(sink, qa, ka, va, ck, cv)


def _na_row_start(rq):
    return min(max(rq - NA_ROWS // 2, 0), GRID_ROWS - NA_ROWS)


def _na_window_start(group):
    return min(max(group * NA_GROUP_ROWS - NA_ROWS // 2, 0), GRID_ROWS - NA_WIN_ROWS)


def _na_build_bias(rb_ref, bias_ref, head):
    shape = (GRID_W, 2 * GRID_W)
    lane = lax.broadcasted_iota(jnp.int32, shape, 1)
    qcol = lax.broadcasted_iota(jnp.int32, shape, 0)
    kcol = lane & (GRID_W - 1)
    left = lane < GRID_W
    dcol = kcol - qcol + (NA_COLS - 1)
    wstart = jnp.clip(qcol - NA_COLS // 2, 0, GRID_W - NA_COLS)
    col_ok = (kcol >= wstart) & (kcol < wstart + NA_COLS)
    base = head * (NA_REL_ROWS * NA_REL_COLS)
    toeplitz = []
    for a in range(NA_REL_ROWS):
        acc = jnp.zeros(shape, F32)
        for bb in range(NA_REL_COLS):
            acc = jnp.where(dcol == bb, rb_ref[base + a * NA_REL_COLS + bb], acc)
        toeplitz.append(acc)
    neg = jnp.full(shape, NEG_INF, F32)
    for g in range(GRID_ROWS // NA_GROUP_ROWS):
        wrow = _na_window_start(g)
        for qi in range(NA_GROUP_ROWS):
            rq = g * NA_GROUP_ROWS + qi
            row0 = _na_row_start(rq)
            for p in range(NA_WIN_ROWS // 2):
                rk0 = wrow + 2 * p
                ok0 = row0 <= rk0 < row0 + NA_ROWS
                ok1 = row0 <= rk0 + 1 < row0 + NA_ROWS
                a0 = rk0 - rq + NA_ROWS - 1
                if ok0 and ok1:
                    blk = jnp.where(col_ok, jnp.where(left, toeplitz[a0], toeplitz[a0 + 1]), neg)
                elif ok0:
                    blk = jnp.where(col_ok & left, toeplitz[a0], neg)
                elif ok1:
                    blk = jnp.where(col_ok & jnp.logical_not(left), toeplitz[a0 + 1], neg)
                else:
                    blk = neg
                bias_ref[g, qi * GRID_W:(qi + 1) * GRID_W, p * 2 * GRID_W:(p + 1) * 2 * GRID_W] = blk


def _na_attn_kernel(rb_ref, q_ref, k_ref, v_ref, ck_ref, cv_ref, o_ref, bias_ref):
    head = pl.program_id(0)

    @pl.when(pl.program_id(1) == 0)
    def _():
        _na_build_bias(rb_ref, bias_ref, head)

    ck = ck_ref[...].astype(BF16)
    cv = cv_ref[...].astype(BF16)
    q_rows = NA_GROUP_ROWS * GRID_W
    k_rows = NA_WIN_ROWS * GRID_W
    for g in range(GRID_ROWS // NA_GROUP_ROWS):
        k0 = _na_window_start(g) * GRID_W
        q = q_ref[g * q_rows:(g + 1) * q_rows, :]
        s_loc = _dot_nt(q, k_ref[k0:k0 + k_rows, :]) + bias_ref[g]
        s_ctx = _dot_nt(q, ck)
        o = _softmax_av([s_loc, s_ctx], [v_ref[k0:k0 + k_rows, :], cv])
        o_ref[g * q_rows:(g + 1) * q_rows, :] = o.astype(BF16)


def _neighborhood_attention(rel_bias_flat, qb, kb, vb, ck, cv):
    tok_spec = pl.BlockSpec((DEC_SEQ, HEAD_DIM), lambda h, b: (b, h))
    c_spec = pl.BlockSpec((PAST_LEN, HEAD_DIM), lambda h, b: (b, h))
    n_groups = GRID_ROWS // NA_GROUP_ROWS
    return pl.pallas_call(
        _na_attn_kernel,
        grid=(N_HEADS_B, DEC_BATCH),
        in_specs=[pl.BlockSpec(memory_space=pltpu.SMEM), tok_spec, tok_spec, tok_spec, c_spec, c_spec],
        out_specs=tok_spec,
        out_shape=jax.ShapeDtypeStruct((N_LAT_TOK, Q_B_COLS), BF16),
        scratch_shapes=[pltpu.VMEM((n_groups, NA_GROUP_ROWS * GRID_W, NA_WIN_ROWS * GRID_W), F32)],
        compiler_params=_params(32, 2),
        name="neighborhood_attention",
    )(rel_bias_flat, qb, kb, vb, ck, cv)


def _outproj_kernel(oa_ref, ob_ref, wo_ref, x_ref, gate1_ref, shift2_ref, scale2_ref, g_ref, b_ref, wr_ref,
                    x1_ref, h2_ref, afft_ref):
    attn = _dot(oa_ref[...], wo_ref[:Q_A_COLS, :]) + _dot(ob_ref[...], wo_ref[Q_A_COLS:, :])
    x1 = _layer_norm(DEEPNORM_ALPHA * x_ref[...] + gate1_ref[...] * attn, g_ref[...], b_ref[...])
    x1_ref[...] = x1
    h2 = x1 * (1.0 + scale2_ref[...]) + shift2_ref[...]
    h2_ref[...] = h2.astype(BF16)
    logits = jnp.dot(h2, wr_ref[...], precision=lax.Precision.HIGHEST, preferred_element_type=F32)
    lt = logits.T[:N_EXPERTS, :]
    e = jnp.exp(lt - jnp.max(lt, axis=0, keepdims=True))
    afft_ref[...] = e / jnp.sum(e, axis=0, keepdims=True)


def _output_projection(oa, ob, wo_bf16, x, mod3, ln_g, ln_b, wr_pad, latent):
    n_tok = x.shape[0]
    tm = TOKEN_TILE
    row = lambda i: (i, 0)
    const = lambda i: (0, 0)
    return pl.pallas_call(
        _outproj_kernel,
        grid=(n_tok // tm,),
        in_specs=[pl.BlockSpec((tm, Q_A_COLS), row), pl.BlockSpec((tm, Q_B_COLS), row),
                  pl.BlockSpec((Q_A_COLS + Q_B_COLS, D_MODEL), const, pipeline_mode=pl.Buffered(1)),
                  pl.BlockSpec((tm, D_MODEL), row),
                  _mod_spec(2, latent), _mod_spec(3, latent), _mod_spec(4, latent),
                  pl.BlockSpec((1, D_MODEL), const), pl.BlockSpec((1, D_MODEL), const),
                  pl.BlockSpec((D_MODEL, V7X_LANES), const)],
        out_specs=[pl.BlockSpec((tm, D_MODEL), row), pl.BlockSpec((tm, D_MODEL), row),
                   pl.BlockSpec((N_EXPERTS, tm), lambda i: (0, i))],
        out_shape=[jax.ShapeDtypeStruct((n_tok, D_MODEL), F32), jax.ShapeDtypeStruct((n_tok, D_MODEL), BF16),
                   jax.ShapeDtypeStruct((N_EXPERTS, n_tok), F32)],
        compiler_params=_params(48, 1),
        name="outproj_latent" if latent else "outproj_context",
    )(oa, ob, wo_bf16, x, mod3, mod3, mod3, ln_g, ln_b, wr_pad)


def _pad_rows_to_lanes(a):
    return jnp.concatenate([a, jnp.zeros((V7X_LANES - a.shape[0], a.shape[1]), a.dtype)], axis=0)


def _dispatch_kernel(afft_ref, h2_ref, xs_ref, gate_ref, rank_ref, afft_t_ref, *, seq, cap, experts_per_dot):
    aff = afft_ref[...]
    afft_t_ref[...] = _pad_rows_to_lanes(aff).T
    sub = lax.broadcasted_iota(jnp.int32, (RANK_CHUNK, seq), 0)
    lane = lax.broadcasted_iota(jnp.int32, (RANK_CHUNK, seq), 1)
    ranks = []
    for e in range(N_EXPERTS):
        a_row = aff[e:e + 1, :]

        def body(c, cnt, e=e, a_row=a_row):
            t0 = pl.multiple_of(c * RANK_CHUNK, RANK_CHUNK)
            a_col = afft_t_ref[pl.ds(t0, RANK_CHUNK), e:e + 1]
            beats = (a_col > a_row) | ((a_col == a_row) & (sub + t0 < lane))
            return cnt + jnp.sum(jnp.where(beats, 1.0, 0.0), axis=0, keepdims=True)

        ranks.append(lax.fori_loop(0, seq // RANK_CHUNK, body, jnp.zeros((1, seq), F32)))
    rank = jnp.concatenate(ranks, axis=0)
    rank_ref[...] = _pad_rows_to_lanes(rank).T
    h2 = h2_ref[...]
    slot = lax.broadcasted_iota(jnp.int32, (cap, seq), 0).astype(F32)
    for c in range(N_EXPERTS // experts_per_dot):
        chunk = range(c * experts_per_dot, (c + 1) * experts_per_dot)
        onehot = jnp.concatenate([jnp.where(rank[e:e + 1, :] == slot, 1.0, 0.0) for e in chunk], axis=0)
        xs = _dot(onehot.astype(BF16), h2)
        for i, e in enumerate(chunk):
            xs_ref[e] = xs[i * cap:(i + 1) * cap].astype(BF16)
            gate_ref[e] = jnp.sum(onehot[i * cap:(i + 1) * cap] * aff[e:e + 1, :], axis=1, keepdims=True)


def _dispatch(afft, h2, n_req, seq, cap, experts_per_dot):
    return pl.pallas_call(
        functools.partial(_dispatch_kernel, seq=seq, cap=cap, experts_per_dot=experts_per_dot),
        grid=(n_req,),
        in_specs=[pl.BlockSpec((N_EXPERTS, seq), lambda r: (0, r)),
                  pl.BlockSpec((seq, D_MODEL), lambda r: (r, 0))],
        out_specs=[pl.BlockSpec((N_EXPERTS, cap, D_MODEL), lambda r: (0, r, 0)),
                   pl.BlockSpec((N_EXPERTS, cap, 1), lambda r: (0, r, 0)),
                   pl.BlockSpec((seq, V7X_LANES), lambda r: (r, 0))],
        out_shape=[jax.ShapeDtypeStruct((N_EXPERTS, n_req * cap, D_MODEL), BF16),
                   jax.ShapeDtypeStruct((N_EXPERTS, n_req * cap, 1), F32),
                   jax.ShapeDtypeStruct((n_req * seq, V7X_LANES), F32)],
        scratch_shapes=[pltpu.VMEM((seq, V7X_LANES), F32)],
        compiler_params=_params(48, 1),
        name=f"dispatch_s{seq}",
    )(afft, h2)


def _ffn_up_kernel(xc_ref, xl_ref, wg_ref, wu_ref, hc_ref, hl_ref):
    wg = wg_ref[...].astype(BF16)
    wu = wu_ref[...].astype(BF16)
    for x_ref, h_ref in ((xc_ref, hc_ref), (xl_ref, hl_ref)):
        x = x_ref[...]
        h_ref[...] = (jax.nn.silu(_dot(x, wg)) * _dot(x, wu)).astype(BF16)


def _ffn_down_kernel(hc_ref, hl_ref, wd_ref, gc_ref, gl_ref, yc_ref, yl_ref):
    wd = wd_ref[...].astype(BF16)
    for h_ref, g_ref, y_ref in ((hc_ref, gc_ref, yc_ref), (hl_ref, gl_ref, yl_ref)):
        y_ref[...] = (_dot(h_ref[...], wd) * g_ref[...]).astype(BF16)


def _expert_ffn(xs_ctx, xs_lat, gate_ctx, gate_lat, w_gate, w_up, w_down):
    rows_c, rows_l = xs_ctx.shape[1], xs_lat.shape[1]
    tn = FFN_COL_TILE
    tok_c = pl.BlockSpec((None, rows_c, D_MODEL), lambda e, j: (e, 0, 0))
    tok_l = pl.BlockSpec((None, rows_l, D_MODEL), lambda e, j: (e, 0, 0))
    w_spec = pl.BlockSpec((None, D_MODEL, tn), lambda e, j: (e, 0, j))
    out_c = pl.BlockSpec((None, rows_c, tn), lambda e, j: (e, 0, j))
    out_l = pl.BlockSpec((None, rows_l, tn), lambda e, j: (e, 0, j))
    shapes = [jax.ShapeDtypeStruct((N_EXPERTS, rows_c, D_FF), BF16),
              jax.ShapeDtypeStruct((N_EXPERTS, rows_l, D_FF), BF16)]
    hid_c, hid_l = pl.pallas_call(
        _ffn_up_kernel,
        grid=(N_EXPERTS, D_FF // tn),
        in_specs=[tok_c, tok_l, w_spec, w_spec], out_specs=[out_c, out_l], out_shape=shapes,
        compiler_params=_params(56, 2),
        name="ffn_up",
    )(xs_ctx, xs_lat, w_gate, w_up)
    g_c = pl.BlockSpec((None, rows_c, 1), lambda e, j: (e, 0, 0))
    g_l = pl.BlockSpec((None, rows_l, 1), lambda e, j: (e, 0, 0))
    return pl.pallas_call(
        _ffn_down_kernel,
        grid=(N_EXPERTS, D_MODEL // tn),
        in_specs=[tok_c, tok_l, w_spec, g_c, g_l], out_specs=[out_c, out_l], out_shape=shapes,
        compiler_params=_params(56, 2),
        name="ffn_down",
    )(hid_c, hid_l, w_down, gate_ctx, gate_lat)


def _combine_kernel(rank_ref, y_ref, x1_ref, gate2_ref, g_ref, b_ref, o_ref, *, cap, experts_per_dot):
    rk = jnp.minimum(rank_ref[...], float(cap)).astype(BF16)
    cols = experts_per_dot * cap
    shift = cap.bit_length() - 1
    erow = lax.broadcasted_iota(jnp.int32, (V7X_LANES, cols), 0)
    ecol = lax.shift_right_logical(lax.broadcasted_iota(jnp.int32, (V7X_LANES, cols), 1), shift)
    slot = (lax.broadcasted_iota(jnp.int32, (rk.shape[0], cols), 1) & (cap - 1)).astype(F32)
    ffn = None
    for c in range(N_EXPERTS // experts_per_dot):
        expand = jnp.where(erow == ecol + c * experts_per_dot, 1.0, 0.0).astype(BF16)
        onehot_t = jnp.where(_dot(rk, expand) == slot, 1.0, 0.0).astype(BF16)
        y = y_ref[c * experts_per_dot:(c + 1) * experts_per_dot].reshape(cols, D_MODEL)
        part = _dot(onehot_t, y)
        ffn = part if ffn is None else ffn + part
    o_ref[...] = _layer_norm(DEEPNORM_ALPHA * x1_ref[...] + gate2_ref[...] * ffn, g_ref[...], b_ref[...])


def _combine(rank_t, y, x1, mod3, ln_g, ln_b, n_req, seq, cap, experts_per_dot, latent):
    ts = TOKEN_TILE
    steps = seq // ts
    row = lambda r, t: (r * steps + t, 0)
    const = lambda r, t: (0, 0)
    mod_row = (lambda r, t: ((1 + r) * N_MOD + 5, 0, 0)) if latent else (lambda r, t: (5, 0, 0))
    return pl.pallas_call(
        functools.partial(_combine_kernel, cap=cap, experts_per_dot=experts_per_dot),
        grid=(n_req, steps),
        in_specs=[pl.BlockSpec((ts, V7X_LANES), row),
                  pl.BlockSpec((N_EXPERTS, cap, D_MODEL), lambda r, t: (0, r, 0)),
                  pl.BlockSpec((ts, D_MODEL), row),
                  pl.BlockSpec((None, 1, D_MODEL), mod_row),
                  pl.BlockSpec((1, D_MODEL), const), pl.BlockSpec((1, D_MODEL), const)],
        out_specs=pl.BlockSpec((ts, D_MODEL), row),
        out_shape=jax.ShapeDtypeStruct((n_req * seq, D_MODEL), F32),
        compiler_params=_params(48, 2),
        name=f"combine_s{seq}",
    )(rank_t, y, x1, mod3, ln_g, ln_b)


def _rope_tables():
    t = jnp.arange(DEC_SEQ)
    row = (t // GRID_W).astype(F32)
    col = (t % GRID_W).astype(F32)
    n_freq = HEAD_DIM // 4
    inv_freq = ROPE_THETA ** (-jnp.arange(n_freq, dtype=F32) / n_freq)
    ang = jnp.concatenate([row[:, None] * inv_freq, col[:, None] * inv_freq], axis=-1)
    cos = jnp.repeat(jnp.cos(ang), 2, axis=-1)
    sin = jnp.stack([-jnp.sin(ang), jnp.sin(ang)], axis=-1).reshape(DEC_SEQ, HEAD_DIM)
    return cos, sin


def kernel(x_prompt, x_sample, cache_k_a, cache_v_a, cache_k_b, cache_v_b, c, c_ctx, w_mod, b_mod, w_in, w_o,
           sink_a, rel_bias_b, w_router, w_gate, w_up, w_down, ln1_g, ln1_b, ln2_g, ln2_b):
    assert DEPTH == 1 and w_in.shape == (DEPTH, D_MODEL, IN_COLS)
    x_ctx = x_prompt.reshape(N_CTX_TOK, D_MODEL)
    x_lat = x_sample.reshape(N_LAT_TOK, D_MODEL)

    cond = jnp.concatenate([c_ctx[None, :], c, jnp.zeros((V7X_SUBLANES - N_COND, D_MODEL), F32)], axis=0)
    mod = _modulation(cond.T, w_mod[0], b_mod)
    mod3 = mod.reshape(V7X_SUBLANES * N_MOD, 1, D_MODEL)

    w_in_bf16 = w_in[0].astype(BF16)
    wo_bf16 = w_o[0].astype(BF16)
    wr_pad = jnp.pad(w_router[0], ((0, 0), (0, V7X_LANES - N_EXPERTS)))
    sink = sink_a[0]
    ln1 = (ln1_g, ln1_b)
    ln2 = (ln2_g, ln2_b)

    qa, ka, va, qb, kb, vb, ka_f, va_f, kb_f, vb_f = _input_projection(x_ctx, mod3, w_in_bf16, None, False)
    oa, ob = _context_attention(sink, qa, ka, va, qb, kb, vb)
    x1_ctx, h2_ctx, afft_ctx = _output_projection(oa, ob, wo_bf16, x_ctx, mod3, *ln1, wr_pad, False)
    xs_ctx, gate_ctx, rank_ctx = _dispatch(afft_ctx, h2_ctx, BATCH, SEQ, CAP_CTX, N_EXPERTS)

    qa, ka, va, qb, kb, vb = _input_projection(x_lat, mod3, w_in_bf16, _rope_tables(), True)
    oa = _window_attention(sink, qa, ka, va,
                           cache_k_a.reshape(DEC_BATCH * PAST_LEN, KV_A_COLS),
                           cache_v_a.reshape(DEC_BATCH * PAST_LEN, KV_A_COLS))
    ob = _neighborhood_attention(rel_bias_b.reshape(-1), qb, kb, vb,
                                 cache_k_b.reshape(DEC_BATCH * PAST_LEN, Q_B_COLS),
                                 cache_v_b.reshape(DEC_BATCH * PAST_LEN, Q_B_COLS))
    x1_lat, h2_lat, afft_lat = _output_projection(oa, ob, wo_bf16, x_lat, mod3, *ln1, wr_pad, True)
    xs_lat, gate_lat, rank_lat = _dispatch(afft_lat, h2_lat, DEC_BATCH, DEC_SEQ, CAP_LAT, 4)

    y_ctx, y_lat = _expert_ffn(xs_ctx, xs_lat, gate_ctx, gate_lat, w_gate[0], w_up[0], w_down[0])

    y_p = _combine(rank_ctx, y_ctx, x1_ctx, mod3, *ln2, BATCH, SEQ, CAP_CTX, N_EXPERTS, False)
    y_s = _combine(rank_lat, y_lat, x1_lat, mod3, *ln2, DEC_BATCH, DEC_SEQ, CAP_LAT, 4, True)

    kv_a_shape = (BATCH, DEPTH, SEQ, N_KV_A, HEAD_DIM)
    kv_b_shape = (BATCH, DEPTH, SEQ, N_HEADS_B, HEAD_DIM)
    return (y_p.reshape(BATCH, SEQ, D_MODEL), y_s.reshape(DEC_BATCH, DEC_SEQ, D_MODEL),
            ka_f.reshape(kv_a_shape), va_f.reshape(kv_a_shape), kb_f.reshape(kv_b_shape), vb_f.reshape(kv_b_shape))
```

```python
import functools

import jax
import jax.numpy as jnp
from jax import lax
from jax.experimental import pallas as pl
from jax.experimental.pallas import tpu as pltpu

D_MODEL = 2048
BATCH = 32
SEQ = 256
DEC_BATCH = 4
DEC_SEQ = 1024
PAST_LEN = 256
GRID_W = 64
HEAD_DIM = 128
N_HEADS_A = 8
N_KV_A = 2
GROUP_A = N_HEADS_A // N_KV_A
N_HEADS_B = 8
WINDOW_A = 128
BLOCK_A = 128
NA_ROWS = 8
NA_COLS = 16
N_EXPERTS = 16
EC_CAPACITY = 2
D_FF = D_MODEL
ROPE_THETA = 10000.0
LN_EPS = 1e-5
NEG_INF = -1e30
DEPTH = 1
Q_A_COLS = N_HEADS_A * HEAD_DIM
KV_A_COLS = N_KV_A * HEAD_DIM
Q_B_COLS = N_HEADS_B * HEAD_DIM
IN_COLS = Q_A_COLS + 2 * KV_A_COLS + 3 * Q_B_COLS
COL_STARTS = (0, Q_A_COLS, Q_A_COLS + KV_A_COLS, Q_A_COLS + 2 * KV_A_COLS,
              Q_A_COLS + 2 * KV_A_COLS + Q_B_COLS, Q_A_COLS + 2 * KV_A_COLS + 2 * Q_B_COLS, IN_COLS)
DEEPNORM_ALPHA = (2.0 * DEPTH) ** 0.25
ATTN_SCALE = HEAD_DIM ** -0.5

N_CTX_TOK = BATCH * SEQ
N_LAT_TOK = DEC_BATCH * DEC_SEQ
GRID_ROWS = DEC_SEQ // GRID_W
CAP_CTX = EC_CAPACITY * SEQ // N_EXPERTS
CAP_LAT = EC_CAPACITY * DEC_SEQ // N_EXPERTS
N_COND = 1 + DEC_BATCH
N_MOD = 6

V7X_LANES = 128
V7X_SUBLANES = 8
V7X_VMEM_BYTES = 64 * 1024 * 1024
MIB = 1024 * 1024

F32 = jnp.float32
BF16 = jnp.bfloat16
NT_DIMS = (((1,), (1,)), ((), ()))

TOKEN_TILE = 256
MOD_COL_TILE = 1024
FFN_COL_TILE = 512
OUTPROJ_TILE = 512
OUTPROJ_SUBTILE = 256
ROUTE_REQUESTS_CTX = 8
NA_GROUP_ROWS = 4
NA_WIN_ROWS = 12
NA_REL_ROWS = 2 * NA_ROWS - 1
NA_REL_COLS = 2 * NA_COLS - 1


def _params(vmem_mib, n_axes):
    return pltpu.CompilerParams(dimension_semantics=("arbitrary",) * n_axes,
                                vmem_limit_bytes=vmem_mib * MIB)


def _dot(a, b):
    return jnp.dot(a, b, preferred_element_type=F32)


def _dot_nt(a, b):
    return lax.dot_general(a, b, NT_DIMS, preferred_element_type=F32)


def _layer_norm(z, g, b):
    mu = jnp.mean(z, axis=-1, keepdims=True)
    zc = z - mu
    var = jnp.mean(zc * zc, axis=-1, keepdims=True)
    return zc * lax.rsqrt(var + LN_EPS) * g + b


def _mod_kernel(cond_ref, w_ref, b_ref, o_ref):
    s = jax.nn.silu(cond_ref[...]).astype(BF16)
    o_ref[...] = _dot(s, w_ref[...].astype(BF16)) + b_ref[...]


def _modulation(cond, w_mod, b_mod):
    n_out = N_MOD * D_MODEL
    return pl.pallas_call(
        _mod_kernel,
        grid=(n_out // MOD_COL_TILE,),
        in_specs=[pl.BlockSpec((V7X_SUBLANES, D_MODEL), lambda j: (0, 0)),
                  pl.BlockSpec((D_MODEL, MOD_COL_TILE), lambda j: (0, j)),
                  pl.BlockSpec((1, MOD_COL_TILE), lambda j: (0, j))],
        out_specs=pl.BlockSpec((V7X_SUBLANES, MOD_COL_TILE), lambda j: (0, j)),
        out_shape=jax.ShapeDtypeStruct((V7X_SUBLANES, n_out), F32),
        compiler_params=_params(40, 1),
        name="modulation",
    )(cond, w_mod, b_mod)


def _mod_spec(which, latent, tile=TOKEN_TILE):
    steps_per_request = DEC_SEQ // tile
    if latent:
        return pl.BlockSpec((None, 1, D_MODEL), lambda i: ((1 + i // steps_per_request) * N_MOD + which, 0, 0))
    return pl.BlockSpec((None, 1, D_MODEL), lambda i: (which, 0, 0))


def _rope(x, cos, sin, even):
    swapped = jnp.where(even, pltpu.roll(x, HEAD_DIM - 1, 1), pltpu.roll(x, 1, 1))
    return x * cos + swapped * sin


def _inproj_kernel(*refs, latent):
    if latent:
        (x_ref, shift_ref, scale_ref, w_ref, cos_ref, sin_ref,
         qa_o, ka_o, va_o, qb_o, kb_o, vb_o) = refs
    else:
        (x_ref, shift_ref, scale_ref, w_ref,
         qa_o, ka_o, va_o, qb_o, kb_o, vb_o, kaf_o, vaf_o, kbf_o, vbf_o) = refs
    h = (x_ref[...] * (1.0 + scale_ref[...]) + shift_ref[...]).astype(BF16)

    def proj(piece):
        return _dot(h, w_ref[:, COL_STARTS[piece]:COL_STARTS[piece + 1]])

    if latent:
        cos = cos_ref[...]
        sin = sin_ref[...]
        even = (lax.broadcasted_iota(jnp.int32, cos.shape, 1) & 1) == 0

    def heads(y, o_ref, n_heads, scale, rope):
        for hd in range(n_heads):
            cols = slice(hd * HEAD_DIM, (hd + 1) * HEAD_DIM)
            yh = y[:, cols]
            if rope:
                yh = _rope(yh, cos, sin, even)
            if scale != 1.0:
                yh = yh * scale
            o_ref[:, cols] = yh.astype(BF16)

    qa = proj(0)
    heads(qa, qa_o, N_HEADS_A, ATTN_SCALE, latent)
    ka = proj(1)
    heads(ka, ka_o, N_KV_A, 1.0, latent)
    va = proj(2)
    va_o[...] = va.astype(BF16)
    qb = proj(3)
    qb_o[...] = (qb * ATTN_SCALE).astype(BF16)
    kb = proj(4)
    kb_o[...] = kb.astype(BF16)
    vb = proj(5)
    vb_o[...] = vb.astype(BF16)
    if not latent:
        kaf_o[...] = ka
        vaf_o[...] = va
        kbf_o[...] = kb
        vbf_o[...] = vb


def _input_projection(x, mod3, w_in_bf16, rope_tables, latent):
    n_tok = x.shape[0]
    tm = TOKEN_TILE
    row = lambda i: (i, 0)
    in_specs = [pl.BlockSpec((tm, D_MODEL), row), _mod_spec(0, latent), _mod_spec(1, latent),
                pl.BlockSpec((D_MODEL, IN_COLS), lambda i: (0, 0), pipeline_mode=pl.Buffered(1))]
    args = [x, mod3, mod3, w_in_bf16]
    widths = [Q_A_COLS, KV_A_COLS, KV_A_COLS, Q_B_COLS, Q_B_COLS, Q_B_COLS]
    out_specs = [pl.BlockSpec((tm, w), row) for w in widths]
    out_shape = [jax.ShapeDtypeStruct((n_tok, w), BF16) for w in widths]
    if latent:
        steps_per_request = DEC_SEQ // tm
        in_specs += [pl.BlockSpec((tm, HEAD_DIM), lambda i: (i % steps_per_request, 0))] * 2
        args += list(rope_tables)
    else:
        kv_widths = [KV_A_COLS, KV_A_COLS, Q_B_COLS, Q_B_COLS]
        out_specs += [pl.BlockSpec((tm, w), row) for w in kv_widths]
        out_shape += [jax.ShapeDtypeStruct((n_tok, w), F32) for w in kv_widths]
    return pl.pallas_call(
        functools.partial(_inproj_kernel, latent=latent),
        grid=(n_tok // tm,),
        in_specs=in_specs, out_specs=out_specs, out_shape=out_shape,
        compiler_params=_params(52, 1),
        name="inproj_latent" if latent else "inproj_context",
    )(*args)


def _softmax_av(scores, values, sink=None):
    m = functools.reduce(jnp.maximum, [jnp.max(s, axis=-1, keepdims=True) for s in scores])
    if sink is not None:
        m = jnp.maximum(m, sink)
    probs = [jnp.exp(s - m) for s in scores]
    denom = functools.reduce(jnp.add, [jnp.sum(p, axis=-1, keepdims=True) for p in probs])
    if sink is not None:
        denom = denom + jnp.exp(sink - m)
    out = functools.reduce(jnp.add, [_dot(p.astype(BF16), v) for p, v in zip(probs, values)])
    return out / denom


def _stack_heads(q_ref, first_head, n_heads):
    return jnp.concatenate([q_ref[:, (first_head + g) * HEAD_DIM:(first_head + g + 1) * HEAD_DIM]
                            for g in range(n_heads)], axis=0)


def _sink_column(sink_ref, first_head, n_heads, rows):
    return jnp.concatenate([jnp.full((rows, 1), sink_ref[first_head + g], F32) for g in range(n_heads)], axis=0)


def _ctx_attn_kernel(sink_ref, qa_ref, ka_ref, va_ref, qb_ref, kb_ref, vb_ref, oa_ref, ob_ref):
    for kv in range(N_KV_A):
        cols = slice(kv * HEAD_DIM, (kv + 1) * HEAD_DIM)
        q = _stack_heads(qa_ref, kv * GROUP_A, GROUP_A)
        sink = _sink_column(sink_ref, kv * GROUP_A, GROUP_A, SEQ)
        o = _softmax_av([_dot_nt(q, ka_ref[:, cols])], [va_ref[:, cols]], sink)
        for g in range(GROUP_A):
            hd = kv * GROUP_A + g
            oa_ref[:, hd * HEAD_DIM:(hd + 1) * HEAD_DIM] = o[g * SEQ:(g + 1) * SEQ].astype(BF16)
    for hd in range(N_HEADS_B):
        cols = slice(hd * HEAD_DIM, (hd + 1) * HEAD_DIM)
        o = _softmax_av([_dot_nt(qb_ref[:, cols], kb_ref[:, cols])], [vb_ref[:, cols]])
        ob_ref[:, cols] = o.astype(BF16)


def _context_attention(sink, qa, ka, va, qb, kb, vb):
    row = lambda b: (b, 0)
    widths = [Q_A_COLS, KV_A_COLS, KV_A_COLS, Q_B_COLS, Q_B_COLS, Q_B_COLS]
    return pl.pallas_call(
        _ctx_attn_kernel,
        grid=(BATCH,),
        in_specs=[pl.BlockSpec(memory_space=pltpu.SMEM)] + [pl.BlockSpec((SEQ, w), row) for w in widths],
        out_specs=[pl.BlockSpec((SEQ, Q_A_COLS), row), pl.BlockSpec((SEQ, Q_B_COLS), row)],
        out_shape=[jax.ShapeDtypeStruct((N_CTX_TOK, Q_A_COLS), BF16),
                   jax.ShapeDtypeStruct((N_CTX_TOK, Q_B_COLS), BF16)],
        compiler_params=_params(32, 1),
        name="context_attention",
    )(sink, qa, ka, va, qb, kb, vb)


def _window_attn_kernel(sink_ref, q_ref, k_ref, v_ref, ck_ref, cv_ref, o_ref):
    kv = pl.program_id(1)
    n = pl.program_id(2)
    span = 3 * BLOCK_A
    start = pl.multiple_of(jnp.clip((n - 1) * BLOCK_A, 0, DEC_SEQ - span), BLOCK_A)
    q = _stack_heads(q_ref, 0, GROUP_A)
    rows = GROUP_A * BLOCK_A
    sink = jnp.concatenate([jnp.full((BLOCK_A, 1), sink_ref[kv * GROUP_A + g], F32) for g in range(GROUP_A)],
                           axis=0)
    qpos = n * BLOCK_A + (lax.broadcasted_iota(jnp.int32, (rows, span), 0) & (BLOCK_A - 1))
    kpos = start + lax.broadcasted_iota(jnp.int32, (rows, span), 1)
    valid = jnp.abs(qpos - kpos) <= WINDOW_A
    s_loc = jnp.where(valid, _dot_nt(q, k_ref[pl.ds(start, span), :]), NEG_INF)
    s_ctx = _dot_nt(q, ck_ref[...].astype(BF16))
    o = _softmax_av([s_loc, s_ctx], [v_ref[pl.ds(start, span), :], cv_ref[...].astype(BF16)], sink)
    for g in range(GROUP_A):
        o_ref[:, g * HEAD_DIM:(g + 1) * HEAD_DIM] = o[g * BLOCK_A:(g + 1) * BLOCK_A].astype(BF16)


def _window_attention(sink, qa, ka, va, ck, cv):
    nb = DEC_SEQ // BLOCK_A
    q_spec = pl.BlockSpec((BLOCK_A, GROUP_A * HEAD_DIM), lambda b, kv, n: (b * nb + n, kv))
    kv_spec = pl.BlockSpec((DEC_SEQ, HEAD_DIM), lambda b, kv, n: (b, kv))
    c_spec = pl.BlockSpec((PAST_LEN, HEAD_DIM), lambda b, kv, n: (b, kv))
    return pl.pallas_call(
        _window_attn_kernel,
        grid=(DEC_BATCH, N_KV_A, nb),
        in_specs=[pl.BlockSpec(memory_space=pltpu.SMEM), q_spec, kv_spec, kv_spec, c_spec, c_spec],
        out_specs=q_spec,
        out_shape=jax.ShapeDtypeStruct((N_LAT_TOK, Q_A_COLS), BF16),
        compiler_params=_params(32, 3),
        name="window_attention",
    )(sink, qa, ka, va, ck, cv)


def _na_row_start(rq):
    return min(max(rq - NA_ROWS // 2, 0), GRID_ROWS - NA_ROWS)


def _na_window_start(group):
    return min(max(group * NA_GROUP_ROWS - NA_ROWS // 2, 0), GRID_ROWS - NA_WIN_ROWS)


def _na_build_bias(rb_ref, bias_ref, head):
    shape = (GRID_W, 2 * GRID_W)
    lane = lax.broadcasted_iota(jnp.int32, shape, 1)
    qcol = lax.broadcasted_iota(jnp.int32, shape, 0)
    kcol = lane & (GRID_W - 1)
    left = lane < GRID_W
    dcol = kcol - qcol + (NA_COLS - 1)
    wstart = jnp.clip(qcol - NA_COLS // 2, 0, GRID_W - NA_COLS)
    col_ok = (kcol >= wstart) & (kcol < wstart + NA_COLS)
    base = head * (NA_REL_ROWS * NA_REL_COLS)
    toeplitz = []
    for a in range(NA_REL_ROWS):
        acc = jnp.zeros(shape, F32)
        for bb in range(NA_REL_COLS):
            acc = jnp.where(dcol == bb, rb_ref[base + a * NA_REL_COLS + bb], acc)
        toeplitz.append(acc)
    neg = jnp.full(shape, NEG_INF, F32)
    for g in range(GRID_ROWS // NA_GROUP_ROWS):
        wrow = _na_window_start(g)
        for qi in range(NA_GROUP_ROWS):
            rq = g * NA_GROUP_ROWS + qi
            row0 = _na_row_start(rq)
            for p in range(NA_WIN_ROWS // 2):
                rk0 = wrow + 2 * p
                ok0 = row0 <= rk0 < row0 + NA_ROWS
                ok1 = row0 <= rk0 + 1 < row0 + NA_ROWS
                a0 = rk0 - rq + NA_ROWS - 1
                if ok0 and ok1:
                    blk = jnp.where(col_ok, jnp.where(left, toeplitz[a0], toeplitz[a0 + 1]), neg)
                elif ok0:
                    blk = jnp.where(col_ok & left, toeplitz[a0], neg)
                elif ok1:
                    blk = jnp.where(col_ok & jnp.logical_not(left), toeplitz[a0 + 1], neg)
                else:
                    blk = neg
                bias_ref[g, qi * GRID_W:(qi + 1) * GRID_W, p * 2 * GRID_W:(p + 1) * 2 * GRID_W] = blk


def _na_attn_kernel(rb_ref, q_ref, k_ref, v_ref, ck_ref, cv_ref, o_ref, bias_ref):
    head = pl.program_id(0)

    @pl.when(pl.program_id(1) == 0)
    def _():
        _na_build_bias(rb_ref, bias_ref, head)

    ck = ck_ref[...].astype(BF16)
    cv = cv_ref[...].astype(BF16)
    q_rows = NA_GROUP_ROWS * GRID_W
    k_rows = NA_WIN_ROWS * GRID_W
    for g in range(GRID_ROWS // NA_GROUP_ROWS):
        k0 = _na_window_start(g) * GRID_W
        q = q_ref[g * q_rows:(g + 1) * q_rows, :]
        s_loc = _dot_nt(q, k_ref[k0:k0 + k_rows, :]) + bias_ref[g]
        s_ctx = _dot_nt(q, ck)
        o = _softmax_av([s_loc, s_ctx], [v_ref[k0:k0 + k_rows, :], cv])
        o_ref[g * q_rows:(g + 1) * q_rows, :] = o.astype(BF16)


def _neighborhood_attention(rel_bias_flat, qb, kb, vb, ck, cv):
    tok_spec = pl.BlockSpec((DEC_SEQ, HEAD_DIM), lambda h, b: (b, h))
    c_spec = pl.BlockSpec((PAST_LEN, HEAD_DIM), lambda h, b: (b, h))
    n_groups = GRID_ROWS // NA_GROUP_ROWS
    return pl.pallas_call(
        _na_attn_kernel,
        grid=(N_HEADS_B, DEC_BATCH),
        in_specs=[pl.BlockSpec(memory_space=pltpu.SMEM), tok_spec, tok_spec, tok_spec, c_spec, c_spec],
        out_specs=tok_spec,
        out_shape=jax.ShapeDtypeStruct((N_LAT_TOK, Q_B_COLS), BF16),
        scratch_shapes=[pltpu.VMEM((n_groups, NA_GROUP_ROWS * GRID_W, NA_WIN_ROWS * GRID_W), F32)],
        compiler_params=_params(32, 2),
        name="neighborhood_attention",
    )(rel_bias_flat, qb, kb, vb, ck, cv)


def _outproj_kernel(oa_ref, ob_ref, wo_ref, x_ref, gate1_ref, shift2_ref, scale2_ref, g_ref, b_ref, wr_ref,
                    x1_ref, h2_ref, aff_ref):
    lanes = aff_ref.shape[2]
    for r0 in range(0, OUTPROJ_TILE, OUTPROJ_SUBTILE):
        rows = slice(r0, r0 + OUTPROJ_SUBTILE)
        attn = _dot(oa_ref[rows, :], wo_ref[:Q_A_COLS, :]) + _dot(ob_ref[rows, :], wo_ref[Q_A_COLS:, :])
        x1 = _layer_norm(DEEPNORM_ALPHA * x_ref[rows, :] + gate1_ref[...] * attn, g_ref[...], b_ref[...])
        x1_ref[rows, :] = x1
        h2 = (x1 * (1.0 + scale2_ref[...]) + shift2_ref[...]).astype(BF16)
        h2_ref[rows, :] = h2
        lt = _dot(h2, wr_ref[...]).T[:N_EXPERTS, :]
        e = jnp.exp(lt - jnp.max(lt, axis=0, keepdims=True))
        req, l0 = divmod(r0, lanes)
        aff_ref[req, :, l0:l0 + OUTPROJ_SUBTILE] = e / jnp.sum(e, axis=0, keepdims=True)


def _output_projection(oa, ob, wo_bf16, x, mod3, ln_g, ln_b, wr_pad, seq, latent):
    n_tok = x.shape[0]
    tm = OUTPROJ_TILE
    row = lambda i: (i, 0)
    const = lambda i: (0, 0)
    if seq >= tm:
        tiles_per_request = seq // tm
        aff_spec = pl.BlockSpec((1, N_EXPERTS, tm), lambda i: (i // tiles_per_request, 0, i % tiles_per_request))
    else:
        aff_spec = pl.BlockSpec((tm // seq, N_EXPERTS, seq), lambda i: (i, 0, 0))
    return pl.pallas_call(
        _outproj_kernel,
        grid=(n_tok // tm,),
        in_specs=[pl.BlockSpec((tm, Q_A_COLS), row), pl.BlockSpec((tm, Q_B_COLS), row),
                  pl.BlockSpec((Q_A_COLS + Q_B_COLS, D_MODEL), const, pipeline_mode=pl.Buffered(1)),
                  pl.BlockSpec((tm, D_MODEL), row),
                  _mod_spec(2, latent, tm), _mod_spec(3, latent, tm), _mod_spec(4, latent, tm),
                  pl.BlockSpec((1, D_MODEL), const), pl.BlockSpec((1, D_MODEL), const),
                  pl.BlockSpec((D_MODEL, V7X_LANES), const)],
        out_specs=[pl.BlockSpec((tm, D_MODEL), row), pl.BlockSpec((tm, D_MODEL), row), aff_spec],
        out_shape=[jax.ShapeDtypeStruct((n_tok, D_MODEL), F32), jax.ShapeDtypeStruct((n_tok, D_MODEL), BF16),
                   jax.ShapeDtypeStruct((n_tok // seq, N_EXPERTS, seq), F32)],
        compiler_params=_params(56, 1),
        name="outproj_latent" if latent else "outproj_context",
    )(oa, ob, wo_bf16, x, mod3, mod3, mod3, ln_g, ln_b, wr_pad)


def _pad_rows_to_lanes(a):
    return jnp.concatenate([a, jnp.zeros((V7X_LANES - a.shape[0], a.shape[1]), a.dtype)], axis=0)


def _kth_largest(a, k):
    n = a.shape[1]
    lane = lax.broadcasted_iota(jnp.int32, (1, n), 1)
    x = a
    size = 2
    while size <= n:
        descending = (lane & size) == 0
        j = size // 2
        while j >= 1:
            lower = (lane & j) == 0
            partner = jnp.where(lower, pltpu.roll(x, n - j, 1), pltpu.roll(x, j, 1))
            x = jnp.where(lower == descending, jnp.maximum(x, partner), jnp.minimum(x, partner))
            j //= 2
        size *= 2
    return x[:, k - 1:k]


def _route_kernel(aff_ref, slot_ref, *, cap):
    n_req, n_exp, seq = aff_ref.shape
    key = aff_ref[...].reshape(n_req * n_exp, seq)
    tau = _kth_largest(key, cap)
    above = key > tau
    tied = key == tau
    need = cap - jnp.sum(jnp.where(above, 1.0, 0.0), axis=-1, keepdims=True)
    before = jnp.where(lax.broadcasted_iota(jnp.int32, (seq, seq), 0) < lax.broadcasted_iota(jnp.int32, (seq, seq), 1),
                       1.0, 0.0).astype(BF16)
    tied_before = _dot(jnp.where(tied, 1.0, 0.0).astype(BF16), before)
    chosen = above | (tied & (tied_before < need))
    slot = _dot(jnp.where(chosen, 1.0, 0.0).astype(BF16), before)
    slot_ref[...] = jnp.where(chosen, slot, float(cap)).reshape(n_req, n_exp, seq)


def _route(aff, cap, requests_per_step):
    n_req, _, seq = aff.shape
    spec = pl.BlockSpec((requests_per_step, N_EXPERTS, seq), lambda r: (r, 0, 0))
    return pl.pallas_call(
        functools.partial(_route_kernel, cap=cap),
        grid=(n_req // requests_per_step,),
        in_specs=[spec], out_specs=spec,
        out_shape=jax.ShapeDtypeStruct(aff.shape, F32),
        compiler_params=_params(32, 1),
        name=f"route_s{seq}",
    )(aff)


def _dispatch_kernel(slot_ref, aff_ref, h2_ref, xs_ref, gate_ref, slot_t_ref, *, cap, experts_per_dot):
    slot_of = slot_ref[...]
    aff = aff_ref[...]
    seq = slot_of.shape[1]
    slot_t_ref[...] = _pad_rows_to_lanes(slot_of).T
    h2 = h2_ref[...]
    slot = lax.broadcasted_iota(jnp.int32, (cap, seq), 0).astype(F32)
    for c in range(N_EXPERTS // experts_per_dot):
        chunk = range(c * experts_per_dot, (c + 1) * experts_per_dot)
        onehot = jnp.concatenate([jnp.where(slot_of[e:e + 1, :] == slot, 1.0, 0.0) for e in chunk], axis=0)
        xs = _dot(onehot.astype(BF16), h2)
        for i, e in enumerate(chunk):
            xs_ref[e] = xs[i * cap:(i + 1) * cap].astype(BF16)
            gate_ref[e] = jnp.sum(onehot[i * cap:(i + 1) * cap] * aff[e:e + 1, :], axis=1, keepdims=True)


def _dispatch(slot, aff, h2, cap, experts_per_dot):
    n_req, _, seq = aff.shape
    req_spec = pl.BlockSpec((None, N_EXPERTS, seq), lambda r: (r, 0, 0))
    return pl.pallas_call(
        functools.partial(_dispatch_kernel, cap=cap, experts_per_dot=experts_per_dot),
        grid=(n_req,),
        in_specs=[req_spec, req_spec, pl.BlockSpec((seq, D_MODEL), lambda r: (r, 0))],
        out_specs=[pl.BlockSpec((N_EXPERTS, cap, D_MODEL), lambda r: (0, r, 0)),
                   pl.BlockSpec((N_EXPERTS, cap, 1), lambda r: (0, r, 0)),
                   pl.BlockSpec((seq, V7X_LANES), lambda r: (r, 0))],
        out_shape=[jax.ShapeDtypeStruct((N_EXPERTS, n_req * cap, D_MODEL), BF16),
                   jax.ShapeDtypeStruct((N_EXPERTS, n_req * cap, 1), F32),
                   jax.ShapeDtypeStruct((n_req * seq, V7X_LANES), F32)],
        compiler_params=_params(48, 1),
        name=f"dispatch_s{seq}",
    )(slot, aff, h2)


def _ffn_up_kernel(xc_ref, xl_ref, wg_ref, wu_ref, hc_ref, hl_ref):
    wg = wg_ref[...].astype(BF16)
    wu = wu_ref[...].astype(BF16)
    for x_ref, h_ref in ((xc_ref, hc_ref), (xl_ref, hl_ref)):
        x = x_ref[...]
        h_ref[...] = (jax.nn.silu(_dot(x, wg)) * _dot(x, wu)).astype(BF16)


def _ffn_down_kernel(hc_ref, hl_ref, wd_ref, gc_ref, gl_ref, yc_ref, yl_ref):
    wd = wd_ref[...].astype(BF16)
    for h_ref, g_ref, y_ref in ((hc_ref, gc_ref, yc_ref), (hl_ref, gl_ref, yl_ref)):
        y_ref[...] = (_dot(h_ref[...], wd) * g_ref[...]).astype(BF16)


def _expert_ffn(xs_ctx, xs_lat, gate_ctx, gate_lat, w_gate, w_up, w_down):
    rows_c, rows_l = xs_ctx.shape[1], xs_lat.shape[1]
    tn = FFN_COL_TILE
    tok_c = pl.BlockSpec((None, rows_c, D_MODEL), lambda e, j: (e, 0, 0))
    tok_l = pl.BlockSpec((None, rows_l, D_MODEL), lambda e, j: (e, 0, 0))
    w_spec = pl.BlockSpec((None, D_MODEL, tn), lambda e, j: (e, 0, j))
    out_c = pl.BlockSpec((None, rows_c, tn), lambda e, j: (e, 0, j))
    out_l = pl.BlockSpec((None, rows_l, tn), lambda e, j: (e, 0, j))
    shapes = [jax.ShapeDtypeStruct((N_EXPERTS, rows_c, D_FF), BF16),
              jax.ShapeDtypeStruct((N_EXPERTS, rows_l, D_FF), BF16)]
    hid_c, hid_l = pl.pallas_call(
        _ffn_up_kernel,
        grid=(N_EXPERTS, D_FF // tn),
        in_specs=[tok_c, tok_l, w_spec, w_spec], out_specs=[out_c, out_l], out_shape=shapes,
        compiler_params=_params(56, 2),
        name="ffn_up",
    )(xs_ctx, xs_lat, w_gate, w_up)
    g_c = pl.BlockSpec((None, rows_c, 1), lambda e, j: (e, 0, 0))
    g_l = pl.BlockSpec((None, rows_l, 1), lambda e, j: (e, 0, 0))
    return pl.pallas_call(
        _ffn_down_kernel,
        grid=(N_EXPERTS, D_MODEL // tn),
        in_specs=[tok_c, tok_l, w_spec, g_c, g_l], out_specs=[out_c, out_l], out_shape=shapes,
        compiler_params=_params(56, 2),
        name="ffn_down",
    )(hid_c, hid_l, w_down, gate_ctx, gate_lat)


def _combine_kernel(slot_t_ref, y_ref, x1_ref, gate2_ref, g_ref, b_ref, o_ref, *, cap, experts_per_dot):
    rk = slot_t_ref[...].astype(BF16)
    cols = experts_per_dot * cap
    shift = cap.bit_length() - 1
    erow = lax.broadcasted_iota(jnp.int32, (V7X_LANES, cols), 0)
    ecol = lax.shift_right_logical(lax.broadcasted_iota(jnp.int32, (V7X_LANES, cols), 1), shift)
    slot = (lax.broadcasted_iota(jnp.int32, (rk.shape[0], cols), 1) & (cap - 1)).astype(F32)
    ffn = None
    for c in range(N_EXPERTS // experts_per_dot):
        expand = jnp.where(erow == ecol + c * experts_per_dot, 1.0, 0.0).astype(BF16)
        onehot_t = jnp.where(_dot(rk, expand) == slot, 1.0, 0.0).astype(BF16)
        y = y_ref[c * experts_per_dot:(c + 1) * experts_per_dot].reshape(cols, D_MODEL)
        part = _dot(onehot_t, y)
        ffn = part if ffn is None else ffn + part
    o_ref[...] = _layer_norm(DEEPNORM_ALPHA * x1_ref[...] + gate2_ref[...] * ffn, g_ref[...], b_ref[...])


def _combine(slot_t, y, x1, mod3, ln_g, ln_b, n_req, seq, cap, experts_per_dot, latent):
    ts = TOKEN_TILE
    steps = seq // ts
    row = lambda r, t: (r * steps + t, 0)
    const = lambda r, t: (0, 0)
    mod_row = (lambda r, t: ((1 + r) * N_MOD + 5, 0, 0)) if latent else (lambda r, t: (5, 0, 0))
    return pl.pallas_call(
        functools.partial(_combine_kernel, cap=cap, experts_per_dot=experts_per_dot),
        grid=(n_req, steps),
        in_specs=[pl.BlockSpec((ts, V7X_LANES), row),
                  pl.BlockSpec((N_EXPERTS, cap, D_MODEL), lambda r, t: (0, r, 0)),
                  pl.BlockSpec((ts, D_MODEL), row),
                  pl.BlockSpec((None, 1, D_MODEL), mod_row),
                  pl.BlockSpec((1, D_MODEL), const), pl.BlockSpec((1, D_MODEL), const)],
        out_specs=pl.BlockSpec((ts, D_MODEL), row),
        out_shape=jax.ShapeDtypeStruct((n_req * seq, D_MODEL), F32),
        compiler_params=_params(48, 2),
        name=f"combine_s{seq}",
    )(slot_t, y, x1, mod3, ln_g, ln_b)


def _rope_tables():
    t = jnp.arange(DEC_SEQ)
    row = (t // GRID_W).astype(F32)
    col = (t % GRID_W).astype(F32)
    n_freq = HEAD_DIM // 4
    inv_freq = ROPE_THETA ** (-jnp.arange(n_freq, dtype=F32) / n_freq)
    ang = jnp.concatenate([row[:, None] * inv_freq, col[:, None] * inv_freq], axis=-1)
    cos = jnp.repeat(jnp.cos(ang), 2, axis=-1)
    sin = jnp.stack([-jnp.sin(ang), jnp.sin(ang)], axis=-1).reshape(DEC_SEQ, HEAD_DIM)
    return cos, sin


def kernel(x_prompt, x_sample, cache_k_a, cache_v_a, cache_k_b, cache_v_b, c, c_ctx, w_mod, b_mod, w_in, w_o,
           sink_a, rel_bias_b, w_router, w_gate, w_up, w_down, ln1_g, ln1_b, ln2_g, ln2_b):
    assert DEPTH == 1 and w_in.shape == (DEPTH, D_MODEL, IN_COLS)
    x_ctx = x_prompt.reshape(N_CTX_TOK, D_MODEL)
    x_lat = x_sample.reshape(N_LAT_TOK, D_MODEL)

    cond = jnp.concatenate([c_ctx[None, :], c, jnp.zeros((V7X_SUBLANES - N_COND, D_MODEL), F32)], axis=0)
    mod = _modulation(cond, w_mod[0], b_mod)
    mod3 = mod.reshape(V7X_SUBLANES * N_MOD, 1, D_MODEL)

    w_in_bf16 = w_in[0].astype(BF16)
    wo_bf16 = w_o[0].astype(BF16)
    wr_pad = jnp.pad(w_router[0], ((0, 0), (0, V7X_LANES - N_EXPERTS))).astype(BF16)
    sink = sink_a[0]
    ln1 = (ln1_g, ln1_b)
    ln2 = (ln2_g, ln2_b)

    qa, ka, va, qb, kb, vb, ka_f, va_f, kb_f, vb_f = _input_projection(x_ctx, mod3, w_in_bf16, None, False)
    oa, ob = _context_attention(sink, qa, ka, va, qb, kb, vb)
    x1_ctx, h2_ctx, aff_ctx = _output_projection(oa, ob, wo_bf16, x_ctx, mod3, *ln1, wr_pad, SEQ, False)
    slot_ctx = _route(aff_ctx, CAP_CTX, ROUTE_REQUESTS_CTX)
    xs_ctx, gate_ctx, slot_t_ctx = _dispatch(slot_ctx, aff_ctx, h2_ctx, CAP_CTX, N_EXPERTS)

    qa, ka, va, qb, kb, vb = _input_projection(x_lat, mod3, w_in_bf16, _rope_tables(), True)
    oa = _window_attention(sink, qa, ka, va,
                           cache_k_a.reshape(DEC_BATCH * PAST_LEN, KV_A_COLS),
                           cache_v_a.reshape(DEC_BATCH * PAST_LEN, KV_A_COLS))
    ob = _neighborhood_attention(rel_bias_b.reshape(-1), qb, kb, vb,
                                 cache_k_b.reshape(DEC_BATCH * PAST_LEN, Q_B_COLS),
                                 cache_v_b.reshape(DEC_BATCH * PAST_LEN, Q_B_COLS))
    x1_lat, h2_lat, aff_lat = _output_projection(oa, ob, wo_bf16, x_lat, mod3, *ln1, wr_pad, DEC_SEQ, True)
    slot_lat = _route(aff_lat, CAP_LAT, DEC_BATCH)
    xs_lat, gate_lat, slot_t_lat = _dispatch(slot_lat, aff_lat, h2_lat, CAP_LAT, 4)

    y_ctx, y_lat = _expert_ffn(xs_ctx, xs_lat, gate_ctx, gate_lat, w_gate[0], w_up[0], w_down[0])

    y_p = _combine(slot_t_ctx, y_ctx, x1_ctx, mod3, *ln2, BATCH, SEQ, CAP_CTX, N_EXPERTS, False)
    y_s = _combine(slot_t_lat, y_lat, x1_lat, mod3, *ln2, DEC_BATCH, DEC_SEQ, CAP_LAT, 4, True)

    kv_a_shape = (BATCH, DEPTH, SEQ, N_KV_A, HEAD_DIM)
    kv_b_shape = (BATCH, DEPTH, SEQ, N_HEADS_B, HEAD_DIM)
    return (y_p.reshape(BATCH, SEQ, D_MODEL), y_s.reshape(DEC_BATCH, DEC_SEQ, D_MODEL),
            ka_f.reshape(kv_a_shape), va_f.reshape(kv_a_shape), kb_f.reshape(kv_b_shape), vb_f.reshape(kv_b_shape))
```

```python
import functools

import jax
import jax.numpy as jnp
from jax import lax
from jax.experimental import pallas as pl
from jax.experimental.pallas import tpu as pltpu

D_MODEL = 2048
BATCH = 32
SEQ = 256
DEC_BATCH = 4
DEC_SEQ = 1024
PAST_LEN = 256
GRID_W = 64
HEAD_DIM = 128
N_HEADS_A = 8
N_KV_A = 2
GROUP_A = N_HEADS_A // N_KV_A
N_HEADS_B = 8
WINDOW_A = 128
BLOCK_A = 128
NA_ROWS = 8
NA_COLS = 16
N_EXPERTS = 16
EC_CAPACITY = 2
D_FF = D_MODEL
ROPE_THETA = 10000.0
LN_EPS = 1e-5
NEG_INF = -1e30
DEPTH = 1
Q_A_COLS = N_HEADS_A * HEAD_DIM
KV_A_COLS = N_KV_A * HEAD_DIM
Q_B_COLS = N_HEADS_B * HEAD_DIM
IN_COLS = Q_A_COLS + 2 * KV_A_COLS + 3 * Q_B_COLS
COL_STARTS = (0, Q_A_COLS, Q_A_COLS + KV_A_COLS, Q_A_COLS + 2 * KV_A_COLS,
              Q_A_COLS + 2 * KV_A_COLS + Q_B_COLS, Q_A_COLS + 2 * KV_A_COLS + 2 * Q_B_COLS, IN_COLS)
DEEPNORM_ALPHA = (2.0 * DEPTH) ** 0.25
ATTN_SCALE = HEAD_DIM ** -0.5

N_CTX_TOK = BATCH * SEQ
N_LAT_TOK = DEC_BATCH * DEC_SEQ
GRID_ROWS = DEC_SEQ // GRID_W
CAP_CTX = EC_CAPACITY * SEQ // N_EXPERTS
CAP_LAT = EC_CAPACITY * DEC_SEQ // N_EXPERTS
N_COND = 1 + DEC_BATCH
N_MOD = 6

V7X_LANES = 128
V7X_SUBLANES = 8
V7X_VMEM_BYTES = 64 * 1024 * 1024
MIB = 1024 * 1024

F32 = jnp.float32
BF16 = jnp.bfloat16
NT_DIMS = (((1,), (1,)), ((), ()))

TOKEN_TILE = 256
MOD_COL_TILE = 1024
FFN_COL_TILE = 512
OUTPROJ_TILE = 512
OUTPROJ_SUBTILE = 256
ROUTE_REQUESTS_CTX = 8
NA_GROUP_ROWS = 4
NA_WIN_ROWS = 12
NA_REL_ROWS = 2 * NA_ROWS - 1
NA_REL_COLS = 2 * NA_COLS - 1


def _params(vmem_mib, n_axes):
    return pltpu.CompilerParams(dimension_semantics=("arbitrary",) * n_axes,
                                vmem_limit_bytes=vmem_mib * MIB)


def _dot(a, b):
    return jnp.dot(a, b, preferred_element_type=F32)


def _dot_nt(a, b):
    return lax.dot_general(a, b, NT_DIMS, preferred_element_type=F32)


def _layer_norm(z, g, b):
    mu = jnp.mean(z, axis=-1, keepdims=True)
    zc = z - mu
    var = jnp.mean(zc * zc, axis=-1, keepdims=True)
    return zc * lax.rsqrt(var + LN_EPS) * g + b


def _mod_kernel(cond_ref, w_ref, b_ref, o_ref):
    s = jax.nn.silu(cond_ref[...]).astype(BF16)
    o_ref[...] = _dot(s, w_ref[...].astype(BF16)) + b_ref[...]


def _modulation(cond, w_mod, b_mod):
    n_out = N_MOD * D_MODEL
    return pl.pallas_call(
        _mod_kernel,
        grid=(n_out // MOD_COL_TILE,),
        in_specs=[pl.BlockSpec((V7X_SUBLANES, D_MODEL), lambda j: (0, 0)),
                  pl.BlockSpec((D_MODEL, MOD_COL_TILE), lambda j: (0, j)),
                  pl.BlockSpec((1, MOD_COL_TILE), lambda j: (0, j))],
        out_specs=pl.BlockSpec((V7X_SUBLANES, MOD_COL_TILE), lambda j: (0, j)),
        out_shape=jax.ShapeDtypeStruct((V7X_SUBLANES, n_out), F32),
        compiler_params=_params(40, 1),
        name="modulation",
    )(cond, w_mod, b_mod)


def _mod_spec(which, latent, tile=TOKEN_TILE):
    steps_per_request = DEC_SEQ // tile
    if latent:
        return pl.BlockSpec((None, 1, D_MODEL), lambda i: ((1 + i // steps_per_request) * N_MOD + which, 0, 0))
    return pl.BlockSpec((None, 1, D_MODEL), lambda i: (which, 0, 0))


def _rope(x, cos, sin, even):
    swapped = jnp.where(even, pltpu.roll(x, HEAD_DIM - 1, 1), pltpu.roll(x, 1, 1))
    return x * cos + swapped * sin


def _inproj_kernel(*refs, latent):
    if latent:
        (x_ref, shift_ref, scale_ref, w_ref, cos_ref, sin_ref,
         qa_o, ka_o, va_o, qb_o, kb_o, vb_o) = refs
    else:
        (x_ref, shift_ref, scale_ref, w_ref,
         qa_o, ka_o, va_o, qb_o, kb_o, vb_o, kaf_o, vaf_o, kbf_o, vbf_o) = refs
    h = (x_ref[...] * (1.0 + scale_ref[...]) + shift_ref[...]).astype(BF16)

    def proj(piece):
        return _dot(h, w_ref[:, COL_STARTS[piece]:COL_STARTS[piece + 1]])

    if latent:
        cos = cos_ref[...]
        sin = sin_ref[...]
        even = (lax.broadcasted_iota(jnp.int32, cos.shape, 1) & 1) == 0

    def heads(y, o_ref, n_heads, scale, rope):
        for hd in range(n_heads):
            cols = slice(hd * HEAD_DIM, (hd + 1) * HEAD_DIM)
            yh = y[:, cols]
            if rope:
                yh = _rope(yh, cos, sin, even)
            if scale != 1.0:
                yh = yh * scale
            o_ref[:, cols] = yh.astype(BF16)

    qa = proj(0)
    heads(qa, qa_o, N_HEADS_A, ATTN_SCALE, latent)
    ka = proj(1)
    heads(ka, ka_o, N_KV_A, 1.0, latent)
    va = proj(2)
    va_o[...] = va.astype(BF16)
    qb = proj(3)
    qb_o[...] = (qb * ATTN_SCALE).astype(BF16)
    kb = proj(4)
    kb_o[...] = kb.astype(BF16)
    vb = proj(5)
    vb_o[...] = vb.astype(BF16)
    if not latent:
        kaf_o[...] = ka
        vaf_o[...] = va
        kbf_o[...] = kb
        vbf_o[...] = vb


def _input_projection(x, mod3, w_in_bf16, rope_tables, latent):
    n_tok = x.shape[0]
    tm = TOKEN_TILE
    row = lambda i: (i, 0)
    in_specs = [pl.BlockSpec((tm, D_MODEL), row), _mod_spec(0, latent), _mod_spec(1, latent),
                pl.BlockSpec((D_MODEL, IN_COLS), lambda i: (0, 0), pipeline_mode=pl.Buffered(1))]
    args = [x, mod3, mod3, w_in_bf16]
    widths = [Q_A_COLS, KV_A_COLS, KV_A_COLS, Q_B_COLS, Q_B_COLS, Q_B_COLS]
    out_specs = [pl.BlockSpec((tm, w), row) for w in widths]
    out_shape = [jax.ShapeDtypeStruct((n_tok, w), BF16) for w in widths]
    if latent:
        steps_per_request = DEC_SEQ // tm
        in_specs += [pl.BlockSpec((tm, HEAD_DIM), lambda i: (i % steps_per_request, 0))] * 2
        args += list(rope_tables)
    else:
        kv_widths = [KV_A_COLS, KV_A_COLS, Q_B_COLS, Q_B_COLS]
        out_specs += [pl.BlockSpec((tm, w), row) for w in kv_widths]
        out_shape += [jax.ShapeDtypeStruct((n_tok, w), F32) for w in kv_widths]
    return pl.pallas_call(
        functools.partial(_inproj_kernel, latent=latent),
        grid=(n_tok // tm,),
        in_specs=in_specs, out_specs=out_specs, out_shape=out_shape,
        compiler_params=_params(52, 1),
        name="inproj_latent" if latent else "inproj_context",
    )(*args)


def _with_ones(v):
    return jnp.concatenate([v, jnp.ones((v.shape[0], V7X_LANES), v.dtype)], axis=1)


def _softmax_av(scores, values, sink=None):
    m = functools.reduce(jnp.maximum, [jnp.max(s, axis=-1, keepdims=True) for s in scores])
    if sink is not None:
        m = jnp.maximum(m, sink)
    acc = functools.reduce(jnp.add, [_dot(jnp.exp(s - m).astype(BF16), _with_ones(v))
                                     for s, v in zip(scores, values)])
    denom = acc[:, HEAD_DIM:]
    if sink is not None:
        denom = denom + jnp.exp(sink - m)
    return acc[:, :HEAD_DIM] / denom


def _stack_heads(q_ref, first_head, n_heads):
    return jnp.concatenate([q_ref[:, (first_head + g) * HEAD_DIM:(first_head + g + 1) * HEAD_DIM]
                            for g in range(n_heads)], axis=0)


def _sink_column(sink_ref, first_head, n_heads, rows):
    return jnp.concatenate([jnp.full((rows, 1), sink_ref[first_head + g], F32) for g in range(n_heads)], axis=0)


def _ctx_attn_kernel(sink_ref, qa_ref, ka_ref, va_ref, qb_ref, kb_ref, vb_ref, oa_ref, ob_ref):
    for kv in range(N_KV_A):
        cols = slice(kv * HEAD_DIM, (kv + 1) * HEAD_DIM)
        q = _stack_heads(qa_ref, kv * GROUP_A, GROUP_A)
        sink = _sink_column(sink_ref, kv * GROUP_A, GROUP_A, SEQ)
        o = _softmax_av([_dot_nt(q, ka_ref[:, cols])], [va_ref[:, cols]], sink)
        for g in range(GROUP_A):
            hd = kv * GROUP_A + g
            oa_ref[:, hd * HEAD_DIM:(hd + 1) * HEAD_DIM] = o[g * SEQ:(g + 1) * SEQ].astype(BF16)
    for hd in range(N_HEADS_B):
        cols = slice(hd * HEAD_DIM, (hd + 1) * HEAD_DIM)
        o = _softmax_av([_dot_nt(qb_ref[:, cols], kb_ref[:, cols])], [vb_ref[:, cols]])
        ob_ref[:, cols] = o.astype(BF16)


def _context_attention(sink, qa, ka, va, qb, kb, vb):
    row = lambda b: (b, 0)
    widths = [Q_A_COLS, KV_A_COLS, KV_A_COLS, Q_B_COLS, Q_B_COLS, Q_B_COLS]
    return pl.pallas_call(
        _ctx_attn_kernel,
        grid=(BATCH,),
        in_specs=[pl.BlockSpec(memory_space=pltpu.SMEM)] + [pl.BlockSpec((SEQ, w), row) for w in widths],
        out_specs=[pl.BlockSpec((SEQ, Q_A_COLS), row), pl.BlockSpec((SEQ, Q_B_COLS), row)],
        out_shape=[jax.ShapeDtypeStruct((N_CTX_TOK, Q_A_COLS), BF16),
                   jax.ShapeDtypeStruct((N_CTX_TOK, Q_B_COLS), BF16)],
        compiler_params=_params(32, 1),
        name="context_attention",
    )(sink, qa, ka, va, qb, kb, vb)


def _window_attn_kernel(sink_ref, q_ref, k_ref, v_ref, ck_ref, cv_ref, o_ref):
    kv = pl.program_id(1)
    span = 3 * BLOCK_A
    rows = GROUP_A * BLOCK_A
    sink = jnp.concatenate([jnp.full((BLOCK_A, 1), sink_ref[kv * GROUP_A + g], F32) for g in range(GROUP_A)],
                           axis=0)
    ck = ck_ref[...].astype(BF16)
    cv = cv_ref[...].astype(BF16)
    band = (lax.broadcasted_iota(jnp.int32, (rows, span), 1)
            - (lax.broadcasted_iota(jnp.int32, (rows, span), 0) & (BLOCK_A - 1)))
    masks = {}
    for n in range(DEC_SEQ // BLOCK_A):
        start = min(max((n - 1) * BLOCK_A, 0), DEC_SEQ - span)
        offset = start - n * BLOCK_A
        if offset not in masks:
            masks[offset] = jnp.where(jnp.abs(band + offset) <= WINDOW_A, 0.0, NEG_INF)
        q = jnp.concatenate([q_ref[n * BLOCK_A:(n + 1) * BLOCK_A, g * HEAD_DIM:(g + 1) * HEAD_DIM]
                             for g in range(GROUP_A)], axis=0)
        s_loc = _dot_nt(q, k_ref[start:start + span, :]) + masks[offset]
        o = _softmax_av([s_loc, _dot_nt(q, ck)], [v_ref[start:start + span, :], cv], sink)
        for g in range(GROUP_A):
            o_ref[n * BLOCK_A:(n + 1) * BLOCK_A, g * HEAD_DIM:(g + 1) * HEAD_DIM] = (
                o[g * BLOCK_A:(g + 1) * BLOCK_A].astype(BF16))


def _window_attention(sink, qa, ka, va, ck, cv):
    q_spec = pl.BlockSpec((DEC_SEQ, GROUP_A * HEAD_DIM), lambda b, kv: (b, kv))
    kv_spec = pl.BlockSpec((DEC_SEQ, HEAD_DIM), lambda b, kv: (b, kv))
    c_spec = pl.BlockSpec((PAST_LEN, HEAD_DIM), lambda b, kv: (b, kv))
    return pl.pallas_call(
        _window_attn_kernel,
        grid=(DEC_BATCH, N_KV_A),
        in_specs=[pl.BlockSpec(memory_space=pltpu.SMEM), q_spec, kv_spec, kv_spec, c_spec, c_spec],
        out_specs=q_spec,
        out_shape=jax.ShapeDtypeStruct((N_LAT_TOK, Q_A_COLS), BF16),
        compiler_params=_params(32, 2),
        name="window_attention",
    )(sink, qa, ka, va, ck, cv)


def _na_row_start(rq):
    return min(max(rq - NA_ROWS // 2, 0), GRID_ROWS - NA_ROWS)


def _na_window_start(group):
    return min(max(group * NA_GROUP_ROWS - NA_ROWS // 2, 0), GRID_ROWS - NA_WIN_ROWS)


def _na_build_bias(rb_ref, bias_ref, head):
    shape = (GRID_W, 2 * GRID_W)
    lane = lax.broadcasted_iota(jnp.int32, shape, 1)
    qcol = lax.broadcasted_iota(jnp.int32, shape, 0)
    kcol = lane & (GRID_W - 1)
    left = lane < GRID_W
    dcol = kcol - qcol + (NA_COLS - 1)
    wstart = jnp.clip(qcol - NA_COLS // 2, 0, GRID_W - NA_COLS)
    col_ok = (kcol >= wstart) & (kcol < wstart + NA_COLS)
    base = head * (NA_REL_ROWS * NA_REL_COLS)
    toeplitz = []
    for a in range(NA_REL_ROWS):
        acc = jnp.zeros(shape, F32)
        for bb in range(NA_REL_COLS):
            acc = jnp.where(dcol == bb, rb_ref[base + a * NA_REL_COLS + bb], acc)
        toeplitz.append(acc)
    neg = jnp.full(shape, NEG_INF, F32)
    for g in range(GRID_ROWS // NA_GROUP_ROWS):
        wrow = _na_window_start(g)
        for qi in range(NA_GROUP_ROWS):
            rq = g * NA_GROUP_ROWS + qi
            row0 = _na_row_start(rq)
            for p in range(NA_WIN_ROWS // 2):
                rk0 = wrow + 2 * p
                ok0 = row0 <= rk0 < row0 + NA_ROWS
                ok1 = row0 <= rk0 + 1 < row0 + NA_ROWS
                a0 = rk0 - rq + NA_ROWS - 1
                if ok0 and ok1:
                    blk = jnp.where(col_ok, jnp.where(left, toeplitz[a0], toeplitz[a0 + 1]), neg)
                elif ok0:
                    blk = jnp.where(col_ok & left, toeplitz[a0], neg)
                elif ok1:
                    blk = jnp.where(col_ok & jnp.logical_not(left), toeplitz[a0 + 1], neg)
                else:
                    blk = neg
                bias_ref[g, qi * GRID_W:(qi + 1) * GRID_W, p * 2 * GRID_W:(p + 1) * 2 * GRID_W] = blk


def _na_attn_kernel(rb_ref, q_ref, k_ref, v_ref, ck_ref, cv_ref, o_ref, bias_ref):
    head = pl.program_id(0)

    @pl.when(pl.program_id(1) == 0)
    def _():
        _na_build_bias(rb_ref, bias_ref, head)

    ck = ck_ref[...].astype(BF16)
    cv = cv_ref[...].astype(BF16)
    q_rows = NA_GROUP_ROWS * GRID_W
    k_rows = NA_WIN_ROWS * GRID_W
    for g in range(GRID_ROWS // NA_GROUP_ROWS):
        k0 = _na_window_start(g) * GRID_W
        q = q_ref[g * q_rows:(g + 1) * q_rows, :]
        s_loc = _dot_nt(q, k_ref[k0:k0 + k_rows, :]) + bias_ref[g]
        s_ctx = _dot_nt(q, ck)
        o = _softmax_av([s_loc, s_ctx], [v_ref[k0:k0 + k_rows, :], cv])
        o_ref[g * q_rows:(g + 1) * q_rows, :] = o.astype(BF16)


def _neighborhood_attention(rel_bias_flat, qb, kb, vb, ck, cv):
    tok_spec = pl.BlockSpec((DEC_SEQ, HEAD_DIM), lambda h, b: (b, h))
    c_spec = pl.BlockSpec((PAST_LEN, HEAD_DIM), lambda h, b: (b, h))
    n_groups = GRID_ROWS // NA_GROUP_ROWS
    return pl.pallas_call(
        _na_attn_kernel,
        grid=(N_HEADS_B, DEC_BATCH),
        in_specs=[pl.BlockSpec(memory_space=pltpu.SMEM), tok_spec, tok_spec, tok_spec, c_spec, c_spec],
        out_specs=tok_spec,
        out_shape=jax.ShapeDtypeStruct((N_LAT_TOK, Q_B_COLS), BF16),
        scratch_shapes=[pltpu.VMEM((n_groups, NA_GROUP_ROWS * GRID_W, NA_WIN_ROWS * GRID_W), F32)],
        compiler_params=_params(32, 2),
        name="neighborhood_attention",
    )(rel_bias_flat, qb, kb, vb, ck, cv)


def _outproj_kernel(oa_ref, ob_ref, wo_ref, x_ref, gate1_ref, shift2_ref, scale2_ref, g_ref, b_ref, wr_ref,
                    x1_ref, h2_ref, aff_ref):
    lanes = aff_ref.shape[2]
    for r0 in range(0, OUTPROJ_TILE, OUTPROJ_SUBTILE):
        rows = slice(r0, r0 + OUTPROJ_SUBTILE)
        attn = _dot(oa_ref[rows, :], wo_ref[:Q_A_COLS, :]) + _dot(ob_ref[rows, :], wo_ref[Q_A_COLS:, :])
        x1 = _layer_norm(DEEPNORM_ALPHA * x_ref[rows, :] + gate1_ref[...] * attn, g_ref[...], b_ref[...])
        x1_ref[rows, :] = x1
        h2 = (x1 * (1.0 + scale2_ref[...]) + shift2_ref[...]).astype(BF16)
        h2_ref[rows, :] = h2
        lt = _dot(h2, wr_ref[...]).T[:N_EXPERTS, :]
        e = jnp.exp(lt - jnp.max(lt, axis=0, keepdims=True))
        req, l0 = divmod(r0, lanes)
        aff_ref[req, :, l0:l0 + OUTPROJ_SUBTILE] = e / jnp.sum(e, axis=0, keepdims=True)


def _output_projection(oa, ob, wo_bf16, x, mod3, ln_g, ln_b, wr_pad, seq, latent):
    n_tok = x.shape[0]
    tm = OUTPROJ_TILE
    row = lambda i: (i, 0)
    const = lambda i: (0, 0)
    if seq >= tm:
        tiles_per_request = seq // tm
        aff_spec = pl.BlockSpec((1, N_EXPERTS, tm), lambda i: (i // tiles_per_request, 0, i % tiles_per_request))
    else:
        aff_spec = pl.BlockSpec((tm // seq, N_EXPERTS, seq), lambda i: (i, 0, 0))
    return pl.pallas_call(
        _outproj_kernel,
        grid=(n_tok // tm,),
        in_specs=[pl.BlockSpec((tm, Q_A_COLS), row), pl.BlockSpec((tm, Q_B_COLS), row),
                  pl.BlockSpec((Q_A_COLS + Q_B_COLS, D_MODEL), const, pipeline_mode=pl.Buffered(1)),
                  pl.BlockSpec((tm, D_MODEL), row),
                  _mod_spec(2, latent, tm), _mod_spec(3, latent, tm), _mod_spec(4, latent, tm),
                  pl.BlockSpec((1, D_MODEL), const), pl.BlockSpec((1, D_MODEL), const),
                  pl.BlockSpec((D_MODEL, V7X_LANES), const)],
        out_specs=[pl.BlockSpec((tm, D_MODEL), row), pl.BlockSpec((tm, D_MODEL), row), aff_spec],
        out_shape=[jax.ShapeDtypeStruct((n_tok, D_MODEL), F32), jax.ShapeDtypeStruct((n_tok, D_MODEL), BF16),
                   jax.ShapeDtypeStruct((n_tok // seq, N_EXPERTS, seq), F32)],
        compiler_params=_params(56, 1),
        name="outproj_latent" if latent else "outproj_context",
    )(oa, ob, wo_bf16, x, mod3, mod3, mod3, ln_g, ln_b, wr_pad)


def _pad_rows_to_lanes(a):
    return jnp.concatenate([a, jnp.zeros((V7X_LANES - a.shape[0], a.shape[1]), a.dtype)], axis=0)


def _kth_largest(a, k):
    n = a.shape[1]
    lane = lax.broadcasted_iota(jnp.int32, (1, n), 1)
    x = a
    size = 2
    while size <= n:
        descending = (lane & size) == 0
        j = size // 2
        while j >= 1:
            lower = (lane & j) == 0
            partner = jnp.where(lower, pltpu.roll(x, n - j, 1), pltpu.roll(x, j, 1))
            x = jnp.where(lower == descending, jnp.maximum(x, partner), jnp.minimum(x, partner))
            j //= 2
        size *= 2
    return x[:, k - 1:k]


def _route_kernel(aff_ref, slot_ref, *, cap):
    n_req, n_exp, seq = aff_ref.shape
    key = aff_ref[...].reshape(n_req * n_exp, seq)
    tau = _kth_largest(key, cap)
    above = key > tau
    tied = key == tau
    need = cap - jnp.sum(jnp.where(above, 1.0, 0.0), axis=-1, keepdims=True)
    before = jnp.where(lax.broadcasted_iota(jnp.int32, (seq, seq), 0) < lax.broadcasted_iota(jnp.int32, (seq, seq), 1),
                       1.0, 0.0).astype(BF16)
    tied_before = _dot(jnp.where(tied, 1.0, 0.0).astype(BF16), before)
    chosen = above | (tied & (tied_before < need))
    slot = _dot(jnp.where(chosen, 1.0, 0.0).astype(BF16), before)
    slot_ref[...] = jnp.where(chosen, slot, float(cap)).reshape(n_req, n_exp, seq)


def _route(aff, cap, requests_per_step):
    n_req, _, seq = aff.shape
    spec = pl.BlockSpec((requests_per_step, N_EXPERTS, seq), lambda r: (r, 0, 0))
    return pl.pallas_call(
        functools.partial(_route_kernel, cap=cap),
        grid=(n_req // requests_per_step,),
        in_specs=[spec], out_specs=spec,
        out_shape=jax.ShapeDtypeStruct(aff.shape, F32),
        compiler_params=_params(32, 1),
        name=f"route_s{seq}",
    )(aff)


def _dispatch_kernel(slot_ref, aff_ref, h2_ref, xs_ref, gate_ref, slot_t_ref, *, cap, experts_per_dot):
    slot_of = slot_ref[...]
    aff = aff_ref[...]
    seq = slot_of.shape[1]
    slot_t_ref[...] = _pad_rows_to_lanes(slot_of).T
    h2 = h2_ref[...]
    slot = lax.broadcasted_iota(jnp.int32, (cap, seq), 0).astype(F32)
    for c in range(N_EXPERTS // experts_per_dot):
        chunk = range(c * experts_per_dot, (c + 1) * experts_per_dot)
        onehot = jnp.concatenate([jnp.where(slot_of[e:e + 1, :] == slot, 1.0, 0.0) for e in chunk], axis=0)
        xs = _dot(onehot.astype(BF16), h2)
        for i, e in enumerate(chunk):
            xs_ref[e] = xs[i * cap:(i + 1) * cap].astype(BF16)
            gate = jnp.sum(onehot[i * cap:(i + 1) * cap] * aff[e:e + 1, :], axis=1, keepdims=True)
            gate_ref[e] = jnp.broadcast_to(gate, (cap, V7X_LANES))


def _dispatch(slot, aff, h2, cap, experts_per_dot):
    n_req, _, seq = aff.shape
    req_spec = pl.BlockSpec((None, N_EXPERTS, seq), lambda r: (r, 0, 0))
    return pl.pallas_call(
        functools.partial(_dispatch_kernel, cap=cap, experts_per_dot=experts_per_dot),
        grid=(n_req,),
        in_specs=[req_spec, req_spec, pl.BlockSpec((seq, D_MODEL), lambda r: (r, 0))],
        out_specs=[pl.BlockSpec((N_EXPERTS, cap, D_MODEL), lambda r: (0, r, 0)),
                   pl.BlockSpec((N_EXPERTS, cap, V7X_LANES), lambda r: (0, r, 0)),
                   pl.BlockSpec((seq, V7X_LANES), lambda r: (r, 0))],
        out_shape=[jax.ShapeDtypeStruct((N_EXPERTS, n_req * cap, D_MODEL), BF16),
                   jax.ShapeDtypeStruct((N_EXPERTS, n_req * cap, V7X_LANES), F32),
                   jax.ShapeDtypeStruct((n_req * seq, V7X_LANES), F32)],
        compiler_params=_params(48, 1),
        name=f"dispatch_s{seq}",
    )(slot, aff, h2)


def _ffn_up_kernel(xc_ref, xl_ref, wg_ref, wu_ref, hc_ref, hl_ref):
    wg = wg_ref[...].astype(BF16)
    wu = wu_ref[...].astype(BF16)
    for x_ref, h_ref in ((xc_ref, hc_ref), (xl_ref, hl_ref)):
        x = x_ref[...]
        h_ref[...] = (jax.nn.silu(_dot(x, wg)) * _dot(x, wu)).astype(BF16)


def _ffn_down_kernel(hc_ref, hl_ref, wd_ref, gc_ref, gl_ref, yc_ref, yl_ref):
    wd = wd_ref[...].astype(BF16)
    for h_ref, g_ref, y_ref in ((hc_ref, gc_ref, yc_ref), (hl_ref, gl_ref, yl_ref)):
        gate = jnp.concatenate([g_ref[...]] * (wd.shape[1] // V7X_LANES), axis=1)
        y_ref[...] = (_dot(h_ref[...], wd) * gate).astype(BF16)


def _expert_ffn(xs_ctx, xs_lat, gate_ctx, gate_lat, w_gate, w_up, w_down):
    rows_c, rows_l = xs_ctx.shape[1], xs_lat.shape[1]
    tn = FFN_COL_TILE
    tok_c = pl.BlockSpec((None, rows_c, D_MODEL), lambda e, j: (e, 0, 0))
    tok_l = pl.BlockSpec((None, rows_l, D_MODEL), lambda e, j: (e, 0, 0))
    w_spec = pl.BlockSpec((None, D_MODEL, tn), lambda e, j: (e, 0, j))
    out_c = pl.BlockSpec((None, rows_c, tn), lambda e, j: (e, 0, j))
    out_l = pl.BlockSpec((None, rows_l, tn), lambda e, j: (e, 0, j))
    shapes = [jax.ShapeDtypeStruct((N_EXPERTS, rows_c, D_FF), BF16),
              jax.ShapeDtypeStruct((N_EXPERTS, rows_l, D_FF), BF16)]
    hid_c, hid_l = pl.pallas_call(
        _ffn_up_kernel,
        grid=(N_EXPERTS, D_FF // tn),
        in_specs=[tok_c, tok_l, w_spec, w_spec], out_specs=[out_c, out_l], out_shape=shapes,
        compiler_params=_params(56, 2),
        name="ffn_up",
    )(xs_ctx, xs_lat, w_gate, w_up)
    g_c = pl.BlockSpec((None, rows_c, V7X_LANES), lambda e, j: (e, 0, 0))
    g_l = pl.BlockSpec((None, rows_l, V7X_LANES), lambda e, j: (e, 0, 0))
    return pl.pallas_call(
        _ffn_down_kernel,
        grid=(N_EXPERTS, D_MODEL // tn),
        in_specs=[tok_c, tok_l, w_spec, g_c, g_l], out_specs=[out_c, out_l], out_shape=shapes,
        compiler_params=_params(56, 2),
        name="ffn_down",
    )(hid_c, hid_l, w_down, gate_ctx, gate_lat)


def _combine_kernel(slot_t_ref, y_ref, x1_ref, gate2_ref, g_ref, b_ref, o_ref, *, cap, experts_per_dot):
    rk = slot_t_ref[...].astype(BF16)
    cols = experts_per_dot * cap
    shift = cap.bit_length() - 1
    erow = lax.broadcasted_iota(jnp.int32, (V7X_LANES, cols), 0)
    ecol = lax.shift_right_logical(lax.broadcasted_iota(jnp.int32, (V7X_LANES, cols), 1), shift)
    slot = (lax.broadcasted_iota(jnp.int32, (rk.shape[0], cols), 1) & (cap - 1)).astype(F32)
    ffn = None
    for c in range(N_EXPERTS // experts_per_dot):
        expand = jnp.where(erow == ecol + c * experts_per_dot, 1.0, 0.0).astype(BF16)
        onehot_t = jnp.where(_dot(rk, expand) == slot, 1.0, 0.0).astype(BF16)
        y = y_ref[c * experts_per_dot:(c + 1) * experts_per_dot].reshape(cols, D_MODEL)
        part = _dot(onehot_t, y)
        ffn = part if ffn is None else ffn + part
    o_ref[...] = _layer_norm(DEEPNORM_ALPHA * x1_ref[...] + gate2_ref[...] * ffn, g_ref[...], b_ref[...])


def _combine(slot_t, y, x1, mod3, ln_g, ln_b, n_req, seq, cap, experts_per_dot, latent):
    ts = TOKEN_TILE
    steps = seq // ts
    row = lambda r, t: (r * steps + t, 0)
    const = lambda r, t: (0, 0)
    mod_row = (lambda r, t: ((1 + r) * N_MOD + 5, 0, 0)) if latent else (lambda r, t: (5, 0, 0))
    return pl.pallas_call(
        functools.partial(_combine_kernel, cap=cap, experts_per_dot=experts_per_dot),
        grid=(n_req, steps),
        in_specs=[pl.BlockSpec((ts, V7X_LANES), row),
                  pl.BlockSpec((N_EXPERTS, cap, D_MODEL), lambda r, t: (0, r, 0)),
                  pl.BlockSpec((ts, D_MODEL), row),
                  pl.BlockSpec((None, 1, D_MODEL), mod_row),
                  pl.BlockSpec((1, D_MODEL), const), pl.BlockSpec((1, D_MODEL), const)],
        out_specs=pl.BlockSpec((ts, D_MODEL), row),
        out_shape=jax.ShapeDtypeStruct((n_req * seq, D_MODEL), F32),
        compiler_params=_params(48, 2),
        name=f"combine_s{seq}",
    )(slot_t, y, x1, mod3, ln_g, ln_b)


def _rope_tables():
    t = jnp.arange(DEC_SEQ)
    row = (t // GRID_W).astype(F32)
    col = (t % GRID_W).astype(F32)
    n_freq = HEAD_DIM // 4
    inv_freq = ROPE_THETA ** (-jnp.arange(n_freq, dtype=F32) / n_freq)
    ang = jnp.concatenate([row[:, None] * inv_freq, col[:, None] * inv_freq], axis=-1)
    cos = jnp.repeat(jnp.cos(ang), 2, axis=-1)
    sin = jnp.stack([-jnp.sin(ang), jnp.sin(ang)], axis=-1).reshape(DEC_SEQ, HEAD_DIM)
    return cos, sin


def kernel(x_prompt, x_sample, cache_k_a, cache_v_a, cache_k_b, cache_v_b, c, c_ctx, w_mod, b_mod, w_in, w_o,
           sink_a, rel_bias_b, w_router, w_gate, w_up, w_down, ln1_g, ln1_b, ln2_g, ln2_b):
    assert DEPTH == 1 and w_in.shape == (DEPTH, D_MODEL, IN_COLS)
    x_ctx = x_prompt.reshape(N_CTX_TOK, D_MODEL)
    x_lat = x_sample.reshape(N_LAT_TOK, D_MODEL)

    cond = jnp.concatenate([c_ctx[None, :], c, jnp.zeros((V7X_SUBLANES - N_COND, D_MODEL), F32)], axis=0)
    mod = _modulation(cond, w_mod[0], b_mod)
    mod3 = mod.reshape(V7X_SUBLANES * N_MOD, 1, D_MODEL)

    w_in_bf16 = w_in[0].astype(BF16)
    wo_bf16 = w_o[0].astype(BF16)
    wr_pad = jnp.pad(w_router[0], ((0, 0), (0, V7X_LANES - N_EXPERTS))).astype(BF16)
    sink = sink_a[0]
    ln1 = (ln1_g, ln1_b)
    ln2 = (ln2_g, ln2_b)

    qa, ka, va, qb, kb, vb, ka_f, va_f, kb_f, vb_f = _input_projection(x_ctx, mod3, w_in_bf16, None, False)
    oa, ob = _context_attention(sink, qa, ka, va, qb, kb, vb)
    x1_ctx, h2_ctx, aff_ctx = _output_projection(oa, ob, wo_bf16, x_ctx, mod3, *ln1, wr_pad, SEQ, False)
    slot_ctx = _route(aff_ctx, CAP_CTX, ROUTE_REQUESTS_CTX)
    xs_ctx, gate_ctx, slot_t_ctx = _dispatch(slot_ctx, aff_ctx, h2_ctx, CAP_CTX, N_EXPERTS)

    qa, ka, va, qb, kb, vb = _input_projection(x_lat, mod3, w_in_bf16, _rope_tables(), True)
    oa = _window_attention(sink, qa, ka, va,
                           cache_k_a.reshape(DEC_BATCH * PAST_LEN, KV_A_COLS),
                           cache_v_a.reshape(DEC_BATCH * PAST_LEN, KV_A_COLS))
    ob = _neighborhood_attention(rel_bias_b.reshape(-1), qb, kb, vb,
                                 cache_k_b.reshape(DEC_BATCH * PAST_LEN, Q_B_COLS),
                                 cache_v_b.reshape(DEC_BATCH * PAST_LEN, Q_B_COLS))
    x1_lat, h2_lat, aff_lat = _output_projection(oa, ob, wo_bf16, x_lat, mod3, *ln1, wr_pad, DEC_SEQ, True)
    slot_lat = _route(aff_lat, CAP_LAT, DEC_BATCH)
    xs_lat, gate_lat, slot_t_lat = _dispatch(slot_lat, aff_lat, h2_lat, CAP_LAT, 4)

    y_ctx, y_lat = _expert_ffn(xs_ctx, xs_lat, gate_ctx, gate_lat, w_gate[0], w_up[0], w_down[0])

    y_p = _combine(slot_t_ctx, y_ctx, x1_ctx, mod3, *ln2, BATCH, SEQ, CAP_CTX, N_EXPERTS, False)
    y_s = _combine(slot_t_lat, y_lat, x1_lat, mod3, *ln2, DEC_BATCH, DEC_SEQ, CAP_LAT, 4, True)

    kv_a_shape = (BATCH, DEPTH, SEQ, N_KV_A, HEAD_DIM)
    kv_b_shape = (BATCH, DEPTH, SEQ, N_HEADS_B, HEAD_DIM)
    return (y_p.reshape(BATCH, SEQ, D_MODEL), y_s.reshape(DEC_BATCH, DEC_SEQ, D_MODEL),
            ka_f.reshape(kv_a_shape), va_f.reshape(kv_a_shape), kb_f.reshape(kv_b_shape), vb_f.reshape(kv_b_shape))
```

```python
import functools

import jax
import jax.numpy as jnp
from jax import lax
from jax.experimental import pallas as pl
from jax.experimental.pallas import tpu as pltpu

D_MODEL = 2048
BATCH = 32
SEQ = 256
DEC_BATCH = 4
DEC_SEQ = 1024
PAST_LEN = 256
GRID_W = 64
HEAD_DIM = 128
N_HEADS_A = 8
N_KV_A = 2
GROUP_A = N_HEADS_A // N_KV_A
N_HEADS_B = 8
WINDOW_A = 128
BLOCK_A = 128
NA_ROWS = 8
NA_COLS = 16
N_EXPERTS = 16
EC_CAPACITY = 2
D_FF = D_MODEL
ROPE_THETA = 10000.0
LN_EPS = 1e-5
NEG_INF = -1e30
DEPTH = 1
Q_A_COLS = N_HEADS_A * HEAD_DIM
KV_A_COLS = N_KV_A * HEAD_DIM
Q_B_COLS = N_HEADS_B * HEAD_DIM
IN_COLS = Q_A_COLS + 2 * KV_A_COLS + 3 * Q_B_COLS
COL_STARTS = (0, Q_A_COLS, Q_A_COLS + KV_A_COLS, Q_A_COLS + 2 * KV_A_COLS,
              Q_A_COLS + 2 * KV_A_COLS + Q_B_COLS, Q_A_COLS + 2 * KV_A_COLS + 2 * Q_B_COLS, IN_COLS)
DEEPNORM_ALPHA = (2.0 * DEPTH) ** 0.25
ATTN_SCALE = HEAD_DIM ** -0.5

N_CTX_TOK = BATCH * SEQ
N_LAT_TOK = DEC_BATCH * DEC_SEQ
GRID_ROWS = DEC_SEQ // GRID_W
CAP_CTX = EC_CAPACITY * SEQ // N_EXPERTS
CAP_LAT = EC_CAPACITY * DEC_SEQ // N_EXPERTS
N_COND = 1 + DEC_BATCH
N_MOD = 6

V7X_LANES = 128
V7X_SUBLANES = 8
V7X_VMEM_BYTES = 64 * 1024 * 1024
MIB = 1024 * 1024

F32 = jnp.float32
BF16 = jnp.bfloat16
NT_DIMS = (((1,), (1,)), ((), ()))

TOKEN_TILE = 256
MOD_COL_TILE = 1024
FFN_COL_TILE = 512
FFN_DOWN_COL_TILE = 1024
INPROJ_TILE = 512
OUTPROJ_TILE = 512
OUTPROJ_SUBTILE = 256
ROUTE_REQUESTS_CTX = 8
NA_GROUP_ROWS = 4
NA_WIN_ROWS = 12
NA_REL_ROWS = 2 * NA_ROWS - 1
NA_REL_COLS = 2 * NA_COLS - 1


def _params(vmem_mib, n_axes):
    return pltpu.CompilerParams(dimension_semantics=("arbitrary",) * n_axes,
                                vmem_limit_bytes=vmem_mib * MIB)


def _dot(a, b):
    return jnp.dot(a, b, preferred_element_type=F32)


def _dot_nt(a, b):
    return lax.dot_general(a, b, NT_DIMS, preferred_element_type=F32)


def _layer_norm(z, g, b):
    mu = jnp.mean(z, axis=-1, keepdims=True)
    zc = z - mu
    var = jnp.mean(zc * zc, axis=-1, keepdims=True)
    return zc * lax.rsqrt(var + LN_EPS) * g + b


def _mod_kernel(cond_ref, w_ref, b_ref, o_ref):
    s = jax.nn.silu(cond_ref[...]).astype(BF16)
    o_ref[...] = _dot(s, w_ref[...].astype(BF16)) + b_ref[...]


def _modulation(cond, w_mod, b_mod):
    n_out = N_MOD * D_MODEL
    return pl.pallas_call(
        _mod_kernel,
        grid=(n_out // MOD_COL_TILE,),
        in_specs=[pl.BlockSpec((V7X_SUBLANES, D_MODEL), lambda j: (0, 0)),
                  pl.BlockSpec((D_MODEL, MOD_COL_TILE), lambda j: (0, j)),
                  pl.BlockSpec((1, MOD_COL_TILE), lambda j: (0, j))],
        out_specs=pl.BlockSpec((V7X_SUBLANES, MOD_COL_TILE), lambda j: (0, j)),
        out_shape=jax.ShapeDtypeStruct((V7X_SUBLANES, n_out), F32),
        compiler_params=_params(40, 1),
        name="modulation",
    )(cond, w_mod, b_mod)


def _mod_spec(which, latent, tile=TOKEN_TILE):
    steps_per_request = DEC_SEQ // tile
    if latent:
        return pl.BlockSpec((None, 1, D_MODEL), lambda i: ((1 + i // steps_per_request) * N_MOD + which, 0, 0))
    return pl.BlockSpec((None, 1, D_MODEL), lambda i: (which, 0, 0))


def _rope(x, cos, sin, even):
    swapped = jnp.where(even, pltpu.roll(x, HEAD_DIM - 1, 1), pltpu.roll(x, 1, 1))
    return x * cos + swapped * sin


def _inproj_kernel(*refs, latent):
    if latent:
        (x_ref, shift_ref, scale_ref, w_ref, cos_ref, sin_ref,
         qa_o, ka_o, va_o, qb_o, kb_o, vb_o) = refs
    else:
        (x_ref, shift_ref, scale_ref, w_ref,
         qa_o, ka_o, va_o, qb_o, kb_o, vb_o, kaf_o, vaf_o, kbf_o, vbf_o) = refs
    h = (x_ref[...] * (1.0 + scale_ref[...]) + shift_ref[...]).astype(BF16)

    def proj(piece):
        return _dot(h, w_ref[:, COL_STARTS[piece]:COL_STARTS[piece + 1]])

    if latent:
        cos = cos_ref[...]
        sin = sin_ref[...]
        even = (lax.broadcasted_iota(jnp.int32, cos.shape, 1) & 1) == 0

    def heads(y, o_ref, n_heads, scale, rope):
        for hd in range(n_heads):
            cols = slice(hd * HEAD_DIM, (hd + 1) * HEAD_DIM)
            yh = y[:, cols]
            if rope:
                yh = _rope(yh, cos, sin, even)
            if scale != 1.0:
                yh = yh * scale
            o_ref[:, cols] = yh.astype(BF16)

    qa = proj(0)
    heads(qa, qa_o, N_HEADS_A, ATTN_SCALE, latent)
    ka = proj(1)
    heads(ka, ka_o, N_KV_A, 1.0, latent)
    va = proj(2)
    va_o[...] = va.astype(BF16)
    qb = proj(3)
    qb_o[...] = (qb * ATTN_SCALE).astype(BF16)
    kb = proj(4)
    kb_o[...] = kb.astype(BF16)
    vb = proj(5)
    vb_o[...] = vb.astype(BF16)
    if not latent:
        kaf_o[...] = ka
        vaf_o[...] = va
        kbf_o[...] = kb
        vbf_o[...] = vb


def _input_projection(x, mod3, w_in_bf16, rope_tables, latent):
    n_tok = x.shape[0]
    tm = INPROJ_TILE
    row = lambda i: (i, 0)
    in_specs = [pl.BlockSpec((tm, D_MODEL), row), _mod_spec(0, latent, tm), _mod_spec(1, latent, tm),
                pl.BlockSpec((D_MODEL, IN_COLS), lambda i: (0, 0), pipeline_mode=pl.Buffered(1))]
    args = [x, mod3, mod3, w_in_bf16]
    widths = [Q_A_COLS, KV_A_COLS, KV_A_COLS, Q_B_COLS, Q_B_COLS, Q_B_COLS]
    out_specs = [pl.BlockSpec((tm, w), row) for w in widths]
    out_shape = [jax.ShapeDtypeStruct((n_tok, w), BF16) for w in widths]
    if latent:
        steps_per_request = DEC_SEQ // tm
        in_specs += [pl.BlockSpec((tm, HEAD_DIM), lambda i: (i % steps_per_request, 0))] * 2
        args += list(rope_tables)
    else:
        kv_widths = [KV_A_COLS, KV_A_COLS, Q_B_COLS, Q_B_COLS]
        out_specs += [pl.BlockSpec((tm, w), row) for w in kv_widths]
        out_shape += [jax.ShapeDtypeStruct((n_tok, w), F32) for w in kv_widths]
    return pl.pallas_call(
        functools.partial(_inproj_kernel, latent=latent),
        grid=(n_tok // tm,),
        in_specs=in_specs, out_specs=out_specs, out_shape=out_shape,
        compiler_params=_params(58, 1),
        name="inproj_latent" if latent else "inproj_context",
    )(*args)


def _with_ones(v):
    return jnp.concatenate([v, jnp.ones((v.shape[0], V7X_LANES), v.dtype)], axis=1)


def _softmax_av(scores, values, sink=None):
    m = functools.reduce(jnp.maximum, [jnp.max(s, axis=-1, keepdims=True) for s in scores])
    if sink is not None:
        m = jnp.maximum(m, sink)
    acc = functools.reduce(jnp.add, [_dot(jnp.exp(s - m).astype(BF16), _with_ones(v))
                                     for s, v in zip(scores, values)])
    denom = acc[:, HEAD_DIM:]
    if sink is not None:
        denom = denom + jnp.exp(sink - m)
    return acc[:, :HEAD_DIM] / denom


def _stack_heads(q_ref, first_head, n_heads):
    return jnp.concatenate([q_ref[:, (first_head + g) * HEAD_DIM:(first_head + g + 1) * HEAD_DIM]
                            for g in range(n_heads)], axis=0)


def _sink_column(sink_ref, first_head, n_heads, rows):
    return jnp.concatenate([jnp.full((rows, 1), sink_ref[first_head + g], F32) for g in range(n_heads)], axis=0)


def _ctx_attn_kernel(sink_ref, qa_ref, ka_ref, va_ref, qb_ref, kb_ref, vb_ref, oa_ref, ob_ref):
    for kv in range(N_KV_A):
        cols = slice(kv * HEAD_DIM, (kv + 1) * HEAD_DIM)
        q = _stack_heads(qa_ref, kv * GROUP_A, GROUP_A)
        sink = _sink_column(sink_ref, kv * GROUP_A, GROUP_A, SEQ)
        o = _softmax_av([_dot_nt(q, ka_ref[:, cols])], [va_ref[:, cols]], sink)
        for g in range(GROUP_A):
            hd = kv * GROUP_A + g
            oa_ref[:, hd * HEAD_DIM:(hd + 1) * HEAD_DIM] = o[g * SEQ:(g + 1) * SEQ].astype(BF16)
    for hd in range(N_HEADS_B):
        cols = slice(hd * HEAD_DIM, (hd + 1) * HEAD_DIM)
        o = _softmax_av([_dot_nt(qb_ref[:, cols], kb_ref[:, cols])], [vb_ref[:, cols]])
        ob_ref[:, cols] = o.astype(BF16)


def _context_attention(sink, qa, ka, va, qb, kb, vb):
    row = lambda b: (b, 0)
    widths = [Q_A_COLS, KV_A_COLS, KV_A_COLS, Q_B_COLS, Q_B_COLS, Q_B_COLS]
    return pl.pallas_call(
        _ctx_attn_kernel,
        grid=(BATCH,),
        in_specs=[pl.BlockSpec(memory_space=pltpu.SMEM)] + [pl.BlockSpec((SEQ, w), row) for w in widths],
        out_specs=[pl.BlockSpec((SEQ, Q_A_COLS), row), pl.BlockSpec((SEQ, Q_B_COLS), row)],
        out_shape=[jax.ShapeDtypeStruct((N_CTX_TOK, Q_A_COLS), BF16),
                   jax.ShapeDtypeStruct((N_CTX_TOK, Q_B_COLS), BF16)],
        compiler_params=_params(32, 1),
        name="context_attention",
    )(sink, qa, ka, va, qb, kb, vb)


def _window_attn_kernel(sink_ref, q_ref, k_ref, v_ref, ck_ref, cv_ref, o_ref):
    kv = pl.program_id(1)
    span = 3 * BLOCK_A
    rows = GROUP_A * BLOCK_A
    sink = jnp.concatenate([jnp.full((BLOCK_A, 1), sink_ref[kv * GROUP_A + g], F32) for g in range(GROUP_A)],
                           axis=0)
    ck = ck_ref[...].astype(BF16)
    cv = cv_ref[...].astype(BF16)
    band = (lax.broadcasted_iota(jnp.int32, (rows, span), 1)
            - (lax.broadcasted_iota(jnp.int32, (rows, span), 0) & (BLOCK_A - 1)))
    masks = {}
    for n in range(DEC_SEQ // BLOCK_A):
        start = min(max((n - 1) * BLOCK_A, 0), DEC_SEQ - span)
        offset = start - n * BLOCK_A
        if offset not in masks:
            masks[offset] = jnp.where(jnp.abs(band + offset) <= WINDOW_A, 0.0, NEG_INF)
        q = jnp.concatenate([q_ref[n * BLOCK_A:(n + 1) * BLOCK_A, g * HEAD_DIM:(g + 1) * HEAD_DIM]
                             for g in range(GROUP_A)], axis=0)
        s_loc = _dot_nt(q, k_ref[start:start + span, :]) + masks[offset]
        o = _softmax_av([s_loc, _dot_nt(q, ck)], [v_ref[start:start + span, :], cv], sink)
        for g in range(GROUP_A):
            o_ref[n * BLOCK_A:(n + 1) * BLOCK_A, g * HEAD_DIM:(g + 1) * HEAD_DIM] = (
                o[g * BLOCK_A:(g + 1) * BLOCK_A].astype(BF16))


def _window_attention(sink, qa, ka, va, ck, cv):
    q_spec = pl.BlockSpec((DEC_SEQ, GROUP_A * HEAD_DIM), lambda b, kv: (b, kv))
    kv_spec = pl.BlockSpec((DEC_SEQ, HEAD_DIM), lambda b, kv: (b, kv))
    c_spec = pl.BlockSpec((PAST_LEN, HEAD_DIM), lambda b, kv: (b, kv))
    return pl.pallas_call(
        _window_attn_kernel,
        grid=(DEC_BATCH, N_KV_A),
        in_specs=[pl.BlockSpec(memory_space=pltpu.SMEM), q_spec, kv_spec, kv_spec, c_spec, c_spec],
        out_specs=q_spec,
        out_shape=jax.ShapeDtypeStruct((N_LAT_TOK, Q_A_COLS), BF16),
        compiler_params=_params(32, 2),
        name="window_attention",
    )(sink, qa, ka, va, ck, cv)


def _na_row_start(rq):
    return min(max(rq - NA_ROWS // 2, 0), GRID_ROWS - NA_ROWS)


def _na_window_start(group):
    return min(max(group * NA_GROUP_ROWS - NA_ROWS // 2, 0), GRID_ROWS - NA_WIN_ROWS)


def _na_build_bias(rb_ref, bias_ref, head):
    shape = (GRID_W, 2 * GRID_W)
    lane = lax.broadcasted_iota(jnp.int32, shape, 1)
    qcol = lax.broadcasted_iota(jnp.int32, shape, 0)
    kcol = lane & (GRID_W - 1)
    left = lane < GRID_W
    dcol = kcol - qcol + (NA_COLS - 1)
    wstart = jnp.clip(qcol - NA_COLS // 2, 0, GRID_W - NA_COLS)
    col_ok = (kcol >= wstart) & (kcol < wstart + NA_COLS)
    base = head * (NA_REL_ROWS * NA_REL_COLS)
    toeplitz = []
    for a in range(NA_REL_ROWS):
        acc = jnp.zeros(shape, F32)
        for bb in range(NA_REL_COLS):
            acc = jnp.where(dcol == bb, rb_ref[base + a * NA_REL_COLS + bb], acc)
        toeplitz.append(acc)
    neg = jnp.full(shape, NEG_INF, F32)
    for g in range(GRID_ROWS // NA_GROUP_ROWS):
        wrow = _na_window_start(g)
        for qi in range(NA_GROUP_ROWS):
            rq = g * NA_GROUP_ROWS + qi
            row0 = _na_row_start(rq)
            for p in range(NA_WIN_ROWS // 2):
                rk0 = wrow + 2 * p
                ok0 = row0 <= rk0 < row0 + NA_ROWS
                ok1 = row0 <= rk0 + 1 < row0 + NA_ROWS
                a0 = rk0 - rq + NA_ROWS - 1
                if ok0 and ok1:
                    blk = jnp.where(col_ok, jnp.where(left, toeplitz[a0], toeplitz[a0 + 1]), neg)
                elif ok0:
                    blk = jnp.where(col_ok & left, toeplitz[a0], neg)
                elif ok1:
                    blk = jnp.where(col_ok & jnp.logical_not(left), toeplitz[a0 + 1], neg)
                else:
                    blk = neg
                bias_ref[g, qi * GRID_W:(qi + 1) * GRID_W, p * 2 * GRID_W:(p + 1) * 2 * GRID_W] = blk


def _na_attn_kernel(rb_ref, q_ref, k_ref, v_ref, ck_ref, cv_ref, o_ref, bias_ref):
    head = pl.program_id(0)

    @pl.when(pl.program_id(1) == 0)
    def _():
        _na_build_bias(rb_ref, bias_ref, head)

    ck = ck_ref[...].astype(BF16)
    cv = cv_ref[...].astype(BF16)
    q_rows = NA_GROUP_ROWS * GRID_W
    k_rows = NA_WIN_ROWS * GRID_W
    for g in range(GRID_ROWS // NA_GROUP_ROWS):
        k0 = _na_window_start(g) * GRID_W
        q = q_ref[g * q_rows:(g + 1) * q_rows, :]
        s_loc = _dot_nt(q, k_ref[k0:k0 + k_rows, :]) + bias_ref[g]
        s_ctx = _dot_nt(q, ck)
        o = _softmax_av([s_loc, s_ctx], [v_ref[k0:k0 + k_rows, :], cv])
        o_ref[g * q_rows:(g + 1) * q_rows, :] = o.astype(BF16)


def _neighborhood_attention(rel_bias_flat, qb, kb, vb, ck, cv):
    tok_spec = pl.BlockSpec((DEC_SEQ, HEAD_DIM), lambda h, b: (b, h))
    c_spec = pl.BlockSpec((PAST_LEN, HEAD_DIM), lambda h, b: (b, h))
    n_groups = GRID_ROWS // NA_GROUP_ROWS
    return pl.pallas_call(
        _na_attn_kernel,
        grid=(N_HEADS_B, DEC_BATCH),
        in_specs=[pl.BlockSpec(memory_space=pltpu.SMEM), tok_spec, tok_spec, tok_spec, c_spec, c_spec],
        out_specs=tok_spec,
        out_shape=jax.ShapeDtypeStruct((N_LAT_TOK, Q_B_COLS), BF16),
        scratch_shapes=[pltpu.VMEM((n_groups, NA_GROUP_ROWS * GRID_W, NA_WIN_ROWS * GRID_W), F32)],
        compiler_params=_params(32, 2),
        name="neighborhood_attention",
    )(rel_bias_flat, qb, kb, vb, ck, cv)


def _outproj_kernel(oa_ref, ob_ref, wo_ref, x_ref, gate1_ref, shift2_ref, scale2_ref, g_ref, b_ref, wr_ref,
                    x1_ref, h2_ref, aff_ref):
    lanes = aff_ref.shape[2]
    for r0 in range(0, OUTPROJ_TILE, OUTPROJ_SUBTILE):
        rows = slice(r0, r0 + OUTPROJ_SUBTILE)
        attn = _dot(oa_ref[rows, :], wo_ref[:Q_A_COLS, :]) + _dot(ob_ref[rows, :], wo_ref[Q_A_COLS:, :])
        x1 = _layer_norm(DEEPNORM_ALPHA * x_ref[rows, :] + gate1_ref[...] * attn, g_ref[...], b_ref[...])
        x1_ref[rows, :] = x1
        h2 = (x1 * (1.0 + scale2_ref[...]) + shift2_ref[...]).astype(BF16)
        h2_ref[rows, :] = h2
        lt = _dot(h2, wr_ref[...]).T[:N_EXPERTS, :]
        e = jnp.exp(lt - jnp.max(lt, axis=0, keepdims=True))
        req, l0 = divmod(r0, lanes)
        aff_ref[req, :, l0:l0 + OUTPROJ_SUBTILE] = e / jnp.sum(e, axis=0, keepdims=True)


def _output_projection(oa, ob, wo_bf16, x, mod3, ln_g, ln_b, wr_pad, seq, latent):
    n_tok = x.shape[0]
    tm = OUTPROJ_TILE
    row = lambda i: (i, 0)
    const = lambda i: (0, 0)
    if seq >= tm:
        tiles_per_request = seq // tm
        aff_spec = pl.BlockSpec((1, N_EXPERTS, tm), lambda i: (i // tiles_per_request, 0, i % tiles_per_request))
    else:
        aff_spec = pl.BlockSpec((tm // seq, N_EXPERTS, seq), lambda i: (i, 0, 0))
    return pl.pallas_call(
        _outproj_kernel,
        grid=(n_tok // tm,),
        in_specs=[pl.BlockSpec((tm, Q_A_COLS), row), pl.BlockSpec((tm, Q_B_COLS), row),
                  pl.BlockSpec((Q_A_COLS + Q_B_COLS, D_MODEL), const, pipeline_mode=pl.Buffered(1)),
                  pl.BlockSpec((tm, D_MODEL), row),
                  _mod_spec(2, latent, tm), _mod_spec(3, latent, tm), _mod_spec(4, latent, tm),
                  pl.BlockSpec((1, D_MODEL), const), pl.BlockSpec((1, D_MODEL), const),
                  pl.BlockSpec((D_MODEL, V7X_LANES), const)],
        out_specs=[pl.BlockSpec((tm, D_MODEL), row), pl.BlockSpec((tm, D_MODEL), row), aff_spec],
        out_shape=[jax.ShapeDtypeStruct((n_tok, D_MODEL), F32), jax.ShapeDtypeStruct((n_tok, D_MODEL), BF16),
                   jax.ShapeDtypeStruct((n_tok // seq, N_EXPERTS, seq), F32)],
        compiler_params=_params(56, 1),
        name="outproj_latent" if latent else "outproj_context",
    )(oa, ob, wo_bf16, x, mod3, mod3, mod3, ln_g, ln_b, wr_pad)


def _pad_rows_to_lanes(a):
    return jnp.concatenate([a, jnp.zeros((V7X_LANES - a.shape[0], a.shape[1]), a.dtype)], axis=0)


def _kth_largest(a, k):
    n = a.shape[1]
    lane = lax.broadcasted_iota(jnp.int32, (1, n), 1)
    x = a
    size = 2
    while size <= n:
        descending = (lane & size) == 0
        j = size // 2
        while j >= 1:
            lower = (lane & j) == 0
            partner = jnp.where(lower, pltpu.roll(x, n - j, 1), pltpu.roll(x, j, 1))
            x = jnp.where(lower == descending, jnp.maximum(x, partner), jnp.minimum(x, partner))
            j //= 2
        size *= 2
    return x[:, k - 1:k]


def _route_kernel(aff_ref, slot_ref, *, cap):
    n_req, n_exp, seq = aff_ref.shape
    key = aff_ref[...].reshape(n_req * n_exp, seq)
    tau = _kth_largest(key, cap)
    above = key > tau
    tied = key == tau
    need = cap - jnp.sum(jnp.where(above, 1.0, 0.0), axis=-1, keepdims=True)
    before = jnp.where(lax.broadcasted_iota(jnp.int32, (seq, seq), 0) < lax.broadcasted_iota(jnp.int32, (seq, seq), 1),
                       1.0, 0.0).astype(BF16)
    tied_before = _dot(jnp.where(tied, 1.0, 0.0).astype(BF16), before)
    chosen = above | (tied & (tied_before < need))
    slot = _dot(jnp.where(chosen, 1.0, 0.0).astype(BF16), before)
    slot_ref[...] = jnp.where(chosen, slot, float(cap)).reshape(n_req, n_exp, seq)


def _route(aff, cap, requests_per_step):
    n_req, _, seq = aff.shape
    spec = pl.BlockSpec((requests_per_step, N_EXPERTS, seq), lambda r: (r, 0, 0))
    return pl.pallas_call(
        functools.partial(_route_kernel, cap=cap),
        grid=(n_req // requests_per_step,),
        in_specs=[spec], out_specs=spec,
        out_shape=jax.ShapeDtypeStruct(aff.shape, F32),
        compiler_params=_params(32, 1),
        name=f"route_s{seq}",
    )(aff)


def _dispatch_kernel(slot_ref, aff_ref, h2_ref, xs_ref, gate_ref, slot_t_ref, *, cap, experts_per_dot):
    slot_of = slot_ref[...]
    aff = aff_ref[...]
    seq = slot_of.shape[1]
    slot_t_ref[...] = _pad_rows_to_lanes(slot_of).T
    h2 = h2_ref[...]
    slot = lax.broadcasted_iota(jnp.int32, (cap, seq), 0).astype(F32)
    for c in range(N_EXPERTS // experts_per_dot):
        chunk = range(c * experts_per_dot, (c + 1) * experts_per_dot)
        onehot = jnp.concatenate([jnp.where(slot_of[e:e + 1, :] == slot, 1.0, 0.0) for e in chunk], axis=0)
        xs = _dot(onehot.astype(BF16), h2)
        for i, e in enumerate(chunk):
            xs_ref[e] = xs[i * cap:(i + 1) * cap].astype(BF16)
            gate = jnp.sum(onehot[i * cap:(i + 1) * cap] * aff[e:e + 1, :], axis=1, keepdims=True)
            gate_ref[e] = jnp.broadcast_to(gate, (cap, V7X_LANES))


def _dispatch(slot, aff, h2, cap, experts_per_dot):
    n_req, _, seq = aff.shape
    req_spec = pl.BlockSpec((None, N_EXPERTS, seq), lambda r: (r, 0, 0))
    return pl.pallas_call(
        functools.partial(_dispatch_kernel, cap=cap, experts_per_dot=experts_per_dot),
        grid=(n_req,),
        in_specs=[req_spec, req_spec, pl.BlockSpec((seq, D_MODEL), lambda r: (r, 0))],
        out_specs=[pl.BlockSpec((None, N_EXPERTS, cap, D_MODEL), lambda r: (r, 0, 0, 0)),
                   pl.BlockSpec((None, N_EXPERTS, cap, V7X_LANES), lambda r: (r, 0, 0, 0)),
                   pl.BlockSpec((seq, V7X_LANES), lambda r: (r, 0))],
        out_shape=[jax.ShapeDtypeStruct((n_req, N_EXPERTS, cap, D_MODEL), BF16),
                   jax.ShapeDtypeStruct((n_req, N_EXPERTS, cap, V7X_LANES), F32),
                   jax.ShapeDtypeStruct((n_req * seq, V7X_LANES), F32)],
        compiler_params=_params(48, 1),
        name=f"dispatch_s{seq}",
    )(slot, aff, h2)


def _expert_rows(ref):
    n_req, cap, width = ref.shape
    return ref[...].reshape(n_req * cap, width)


def _ffn_up_kernel(xc_ref, xl_ref, wg_ref, wu_ref, hc_ref, hl_ref):
    wg = wg_ref[...].astype(BF16)
    wu = wu_ref[...].astype(BF16)
    for x_ref, h_ref in ((xc_ref, hc_ref), (xl_ref, hl_ref)):
        x = _expert_rows(x_ref)
        h_ref[...] = (jax.nn.silu(_dot(x, wg)) * _dot(x, wu)).astype(BF16)


def _ffn_down_kernel(hc_ref, hl_ref, wd_ref, gc_ref, gl_ref, yc_ref, yl_ref):
    wd = wd_ref[...].astype(BF16)
    for h_ref, g_ref, y_ref in ((hc_ref, gc_ref, yc_ref), (hl_ref, gl_ref, yl_ref)):
        gate = jnp.concatenate([_expert_rows(g_ref)] * (wd.shape[1] // V7X_LANES), axis=1)
        y_ref[...] = (_dot(h_ref[...], wd) * gate).astype(BF16).reshape(y_ref.shape)


def _expert_ffn(xs_ctx, xs_lat, gate_ctx, gate_lat, w_gate, w_up, w_down):
    def per_expert(a, width):
        n_req, _, cap, _ = a.shape
        return pl.BlockSpec((n_req, None, cap, width), lambda e, j: (0, e, 0, 0 if width == a.shape[3] else j))

    rows_c = xs_ctx.shape[0] * xs_ctx.shape[2]
    rows_l = xs_lat.shape[0] * xs_lat.shape[2]
    tn = FFN_COL_TILE
    hid_c = pl.BlockSpec((None, rows_c, tn), lambda e, j: (e, 0, j))
    hid_l = pl.BlockSpec((None, rows_l, tn), lambda e, j: (e, 0, j))
    hidden_c, hidden_l = pl.pallas_call(
        _ffn_up_kernel,
        grid=(N_EXPERTS, D_FF // tn),
        in_specs=[per_expert(xs_ctx, D_MODEL), per_expert(xs_lat, D_MODEL),
                  pl.BlockSpec((None, D_MODEL, tn), lambda e, j: (e, 0, j)),
                  pl.BlockSpec((None, D_MODEL, tn), lambda e, j: (e, 0, j))],
        out_specs=[hid_c, hid_l],
        out_shape=[jax.ShapeDtypeStruct((N_EXPERTS, rows_c, D_FF), BF16),
                   jax.ShapeDtypeStruct((N_EXPERTS, rows_l, D_FF), BF16)],
        compiler_params=_params(56, 2),
        name="ffn_up",
    )(xs_ctx, xs_lat, w_gate, w_up)
    tn = FFN_DOWN_COL_TILE
    return pl.pallas_call(
        _ffn_down_kernel,
        grid=(N_EXPERTS, D_MODEL // tn),
        in_specs=[pl.BlockSpec((None, rows_c, D_FF), lambda e, j: (e, 0, 0)),
                  pl.BlockSpec((None, rows_l, D_FF), lambda e, j: (e, 0, 0)),
                  pl.BlockSpec((None, D_FF, tn), lambda e, j: (e, 0, j)),
                  per_expert(gate_ctx, V7X_LANES), per_expert(gate_lat, V7X_LANES)],
        out_specs=[per_expert(xs_ctx, tn), per_expert(xs_lat, tn)],
        out_shape=[jax.ShapeDtypeStruct(xs_ctx.shape, BF16), jax.ShapeDtypeStruct(xs_lat.shape, BF16)],
        compiler_params=_params(56, 2),
        name="ffn_down",
    )(hidden_c, hidden_l, w_down, gate_ctx, gate_lat)


def _combine_kernel(slot_t_ref, y_ref, x1_ref, gate2_ref, g_ref, b_ref, o_ref, *, cap, experts_per_dot):
    rk = slot_t_ref[...].astype(BF16)
    cols = experts_per_dot * cap
    shift = cap.bit_length() - 1
    erow = lax.broadcasted_iota(jnp.int32, (V7X_LANES, cols), 0)
    ecol = lax.shift_right_logical(lax.broadcasted_iota(jnp.int32, (V7X_LANES, cols), 1), shift)
    slot = (lax.broadcasted_iota(jnp.int32, (rk.shape[0], cols), 1) & (cap - 1)).astype(F32)
    ffn = None
    for c in range(N_EXPERTS // experts_per_dot):
        expand = jnp.where(erow == ecol + c * experts_per_dot, 1.0, 0.0).astype(BF16)
        onehot_t = jnp.where(_dot(rk, expand) == slot, 1.0, 0.0).astype(BF16)
        y = y_ref[c * experts_per_dot:(c + 1) * experts_per_dot].reshape(cols, D_MODEL)
        part = _dot(onehot_t, y)
        ffn = part if ffn is None else ffn + part
    o_ref[...] = _layer_norm(DEEPNORM_ALPHA * x1_ref[...] + gate2_ref[...] * ffn, g_ref[...], b_ref[...])


def _combine(slot_t, y, x1, mod3, ln_g, ln_b, n_req, seq, cap, experts_per_dot, latent):
    ts = TOKEN_TILE
    steps = seq // ts
    row = lambda r, t: (r * steps + t, 0)
    const = lambda r, t: (0, 0)
    mod_row = (lambda r, t: ((1 + r) * N_MOD + 5, 0, 0)) if latent else (lambda r, t: (5, 0, 0))
    return pl.pallas_call(
        functools.partial(_combine_kernel, cap=cap, experts_per_dot=experts_per_dot),
        grid=(n_req, steps),
        in_specs=[pl.BlockSpec((ts, V7X_LANES), row),
                  pl.BlockSpec((None, N_EXPERTS, cap, D_MODEL), lambda r, t: (r, 0, 0, 0)),
                  pl.BlockSpec((ts, D_MODEL), row),
                  pl.BlockSpec((None, 1, D_MODEL), mod_row),
                  pl.BlockSpec((1, D_MODEL), const), pl.BlockSpec((1, D_MODEL), const)],
        out_specs=pl.BlockSpec((ts, D_MODEL), row),
        out_shape=jax.ShapeDtypeStruct((n_req * seq, D_MODEL), F32),
        compiler_params=_params(48, 2),
        name=f"combine_s{seq}",
    )(slot_t, y, x1, mod3, ln_g, ln_b)


def _rope_tables():
    t = jnp.arange(DEC_SEQ)
    row = (t // GRID_W).astype(F32)
    col = (t % GRID_W).astype(F32)
    n_freq = HEAD_DIM // 4
    inv_freq = ROPE_THETA ** (-jnp.arange(n_freq, dtype=F32) / n_freq)
    ang = jnp.concatenate([row[:, None] * inv_freq, col[:, None] * inv_freq], axis=-1)
    cos = jnp.repeat(jnp.cos(ang), 2, axis=-1)
    sin = jnp.stack([-jnp.sin(ang), jnp.sin(ang)], axis=-1).reshape(DEC_SEQ, HEAD_DIM)
    return cos, sin


def kernel(x_prompt, x_sample, cache_k_a, cache_v_a, cache_k_b, cache_v_b, c, c_ctx, w_mod, b_mod, w_in, w_o,
           sink_a, rel_bias_b, w_router, w_gate, w_up, w_down, ln1_g, ln1_b, ln2_g, ln2_b):
    assert DEPTH == 1 and w_in.shape == (DEPTH, D_MODEL, IN_COLS)
    x_ctx = x_prompt.reshape(N_CTX_TOK, D_MODEL)
    x_lat = x_sample.reshape(N_LAT_TOK, D_MODEL)

    cond = jnp.concatenate([c_ctx[None, :], c, jnp.zeros((V7X_SUBLANES - N_COND, D_MODEL), F32)], axis=0)
    mod = _modulation(cond, w_mod[0], b_mod)
    mod3 = mod.reshape(V7X_SUBLANES * N_MOD, 1, D_MODEL)

    w_in_bf16 = w_in[0].astype(BF16)
    wo_bf16 = w_o[0].astype(BF16)
    wr_pad = jnp.pad(w_router[0], ((0, 0), (0, V7X_LANES - N_EXPERTS))).astype(BF16)
    sink = sink_a[0]
    ln1 = (ln1_g, ln1_b)
    ln2 = (ln2_g, ln2_b)

    qa, ka, va, qb, kb, vb, ka_f, va_f, kb_f, vb_f = _input_projection(x_ctx, mod3, w_in_bf16, None, False)
    oa, ob = _context_attention(sink, qa, ka, va, qb, kb, vb)
    x1_ctx, h2_ctx, aff_ctx = _output_projection(oa, ob, wo_bf16, x_ctx, mod3, *ln1, wr_pad, SEQ, False)
    slot_ctx = _route(aff_ctx, CAP_CTX, ROUTE_REQUESTS_CTX)
    xs_ctx, gate_ctx, slot_t_ctx = _dispatch(slot_ctx, aff_ctx, h2_ctx, CAP_CTX, N_EXPERTS)

    qa, ka, va, qb, kb, vb = _input_projection(x_lat, mod3, w_in_bf16, _rope_tables(), True)
    oa = _window_attention(sink, qa, ka, va,
                           cache_k_a.reshape(DEC_BATCH * PAST_LEN, KV_A_COLS),
                           cache_v_a.reshape(DEC_BATCH * PAST_LEN, KV_A_COLS))
    ob = _neighborhood_attention(rel_bias_b.reshape(-1), qb, kb, vb,
                                 cache_k_b.reshape(DEC_BATCH * PAST_LEN, Q_B_COLS),
                                 cache_v_b.reshape(DEC_BATCH * PAST_LEN, Q_B_COLS))
    x1_lat, h2_lat, aff_lat = _output_projection(oa, ob, wo_bf16, x_lat, mod3, *ln1, wr_pad, DEC_SEQ, True)
    slot_lat = _route(aff_lat, CAP_LAT, DEC_BATCH)
    xs_lat, gate_lat, slot_t_lat = _dispatch(slot_lat, aff_lat, h2_lat, CAP_LAT, 4)

    y_ctx, y_lat = _expert_ffn(xs_ctx, xs_lat, gate_ctx, gate_lat, w_gate[0], w_up[0], w_down[0])

    y_p = _combine(slot_t_ctx, y_ctx, x1_ctx, mod3, *ln2, BATCH, SEQ, CAP_CTX, N_EXPERTS, False)
    y_s = _combine(slot_t_lat, y_lat, x1_lat, mod3, *ln2, DEC_BATCH, DEC_SEQ, CAP_LAT, 4, True)

    kv_a_shape = (BATCH, DEPTH, SEQ, N_KV_A, HEAD_DIM)
    kv_b_shape = (BATCH, DEPTH, SEQ, N_HEADS_B, HEAD_DIM)
    return (y_p.reshape(BATCH, SEQ, D_MODEL), y_s.reshape(DEC_BATCH, DEC_SEQ, D_MODEL),
            ka_f.reshape(kv_a_shape), va_f.reshape(kv_a_shape), kb_f.reshape(kv_b_shape), vb_f.reshape(kv_b_shape))
```

```python
import functools

import jax
import jax.numpy as jnp
from jax import lax
from jax.experimental import pallas as pl
from jax.experimental.pallas import tpu as pltpu

D_MODEL = 2048
BATCH = 32
SEQ = 256
DEC_BATCH = 4
DEC_SEQ = 1024
PAST_LEN = 256
GRID_W = 64
HEAD_DIM = 128
N_HEADS_A = 8
N_KV_A = 2
GROUP_A = N_HEADS_A // N_KV_A
N_HEADS_B = 8
WINDOW_A = 128
BLOCK_A = 128
NA_ROWS = 8
NA_COLS = 16
N_EXPERTS = 16
EC_CAPACITY = 2
D_FF = D_MODEL
ROPE_THETA = 10000.0
LN_EPS = 1e-5
NEG_INF = -1e30
DEPTH = 1
Q_A_COLS = N_HEADS_A * HEAD_DIM
KV_A_COLS = N_KV_A * HEAD_DIM
Q_B_COLS = N_HEADS_B * HEAD_DIM
IN_COLS = Q_A_COLS + 2 * KV_A_COLS + 3 * Q_B_COLS
COL_STARTS = (0, Q_A_COLS, Q_A_COLS + KV_A_COLS, Q_A_COLS + 2 * KV_A_COLS,
              Q_A_COLS + 2 * KV_A_COLS + Q_B_COLS, Q_A_COLS + 2 * KV_A_COLS + 2 * Q_B_COLS, IN_COLS)
DEEPNORM_ALPHA = (2.0 * DEPTH) ** 0.25
ATTN_SCALE = HEAD_DIM ** -0.5

N_CTX_TOK = BATCH * SEQ
N_LAT_TOK = DEC_BATCH * DEC_SEQ
GRID_ROWS = DEC_SEQ // GRID_W
CAP_CTX = EC_CAPACITY * SEQ // N_EXPERTS
CAP_LAT = EC_CAPACITY * DEC_SEQ // N_EXPERTS
N_COND = 1 + DEC_BATCH
N_MOD = 6

V7X_LANES = 128
V7X_SUBLANES = 8
V7X_VMEM_BYTES = 64 * 1024 * 1024
MIB = 1024 * 1024

F32 = jnp.float32
BF16 = jnp.bfloat16
NT_DIMS = (((1,), (1,)), ((), ()))

TOKEN_TILE = 256
MOD_COL_TILE = 1024
FFN_COL_TILE = 512
FFN_DOWN_COL_TILE = 1024
INPROJ_TILE = 512
OUTPROJ_TILE = 512
OUTPROJ_SUBTILE = 256
CTX_ATTN_REQUESTS = 1
CTX_MOE_REQUESTS = 2
ROUTE_REQUESTS_CTX = 8
NA_GROUP_ROWS = 4
NA_WIN_ROWS = 12
NA_REL_ROWS = 2 * NA_ROWS - 1
NA_REL_COLS = 2 * NA_COLS - 1


def _params(vmem_mib, n_axes):
    return pltpu.CompilerParams(dimension_semantics=("arbitrary",) * n_axes,
                                vmem_limit_bytes=vmem_mib * MIB)


def _dot(a, b):
    return jnp.dot(a, b, preferred_element_type=F32)


def _dot_nt(a, b):
    return lax.dot_general(a, b, NT_DIMS, preferred_element_type=F32)


def _layer_norm(z, g, b):
    mu = jnp.mean(z, axis=-1, keepdims=True)
    zc = z - mu
    var = jnp.mean(zc * zc, axis=-1, keepdims=True)
    return zc * lax.rsqrt(var + LN_EPS) * g + b


def _mod_kernel(cond_ref, w_ref, b_ref, o_ref):
    s = jax.nn.silu(cond_ref[...]).astype(BF16)
    o_ref[...] = _dot(s, w_ref[...].astype(BF16)) + b_ref[...]


def _modulation(cond, w_mod, b_mod):
    n_out = N_MOD * D_MODEL
    return pl.pallas_call(
        _mod_kernel,
        grid=(n_out // MOD_COL_TILE,),
        in_specs=[pl.BlockSpec((V7X_SUBLANES, D_MODEL), lambda j: (0, 0)),
                  pl.BlockSpec((D_MODEL, MOD_COL_TILE), lambda j: (0, j)),
                  pl.BlockSpec((1, MOD_COL_TILE), lambda j: (0, j))],
        out_specs=pl.BlockSpec((V7X_SUBLANES, MOD_COL_TILE), lambda j: (0, j)),
        out_shape=jax.ShapeDtypeStruct((V7X_SUBLANES, n_out), F32),
        compiler_params=_params(40, 1),
        name="modulation",
    )(cond, w_mod, b_mod)


def _mod_spec(which, latent, tile=TOKEN_TILE):
    steps_per_request = DEC_SEQ // tile
    if latent:
        return pl.BlockSpec((None, 1, D_MODEL), lambda i: ((1 + i // steps_per_request) * N_MOD + which, 0, 0))
    return pl.BlockSpec((None, 1, D_MODEL), lambda i: (which, 0, 0))


def _rope(x, cos, sin, even):
    swapped = jnp.where(even, pltpu.roll(x, HEAD_DIM - 1, 1), pltpu.roll(x, 1, 1))
    return x * cos + swapped * sin


def _inproj_kernel(*refs, latent):
    if latent:
        (x_ref, shift_ref, scale_ref, w_ref, cos_ref, sin_ref,
         qa_o, ka_o, va_o, qb_o, kb_o, vb_o) = refs
    else:
        (x_ref, shift_ref, scale_ref, w_ref,
         qa_o, ka_o, va_o, qb_o, kb_o, vb_o, kaf_o, vaf_o, kbf_o, vbf_o) = refs
    h = (x_ref[...] * (1.0 + scale_ref[...]) + shift_ref[...]).astype(BF16)

    def proj(piece):
        return _dot(h, w_ref[:, COL_STARTS[piece]:COL_STARTS[piece + 1]])

    if latent:
        cos = cos_ref[...]
        sin = sin_ref[...]
        even = (lax.broadcasted_iota(jnp.int32, cos.shape, 1) & 1) == 0

    def heads(y, o_ref, n_heads, scale, rope):
        for hd in range(n_heads):
            cols = slice(hd * HEAD_DIM, (hd + 1) * HEAD_DIM)
            yh = y[:, cols]
            if rope:
                yh = _rope(yh, cos, sin, even)
            if scale != 1.0:
                yh = yh * scale
            o_ref[:, cols] = yh.astype(BF16)

    qa = proj(0)
    heads(qa, qa_o, N_HEADS_A, ATTN_SCALE, latent)
    ka = proj(1)
    heads(ka, ka_o, N_KV_A, 1.0, latent)
    va = proj(2)
    va_o[...] = va.astype(BF16)
    qb = proj(3)
    qb_o[...] = (qb * ATTN_SCALE).astype(BF16)
    kb = proj(4)
    kb_o[...] = kb.astype(BF16)
    vb = proj(5)
    vb_o[...] = vb.astype(BF16)
    if not latent:
        kaf_o[...] = ka
        vaf_o[...] = va
        kbf_o[...] = kb
        vbf_o[...] = vb


def _input_projection(x, mod3, w_in_bf16, rope_tables, latent):
    n_tok = x.shape[0]
    tm = INPROJ_TILE
    row = lambda i: (i, 0)
    in_specs = [pl.BlockSpec((tm, D_MODEL), row), _mod_spec(0, latent, tm), _mod_spec(1, latent, tm),
                pl.BlockSpec((D_MODEL, IN_COLS), lambda i: (0, 0), pipeline_mode=pl.Buffered(1))]
    args = [x, mod3, mod3, w_in_bf16]
    widths = [Q_A_COLS, KV_A_COLS, KV_A_COLS, Q_B_COLS, Q_B_COLS, Q_B_COLS]
    out_specs = [pl.BlockSpec((tm, w), row) for w in widths]
    out_shape = [jax.ShapeDtypeStruct((n_tok, w), BF16) for w in widths]
    if latent:
        steps_per_request = DEC_SEQ // tm
        in_specs += [pl.BlockSpec((tm, HEAD_DIM), lambda i: (i % steps_per_request, 0))] * 2
        args += list(rope_tables)
    else:
        kv_widths = [KV_A_COLS, KV_A_COLS, Q_B_COLS, Q_B_COLS]
        out_specs += [pl.BlockSpec((tm, w), row) for w in kv_widths]
        out_shape += [jax.ShapeDtypeStruct((n_tok, w), F32) for w in kv_widths]
    return pl.pallas_call(
        functools.partial(_inproj_kernel, latent=latent),
        grid=(n_tok // tm,),
        in_specs=in_specs, out_specs=out_specs, out_shape=out_shape,
        compiler_params=_params(58, 1),
        name="inproj_latent" if latent else "inproj_context",
    )(*args)


def _with_ones(v):
    return jnp.concatenate([v, jnp.ones((v.shape[0], V7X_LANES), v.dtype)], axis=1)


def _softmax_av(scores, values, sink=None):
    m = functools.reduce(jnp.maximum, [jnp.max(s, axis=-1, keepdims=True) for s in scores])
    if sink is not None:
        m = jnp.maximum(m, sink)
    acc = functools.reduce(jnp.add, [_dot(jnp.exp(s - m).astype(BF16), _with_ones(v))
                                     for s, v in zip(scores, values)])
    denom = acc[:, HEAD_DIM:]
    if sink is not None:
        denom = denom + jnp.exp(sink - m)
    return acc[:, :HEAD_DIM] / denom


def _sink_column(sink_ref, first_head, n_heads, rows):
    return jnp.concatenate([jnp.full((rows, 1), sink_ref[first_head + g], F32) for g in range(n_heads)], axis=0)


def _ctx_attn_kernel(sink_ref, qa_ref, ka_ref, va_ref, qb_ref, kb_ref, vb_ref, oa_ref, ob_ref):
    for req in range(qa_ref.shape[0] // SEQ):
        rows = slice(req * SEQ, (req + 1) * SEQ)
        for kv in range(N_KV_A):
            cols = slice(kv * HEAD_DIM, (kv + 1) * HEAD_DIM)
            q = jnp.concatenate([qa_ref[rows, (kv * GROUP_A + g) * HEAD_DIM:(kv * GROUP_A + g + 1) * HEAD_DIM]
                                 for g in range(GROUP_A)], axis=0)
            sink = _sink_column(sink_ref, kv * GROUP_A, GROUP_A, SEQ)
            o = _softmax_av([_dot_nt(q, ka_ref[rows, cols])], [va_ref[rows, cols]], sink)
            for g in range(GROUP_A):
                hd = kv * GROUP_A + g
                oa_ref[rows, hd * HEAD_DIM:(hd + 1) * HEAD_DIM] = o[g * SEQ:(g + 1) * SEQ].astype(BF16)
        for hd in range(N_HEADS_B):
            cols = slice(hd * HEAD_DIM, (hd + 1) * HEAD_DIM)
            o = _softmax_av([_dot_nt(qb_ref[rows, cols], kb_ref[rows, cols])], [vb_ref[rows, cols]])
            ob_ref[rows, cols] = o.astype(BF16)


def _context_attention(sink, qa, ka, va, qb, kb, vb):
    row = lambda b: (b, 0)
    rows = CTX_ATTN_REQUESTS * SEQ
    widths = [Q_A_COLS, KV_A_COLS, KV_A_COLS, Q_B_COLS, Q_B_COLS, Q_B_COLS]
    return pl.pallas_call(
        _ctx_attn_kernel,
        grid=(BATCH // CTX_ATTN_REQUESTS,),
        in_specs=[pl.BlockSpec(memory_space=pltpu.SMEM)] + [pl.BlockSpec((rows, w), row) for w in widths],
        out_specs=[pl.BlockSpec((rows, Q_A_COLS), row), pl.BlockSpec((rows, Q_B_COLS), row)],
        out_shape=[jax.ShapeDtypeStruct((N_CTX_TOK, Q_A_COLS), BF16),
                   jax.ShapeDtypeStruct((N_CTX_TOK, Q_B_COLS), BF16)],
        compiler_params=_params(32, 1),
        name="context_attention",
    )(sink, qa, ka, va, qb, kb, vb)


def _window_attn_kernel(sink_ref, q_ref, k_ref, v_ref, ck_ref, cv_ref, o_ref):
    kv = pl.program_id(1)
    span = 3 * BLOCK_A
    rows = GROUP_A * BLOCK_A
    sink = jnp.concatenate([jnp.full((BLOCK_A, 1), sink_ref[kv * GROUP_A + g], F32) for g in range(GROUP_A)],
                           axis=0)
    ck = ck_ref[...].astype(BF16)
    cv = cv_ref[...].astype(BF16)
    band = (lax.broadcasted_iota(jnp.int32, (rows, span), 1)
            - (lax.broadcasted_iota(jnp.int32, (rows, span), 0) & (BLOCK_A - 1)))
    masks = {}
    for n in range(DEC_SEQ // BLOCK_A):
        start = min(max((n - 1) * BLOCK_A, 0), DEC_SEQ - span)
        offset = start - n * BLOCK_A
        if offset not in masks:
            masks[offset] = jnp.where(jnp.abs(band + offset) <= WINDOW_A, 0.0, NEG_INF)
        q = jnp.concatenate([q_ref[n * BLOCK_A:(n + 1) * BLOCK_A, g * HEAD_DIM:(g + 1) * HEAD_DIM]
                             for g in range(GROUP_A)], axis=0)
        s_loc = _dot_nt(q, k_ref[start:start + span, :]) + masks[offset]
        o = _softmax_av([s_loc, _dot_nt(q, ck)], [v_ref[start:start + span, :], cv], sink)
        for g in range(GROUP_A):
            o_ref[n * BLOCK_A:(n + 1) * BLOCK_A, g * HEAD_DIM:(g + 1) * HEAD_DIM] = (
                o[g * BLOCK_A:(g + 1) * BLOCK_A].astype(BF16))


def _window_attention(sink, qa, ka, va, ck, cv):
    q_spec = pl.BlockSpec((DEC_SEQ, GROUP_A * HEAD_DIM), lambda b, kv: (b, kv))
    kv_spec = pl.BlockSpec((DEC_SEQ, HEAD_DIM), lambda b, kv: (b, kv))
    c_spec = pl.BlockSpec((PAST_LEN, HEAD_DIM), lambda b, kv: (b, kv))
    return pl.pallas_call(
        _window_attn_kernel,
        grid=(DEC_BATCH, N_KV_A),
        in_specs=[pl.BlockSpec(memory_space=pltpu.SMEM), q_spec, kv_spec, kv_spec, c_spec, c_spec],
        out_specs=q_spec,
        out_shape=jax.ShapeDtypeStruct((N_LAT_TOK, Q_A_COLS), BF16),
        compiler_params=_params(32, 2),
        name="window_attention",
    )(sink, qa, ka, va, ck, cv)


def _na_row_start(rq):
    return min(max(rq - NA_ROWS // 2, 0), GRID_ROWS - NA_ROWS)


def _na_window(group):
    first = _na_row_start(group * NA_GROUP_ROWS)
    last = _na_row_start(group * NA_GROUP_ROWS + NA_GROUP_ROWS - 1) + NA_ROWS - 1
    n_rows = last - first + 1
    n_rows += n_rows % 2
    assert first % 2 == 0 and n_rows <= NA_WIN_ROWS and first + n_rows <= GRID_ROWS
    return first, n_rows


def _na_build_bias(rb_ref, bias_ref, head):
    shape = (GRID_W, 2 * GRID_W)
    lane = lax.broadcasted_iota(jnp.int32, shape, 1)
    qcol = lax.broadcasted_iota(jnp.int32, shape, 0)
    kcol = lane & (GRID_W - 1)
    left = lane < GRID_W
    dcol = kcol - qcol + (NA_COLS - 1)
    wstart = jnp.clip(qcol - NA_COLS // 2, 0, GRID_W - NA_COLS)
    col_ok = (kcol >= wstart) & (kcol < wstart + NA_COLS)
    base = head * (NA_REL_ROWS * NA_REL_COLS)
    toeplitz = []
    for a in range(NA_REL_ROWS):
        acc = jnp.zeros(shape, F32)
        for bb in range(NA_REL_COLS):
            acc = jnp.where(dcol == bb, rb_ref[base + a * NA_REL_COLS + bb], acc)
        toeplitz.append(acc)
    neg = jnp.full(shape, NEG_INF, F32)
    for g in range(GRID_ROWS // NA_GROUP_ROWS):
        wrow, n_rows = _na_window(g)
        for qi in range(NA_GROUP_ROWS):
            rq = g * NA_GROUP_ROWS + qi
            row0 = _na_row_start(rq)
            for p in range(n_rows // 2):
                rk0 = wrow + 2 * p
                ok0 = row0 <= rk0 < row0 + NA_ROWS
                ok1 = row0 <= rk0 + 1 < row0 + NA_ROWS
                a0 = rk0 - rq + NA_ROWS - 1
                if ok0 and ok1:
                    blk = jnp.where(col_ok, jnp.where(left, toeplitz[a0], toeplitz[a0 + 1]), neg)
                elif ok0:
                    blk = jnp.where(col_ok & left, toeplitz[a0], neg)
                elif ok1:
                    blk = jnp.where(col_ok & jnp.logical_not(left), toeplitz[a0 + 1], neg)
                else:
                    blk = neg
                bias_ref[g, qi * GRID_W:(qi + 1) * GRID_W, p * 2 * GRID_W:(p + 1) * 2 * GRID_W] = blk


def _na_attn_kernel(rb_ref, q_ref, k_ref, v_ref, ck_ref, cv_ref, o_ref, bias_ref):
    head = pl.program_id(0)

    @pl.when(pl.program_id(1) == 0)
    def _():
        _na_build_bias(rb_ref, bias_ref, head)

    ck = ck_ref[...].astype(BF16)
    cv = cv_ref[...].astype(BF16)
    q_rows = NA_GROUP_ROWS * GRID_W
    for g in range(GRID_ROWS // NA_GROUP_ROWS):
        first, n_rows = _na_window(g)
        k0, k_rows = first * GRID_W, n_rows * GRID_W
        q = q_ref[g * q_rows:(g + 1) * q_rows, :]
        s_loc = _dot_nt(q, k_ref[k0:k0 + k_rows, :]) + bias_ref[g, :, :k_rows]
        s_ctx = _dot_nt(q, ck)
        o = _softmax_av([s_loc, s_ctx], [v_ref[k0:k0 + k_rows, :], cv])
        o_ref[g * q_rows:(g + 1) * q_rows, :] = o.astype(BF16)


def _neighborhood_attention(rel_bias_flat, qb, kb, vb, ck, cv):
    tok_spec = pl.BlockSpec((DEC_SEQ, HEAD_DIM), lambda h, b: (b, h))
    c_spec = pl.BlockSpec((PAST_LEN, HEAD_DIM), lambda h, b: (b, h))
    n_groups = GRID_ROWS // NA_GROUP_ROWS
    return pl.pallas_call(
        _na_attn_kernel,
        grid=(N_HEADS_B, DEC_BATCH),
        in_specs=[pl.BlockSpec(memory_space=pltpu.SMEM), tok_spec, tok_spec, tok_spec, c_spec, c_spec],
        out_specs=tok_spec,
        out_shape=jax.ShapeDtypeStruct((N_LAT_TOK, Q_B_COLS), BF16),
        scratch_shapes=[pltpu.VMEM((n_groups, NA_GROUP_ROWS * GRID_W, NA_WIN_ROWS * GRID_W), F32)],
        compiler_params=_params(32, 2),
        name="neighborhood_attention",
    )(rel_bias_flat, qb, kb, vb, ck, cv)


def _outproj_kernel(oa_ref, ob_ref, wo_ref, x_ref, gate1_ref, shift2_ref, scale2_ref, g_ref, b_ref, wr_ref,
                    x1_ref, h2_ref, aff_ref):
    lanes = aff_ref.shape[2]
    for r0 in range(0, OUTPROJ_TILE, OUTPROJ_SUBTILE):
        rows = slice(r0, r0 + OUTPROJ_SUBTILE)
        attn = _dot(oa_ref[rows, :], wo_ref[:Q_A_COLS, :]) + _dot(ob_ref[rows, :], wo_ref[Q_A_COLS:, :])
        x1 = _layer_norm(DEEPNORM_ALPHA * x_ref[rows, :] + gate1_ref[...] * attn, g_ref[...], b_ref[...])
        x1_ref[rows, :] = x1
        h2 = (x1 * (1.0 + scale2_ref[...]) + shift2_ref[...]).astype(BF16)
        h2_ref[rows, :] = h2
        lt = _dot(h2, wr_ref[...]).T[:N_EXPERTS, :]
        e = jnp.exp(lt - jnp.max(lt, axis=0, keepdims=True))
        req, l0 = divmod(r0, lanes)
        aff_ref[req, :, l0:l0 + OUTPROJ_SUBTILE] = e / jnp.sum(e, axis=0, keepdims=True)


def _output_projection(oa, ob, wo_bf16, x, mod3, ln_g, ln_b, wr_pad, seq, latent):
    n_tok = x.shape[0]
    tm = OUTPROJ_TILE
    row = lambda i: (i, 0)
    const = lambda i: (0, 0)
    if seq >= tm:
        tiles_per_request = seq // tm
        aff_spec = pl.BlockSpec((1, N_EXPERTS, tm), lambda i: (i // tiles_per_request, 0, i % tiles_per_request))
    else:
        aff_spec = pl.BlockSpec((tm // seq, N_EXPERTS, seq), lambda i: (i, 0, 0))
    return pl.pallas_call(
        _outproj_kernel,
        grid=(n_tok // tm,),
        in_specs=[pl.BlockSpec((tm, Q_A_COLS), row), pl.BlockSpec((tm, Q_B_COLS), row),
                  pl.BlockSpec((Q_A_COLS + Q_B_COLS, D_MODEL), const, pipeline_mode=pl.Buffered(1)),
                  pl.BlockSpec((tm, D_MODEL), row),
                  _mod_spec(2, latent, tm), _mod_spec(3, latent, tm), _mod_spec(4, latent, tm),
                  pl.BlockSpec((1, D_MODEL), const), pl.BlockSpec((1, D_MODEL), const),
                  pl.BlockSpec((D_MODEL, V7X_LANES), const)],
        out_specs=[pl.BlockSpec((tm, D_MODEL), row), pl.BlockSpec((tm, D_MODEL), row), aff_spec],
        out_shape=[jax.ShapeDtypeStruct((n_tok, D_MODEL), F32), jax.ShapeDtypeStruct((n_tok, D_MODEL), BF16),
                   jax.ShapeDtypeStruct((n_tok // seq, N_EXPERTS, seq), F32)],
        compiler_params=_params(56, 1),
        name="outproj_latent" if latent else "outproj_context",
    )(oa, ob, wo_bf16, x, mod3, mod3, mod3, ln_g, ln_b, wr_pad)


def _pad_rows_to_lanes(a):
    return jnp.concatenate([a, jnp.zeros((V7X_LANES - a.shape[0], a.shape[1]), a.dtype)], axis=0)


def _kth_largest(a, k):
    n = a.shape[1]
    lane = lax.broadcasted_iota(jnp.int32, (1, n), 1)
    x = a
    size = 2
    while size <= n:
        descending = (lane & size) == 0
        j = size // 2
        while j >= 1:
            lower = (lane & j) == 0
            partner = jnp.where(lower, pltpu.roll(x, n - j, 1), pltpu.roll(x, j, 1))
            x = jnp.where(lower == descending, jnp.maximum(x, partner), jnp.minimum(x, partner))
            j //= 2
        size *= 2
    return x[:, k - 1:k]


def _route_kernel(aff_ref, slot_ref, *, cap):
    n_req, n_exp, seq = aff_ref.shape
    key = aff_ref[...].reshape(n_req * n_exp, seq)
    tau = _kth_largest(key, cap)
    above = key > tau
    tied = key == tau
    need = cap - jnp.sum(jnp.where(above, 1.0, 0.0), axis=-1, keepdims=True)
    before = jnp.where(lax.broadcasted_iota(jnp.int32, (seq, seq), 0) < lax.broadcasted_iota(jnp.int32, (seq, seq), 1),
                       1.0, 0.0).astype(BF16)
    tied_before = _dot(jnp.where(tied, 1.0, 0.0).astype(BF16), before)
    chosen = above | (tied & (tied_before < need))
    slot = _dot(jnp.where(chosen, 1.0, 0.0).astype(BF16), before)
    slot_ref[...] = jnp.where(chosen, slot, float(cap)).reshape(n_req, n_exp, seq)


def _route(aff, cap, requests_per_step):
    n_req, _, seq = aff.shape
    spec = pl.BlockSpec((requests_per_step, N_EXPERTS, seq), lambda r: (r, 0, 0))
    return pl.pallas_call(
        functools.partial(_route_kernel, cap=cap),
        grid=(n_req // requests_per_step,),
        in_specs=[spec], out_specs=spec,
        out_shape=jax.ShapeDtypeStruct(aff.shape, F32),
        compiler_params=_params(32, 1),
        name=f"route_s{seq}",
    )(aff)


def _dispatch_kernel(slot_ref, aff_ref, h2_ref, xs_ref, gate_ref, slot_t_ref, *, cap, experts_per_dot):
    n_req, _, seq = slot_ref.shape
    slot = lax.broadcasted_iota(jnp.int32, (cap, seq), 0).astype(F32)
    for r in range(n_req):
        slot_of = slot_ref[r]
        aff = aff_ref[r]
        rows = slice(r * seq, (r + 1) * seq)
        slot_t_ref[rows, :] = _pad_rows_to_lanes(slot_of).T
        h2 = h2_ref[rows, :]
        for c in range(N_EXPERTS // experts_per_dot):
            chunk = range(c * experts_per_dot, (c + 1) * experts_per_dot)
            onehot = jnp.concatenate([jnp.where(slot_of[e:e + 1, :] == slot, 1.0, 0.0) for e in chunk], axis=0)
            xs = _dot(onehot.astype(BF16), h2)
            for i, e in enumerate(chunk):
                xs_ref[r, e] = xs[i * cap:(i + 1) * cap].astype(BF16)
                gate = jnp.sum(onehot[i * cap:(i + 1) * cap] * aff[e:e + 1, :], axis=1, keepdims=True)
                gate_ref[r, e] = jnp.broadcast_to(gate, (cap, V7X_LANES))


def _dispatch(slot, aff, h2, cap, experts_per_dot, requests_per_step):
    n_req, _, seq = aff.shape
    g = requests_per_step
    req_spec = pl.BlockSpec((g, N_EXPERTS, seq), lambda r: (r, 0, 0))
    return pl.pallas_call(
        functools.partial(_dispatch_kernel, cap=cap, experts_per_dot=experts_per_dot),
        grid=(n_req // g,),
        in_specs=[req_spec, req_spec, pl.BlockSpec((g * seq, D_MODEL), lambda r: (r, 0))],
        out_specs=[pl.BlockSpec((g, N_EXPERTS, cap, D_MODEL), lambda r: (r, 0, 0, 0)),
                   pl.BlockSpec((g, N_EXPERTS, cap, V7X_LANES), lambda r: (r, 0, 0, 0)),
                   pl.BlockSpec((g * seq, V7X_LANES), lambda r: (r, 0))],
        out_shape=[jax.ShapeDtypeStruct((n_req, N_EXPERTS, cap, D_MODEL), BF16),
                   jax.ShapeDtypeStruct((n_req, N_EXPERTS, cap, V7X_LANES), F32),
                   jax.ShapeDtypeStruct((n_req * seq, V7X_LANES), F32)],
        compiler_params=_params(48, 1),
        name=f"dispatch_s{seq}",
    )(slot, aff, h2)


def _expert_rows(ref):
    n_req, cap, width = ref.shape
    return ref[...].reshape(n_req * cap, width)


def _ffn_up_kernel(xc_ref, xl_ref, wg_ref, wu_ref, hc_ref, hl_ref):
    wg = wg_ref[...].astype(BF16)
    wu = wu_ref[...].astype(BF16)
    for x_ref, h_ref in ((xc_ref, hc_ref), (xl_ref, hl_ref)):
        x = _expert_rows(x_ref)
        h_ref[...] = (jax.nn.silu(_dot(x, wg)) * _dot(x, wu)).astype(BF16)


def _ffn_down_kernel(hc_ref, hl_ref, wd_ref, gc_ref, gl_ref, yc_ref, yl_ref):
    wd = wd_ref[...].astype(BF16)
    for h_ref, g_ref, y_ref in ((hc_ref, gc_ref, yc_ref), (hl_ref, gl_ref, yl_ref)):
        gate = jnp.concatenate([_expert_rows(g_ref)] * (wd.shape[1] // V7X_LANES), axis=1)
        y_ref[...] = (_dot(h_ref[...], wd) * gate).astype(BF16).reshape(y_ref.shape)


def _expert_ffn(xs_ctx, xs_lat, gate_ctx, gate_lat, w_gate, w_up, w_down):
    def per_expert(a, width):
        n_req, _, cap, _ = a.shape
        return pl.BlockSpec((n_req, None, cap, width), lambda e, j: (0, e, 0, 0 if width == a.shape[3] else j))

    rows_c = xs_ctx.shape[0] * xs_ctx.shape[2]
    rows_l = xs_lat.shape[0] * xs_lat.shape[2]
    tn = FFN_COL_TILE
    hid_c = pl.BlockSpec((None, rows_c, tn), lambda e, j: (e, 0, j))
    hid_l = pl.BlockSpec((None, rows_l, tn), lambda e, j: (e, 0, j))
    hidden_c, hidden_l = pl.pallas_call(
        _ffn_up_kernel,
        grid=(N_EXPERTS, D_FF // tn),
        in_specs=[per_expert(xs_ctx, D_MODEL), per_expert(xs_lat, D_MODEL),
                  pl.BlockSpec((None, D_MODEL, tn), lambda e, j: (e, 0, j)),
                  pl.BlockSpec((None, D_MODEL, tn), lambda e, j: (e, 0, j))],
        out_specs=[hid_c, hid_l],
        out_shape=[jax.ShapeDtypeStruct((N_EXPERTS, rows_c, D_FF), BF16),
                   jax.ShapeDtypeStruct((N_EXPERTS, rows_l, D_FF), BF16)],
        compiler_params=_params(56, 2),
        name="ffn_up",
    )(xs_ctx, xs_lat, w_gate, w_up)
    tn = FFN_DOWN_COL_TILE
    return pl.pallas_call(
        _ffn_down_kernel,
        grid=(N_EXPERTS, D_MODEL // tn),
        in_specs=[pl.BlockSpec((None, rows_c, D_FF), lambda e, j: (e, 0, 0)),
                  pl.BlockSpec((None, rows_l, D_FF), lambda e, j: (e, 0, 0)),
                  pl.BlockSpec((None, D_FF, tn), lambda e, j: (e, 0, j)),
                  per_expert(gate_ctx, V7X_LANES), per_expert(gate_lat, V7X_LANES)],
        out_specs=[per_expert(xs_ctx, tn), per_expert(xs_lat, tn)],
        out_shape=[jax.ShapeDtypeStruct(xs_ctx.shape, BF16), jax.ShapeDtypeStruct(xs_lat.shape, BF16)],
        compiler_params=_params(56, 2),
        name="ffn_down",
    )(hidden_c, hidden_l, w_down, gate_ctx, gate_lat)


def _combine_kernel(slot_t_ref, y_ref, x1_ref, gate2_ref, g_ref, b_ref, o_ref, *, cap, experts_per_dot):
    n_req = y_ref.shape[0]
    ts = slot_t_ref.shape[0] // n_req
    cols = experts_per_dot * cap
    shift = cap.bit_length() - 1
    erow = lax.broadcasted_iota(jnp.int32, (V7X_LANES, cols), 0)
    ecol = lax.shift_right_logical(lax.broadcasted_iota(jnp.int32, (V7X_LANES, cols), 1), shift)
    slot = (lax.broadcasted_iota(jnp.int32, (ts, cols), 1) & (cap - 1)).astype(F32)
    for r in range(n_req):
        rows = slice(r * ts, (r + 1) * ts)
        rk = slot_t_ref[rows, :].astype(BF16)
        ffn = None
        for c in range(N_EXPERTS // experts_per_dot):
            expand = jnp.where(erow == ecol + c * experts_per_dot, 1.0, 0.0).astype(BF16)
            onehot_t = jnp.where(_dot(rk, expand) == slot, 1.0, 0.0).astype(BF16)
            y = y_ref[r, c * experts_per_dot:(c + 1) * experts_per_dot].reshape(cols, D_MODEL)
            part = _dot(onehot_t, y)
            ffn = part if ffn is None else ffn + part
        o_ref[rows, :] = _layer_norm(DEEPNORM_ALPHA * x1_ref[rows, :] + gate2_ref[...] * ffn,
                                     g_ref[...], b_ref[...])


def _combine(slot_t, y, x1, mod3, ln_g, ln_b, n_req, seq, cap, experts_per_dot, requests_per_step, latent):
    ts = TOKEN_TILE
    steps = seq // ts
    g = requests_per_step
    assert g == 1 or (steps == 1 and not latent)
    row = lambda r, t: (r * steps + t, 0)
    const = lambda r, t: (0, 0)
    mod_row = (lambda r, t: ((1 + r) * N_MOD + 5, 0, 0)) if latent else (lambda r, t: (5, 0, 0))
    return pl.pallas_call(
        functools.partial(_combine_kernel, cap=cap, experts_per_dot=experts_per_dot),
        grid=(n_req // g, steps),
        in_specs=[pl.BlockSpec((g * ts, V7X_LANES), row),
                  pl.BlockSpec((g, N_EXPERTS, cap, D_MODEL), lambda r, t: (r, 0, 0, 0)),
                  pl.BlockSpec((g * ts, D_MODEL), row),
                  pl.BlockSpec((None, 1, D_MODEL), mod_row),
                  pl.BlockSpec((1, D_MODEL), const), pl.BlockSpec((1, D_MODEL), const)],
        out_specs=pl.BlockSpec((g * ts, D_MODEL), row),
        out_shape=jax.ShapeDtypeStruct((n_req * seq, D_MODEL), F32),
        compiler_params=_params(48, 2),
        name=f"combine_s{seq}",
    )(slot_t, y, x1, mod3, ln_g, ln_b)


def _rope_tables():
    t = jnp.arange(DEC_SEQ)
    row = (t // GRID_W).astype(F32)
    col = (t % GRID_W).astype(F32)
    n_freq = HEAD_DIM // 4
    inv_freq = ROPE_THETA ** (-jnp.arange(n_freq, dtype=F32) / n_freq)
    ang = jnp.concatenate([row[:, None] * inv_freq, col[:, None] * inv_freq], axis=-1)
    cos = jnp.repeat(jnp.cos(ang), 2, axis=-1)
    sin = jnp.stack([-jnp.sin(ang), jnp.sin(ang)], axis=-1).reshape(DEC_SEQ, HEAD_DIM)
    return cos, sin


def kernel(x_prompt, x_sample, cache_k_a, cache_v_a, cache_k_b, cache_v_b, c, c_ctx, w_mod, b_mod, w_in, w_o,
           sink_a, rel_bias_b, w_router, w_gate, w_up, w_down, ln1_g, ln1_b, ln2_g, ln2_b):
    assert DEPTH == 1 and w_in.shape == (DEPTH, D_MODEL, IN_COLS)
    x_ctx = x_prompt.reshape(N_CTX_TOK, D_MODEL)
    x_lat = x_sample.reshape(N_LAT_TOK, D_MODEL)

    cond = jnp.concatenate([c_ctx[None, :], c, jnp.zeros((V7X_SUBLANES - N_COND, D_MODEL), F32)], axis=0)
    mod = _modulation(cond, w_mod[0], b_mod)
    mod3 = mod.reshape(V7X_SUBLANES * N_MOD, 1, D_MODEL)

    w_in_bf16 = w_in[0].astype(BF16)
    wo_bf16 = w_o[0].astype(BF16)
    wr_pad = jnp.pad(w_router[0], ((0, 0), (0, V7X_LANES - N_EXPERTS))).astype(BF16)
    sink = sink_a[0]
    ln1 = (ln1_g, ln1_b)
    ln2 = (ln2_g, ln2_b)

    qa, ka, va, qb, kb, vb, ka_f, va_f, kb_f, vb_f = _input_projection(x_ctx, mod3, w_in_bf16, None, False)
    oa, ob = _context_attention(sink, qa, ka, va, qb, kb, vb)
    x1_ctx, h2_ctx, aff_ctx = _output_projection(oa, ob, wo_bf16, x_ctx, mod3, *ln1, wr_pad, SEQ, False)
    slot_ctx = _route(aff_ctx, CAP_CTX, ROUTE_REQUESTS_CTX)
    xs_ctx, gate_ctx, slot_t_ctx = _dispatch(slot_ctx, aff_ctx, h2_ctx, CAP_CTX, N_EXPERTS, CTX_MOE_REQUESTS)

    qa, ka, va, qb, kb, vb = _input_projection(x_lat, mod3, w_in_bf16, _rope_tables(), True)
    oa = _window_attention(sink, qa, ka, va,
                           cache_k_a.reshape(DEC_BATCH * PAST_LEN, KV_A_COLS),
                           cache_v_a.reshape(DEC_BATCH * PAST_LEN, KV_A_COLS))
    ob = _neighborhood_attention(rel_bias_b.reshape(-1), qb, kb, vb,
                                 cache_k_b.reshape(DEC_BATCH * PAST_LEN, Q_B_COLS),
                                 cache_v_b.reshape(DEC_BATCH * PAST_LEN, Q_B_COLS))
    x1_lat, h2_lat, aff_lat = _output_projection(oa, ob, wo_bf16, x_lat, mod3, *ln1, wr_pad, DEC_SEQ, True)
    slot_lat = _route(aff_lat, CAP_LAT, DEC_BATCH)
    xs_lat, gate_lat, slot_t_lat = _dispatch(slot_lat, aff_lat, h2_lat, CAP_LAT, 4, 1)

    y_ctx, y_lat = _expert_ffn(xs_ctx, xs_lat, gate_ctx, gate_lat, w_gate[0], w_up[0], w_down[0])

    y_p = _combine(slot_t_ctx, y_ctx, x1_ctx, mod3, *ln2, BATCH, SEQ, CAP_CTX, N_EXPERTS, CTX_MOE_REQUESTS, False)
    y_s = _combine(slot_t_lat, y_lat, x1_lat, mod3, *ln2, DEC_BATCH, DEC_SEQ, CAP_LAT, 4, 1, True)

    kv_a_shape = (BATCH, DEPTH, SEQ, N_KV_A, HEAD_DIM)
    kv_b_shape = (BATCH, DEPTH, SEQ, N_HEADS_B, HEAD_DIM)
    return (y_p.reshape(BATCH, SEQ, D_MODEL), y_s.reshape(DEC_BATCH, DEC_SEQ, D_MODEL),
            ka_f.reshape(kv_a_shape), va_f.reshape(kv_a_shape), kb_f.reshape(kv_b_shape), vb_f.reshape(kv_b_shape))
```

```python
import functools

import jax
import jax.numpy as jnp
from jax import lax
from jax.experimental import pallas as pl
from jax.experimental.pallas import tpu as pltpu

D_MODEL = 2048
BATCH = 32
SEQ = 256
DEC_BATCH = 4
DEC_SEQ = 1024
PAST_LEN = 256
GRID_W = 64
HEAD_DIM = 128
N_HEADS_A = 8
N_KV_A = 2
GROUP_A = N_HEADS_A // N_KV_A
N_HEADS_B = 8
WINDOW_A = 128
BLOCK_A = 128
NA_ROWS = 8
NA_COLS = 16
N_EXPERTS = 16
EC_CAPACITY = 2
D_FF = D_MODEL
ROPE_THETA = 10000.0
LN_EPS = 1e-5
NEG_INF = -1e30
DEPTH = 1
Q_A_COLS = N_HEADS_A * HEAD_DIM
KV_A_COLS = N_KV_A * HEAD_DIM
Q_B_COLS = N_HEADS_B * HEAD_DIM
IN_COLS = Q_A_COLS + 2 * KV_A_COLS + 3 * Q_B_COLS
COL_STARTS = (0, Q_A_COLS, Q_A_COLS + KV_A_COLS, Q_A_COLS + 2 * KV_A_COLS,
              Q_A_COLS + 2 * KV_A_COLS + Q_B_COLS, Q_A_COLS + 2 * KV_A_COLS + 2 * Q_B_COLS, IN_COLS)
DEEPNORM_ALPHA = (2.0 * DEPTH) ** 0.25
ATTN_SCALE = HEAD_DIM ** -0.5

N_CTX_TOK = BATCH * SEQ
N_LAT_TOK = DEC_BATCH * DEC_SEQ
GRID_ROWS = DEC_SEQ // GRID_W
CAP_CTX = EC_CAPACITY * SEQ // N_EXPERTS
CAP_LAT = EC_CAPACITY * DEC_SEQ // N_EXPERTS
N_COND = 1 + DEC_BATCH
N_MOD = 6

V7X_LANES = 128
V7X_SUBLANES = 8
V7X_VMEM_BYTES = 64 * 1024 * 1024
MIB = 1024 * 1024

F32 = jnp.float32
BF16 = jnp.bfloat16
NT_DIMS = (((1,), (1,)), ((), ()))

TOKEN_TILE = 256
MOD_COL_TILE = 1024
FFN_COL_TILE = 512
FFN_DOWN_COL_TILE = 1024
INPROJ_TILE = 512
OUTPROJ_TILE = 512
OUTPROJ_SUBTILES = (256, 256)
CTX_ATTN_REQUESTS = 1
CTX_DISPATCH_REQUESTS = 4
CTX_COMBINE_REQUESTS = 2
ROUTE_REQUESTS_CTX = 8
NA_GROUP_ROWS = 4
NA_WIN_ROWS = 12
NA_REL_ROWS = 2 * NA_ROWS - 1
NA_REL_COLS = 2 * NA_COLS - 1


def _params(vmem_mib, n_axes):
    return pltpu.CompilerParams(dimension_semantics=("arbitrary",) * n_axes,
                                vmem_limit_bytes=vmem_mib * MIB)


def _dot(a, b):
    return jnp.dot(a, b, preferred_element_type=F32)


def _dot_nt(a, b):
    return lax.dot_general(a, b, NT_DIMS, preferred_element_type=F32)


def _layer_norm(z, g, b):
    mu = jnp.mean(z, axis=-1, keepdims=True)
    zc = z - mu
    var = jnp.mean(zc * zc, axis=-1, keepdims=True)
    return zc * lax.rsqrt(var + LN_EPS) * g + b


def _mod_kernel(cond_ref, w_ref, b_ref, o_ref):
    s = jax.nn.silu(cond_ref[...]).astype(BF16)
    o_ref[...] = _dot(s, w_ref[...].astype(BF16)) + b_ref[...]


def _modulation(cond, w_mod, b_mod):
    n_out = N_MOD * D_MODEL
    return pl.pallas_call(
        _mod_kernel,
        grid=(n_out // MOD_COL_TILE,),
        in_specs=[pl.BlockSpec((V7X_SUBLANES, D_MODEL), lambda j: (0, 0)),
                  pl.BlockSpec((D_MODEL, MOD_COL_TILE), lambda j: (0, j)),
                  pl.BlockSpec((1, MOD_COL_TILE), lambda j: (0, j))],
        out_specs=pl.BlockSpec((V7X_SUBLANES, MOD_COL_TILE), lambda j: (0, j)),
        out_shape=jax.ShapeDtypeStruct((V7X_SUBLANES, n_out), F32),
        compiler_params=_params(40, 1),
        name="modulation",
    )(cond, w_mod, b_mod)


def _mod_spec(which, latent, tile=TOKEN_TILE):
    steps_per_request = DEC_SEQ // tile
    if latent:
        return pl.BlockSpec((None, 1, D_MODEL), lambda i: ((1 + i // steps_per_request) * N_MOD + which, 0, 0))
    return pl.BlockSpec((None, 1, D_MODEL), lambda i: (which, 0, 0))


def _rope(x, cos, sin, even):
    swapped = jnp.where(even, pltpu.roll(x, HEAD_DIM - 1, 1), pltpu.roll(x, 1, 1))
    return x * cos + swapped * sin


def _inproj_kernel(*refs, latent):
    if latent:
        (x_ref, shift_ref, scale_ref, w_ref, cos_ref, sin_ref,
         qa_o, ka_o, va_o, qb_o, kb_o, vb_o) = refs
    else:
        (x_ref, shift_ref, scale_ref, w_ref,
         qa_o, ka_o, va_o, qb_o, kb_o, vb_o, kaf_o, vaf_o, kbf_o, vbf_o) = refs
    h = (x_ref[...] * (1.0 + scale_ref[...]) + shift_ref[...]).astype(BF16)

    def proj(piece):
        return _dot(h, w_ref[:, COL_STARTS[piece]:COL_STARTS[piece + 1]])

    if latent:
        cos = cos_ref[...]
        sin = sin_ref[...]
        even = (lax.broadcasted_iota(jnp.int32, cos.shape, 1) & 1) == 0

    def heads(y, o_ref, n_heads, scale, rope):
        for hd in range(n_heads):
            cols = slice(hd * HEAD_DIM, (hd + 1) * HEAD_DIM)
            yh = y[:, cols]
            if rope:
                yh = _rope(yh, cos, sin, even)
            if scale != 1.0:
                yh = yh * scale
            o_ref[:, cols] = yh.astype(BF16)

    qa = proj(0)
    heads(qa, qa_o, N_HEADS_A, ATTN_SCALE, latent)
    ka = proj(1)
    heads(ka, ka_o, N_KV_A, 1.0, latent)
    va = proj(2)
    va_o[...] = va.astype(BF16)
    qb = proj(3)
    qb_o[...] = (qb * ATTN_SCALE).astype(BF16)
    kb = proj(4)
    kb_o[...] = kb.astype(BF16)
    vb = proj(5)
    vb_o[...] = vb.astype(BF16)
    if not latent:
        for o_ref, y in ((kaf_o, ka), (vaf_o, va)):
            for hd in range(N_KV_A):
                o_ref[pl.ds(hd, y.shape[0], stride=N_KV_A), :] = y[:, hd * HEAD_DIM:(hd + 1) * HEAD_DIM]
        kbf_o[...] = kb.reshape(kbf_o.shape)
        vbf_o[...] = vb.reshape(vbf_o.shape)


def _input_projection(x, mod3, w_in_bf16, rope_tables, latent):
    n_tok = x.shape[0]
    tm = INPROJ_TILE
    row = lambda i: (i, 0)
    in_specs = [pl.BlockSpec((tm, D_MODEL), row), _mod_spec(0, latent, tm), _mod_spec(1, latent, tm),
                pl.BlockSpec((D_MODEL, IN_COLS), lambda i: (0, 0), pipeline_mode=pl.Buffered(1))]
    args = [x, mod3, mod3, w_in_bf16]
    widths = [Q_A_COLS, KV_A_COLS, KV_A_COLS, Q_B_COLS, Q_B_COLS, Q_B_COLS]
    out_specs = [pl.BlockSpec((tm, w), row) for w in widths]
    out_shape = [jax.ShapeDtypeStruct((n_tok, w), BF16) for w in widths]
    if latent:
        steps_per_request = DEC_SEQ // tm
        in_specs += [pl.BlockSpec((tm, HEAD_DIM), lambda i: (i % steps_per_request, 0))] * 2
        args += list(rope_tables)
    else:
        out_specs += [pl.BlockSpec((tm * N_KV_A, HEAD_DIM), row)] * 2
        out_shape += [jax.ShapeDtypeStruct((n_tok * N_KV_A, HEAD_DIM), F32)] * 2
        out_specs += [pl.BlockSpec((tm * N_HEADS_B, HEAD_DIM), row)] * 2
        out_shape += [jax.ShapeDtypeStruct((n_tok * N_HEADS_B, HEAD_DIM), F32)] * 2
    return pl.pallas_call(
        functools.partial(_inproj_kernel, latent=latent),
        grid=(n_tok // tm,),
        in_specs=in_specs, out_specs=out_specs, out_shape=out_shape,
        compiler_params=_params(58, 1),
        name="inproj_latent" if latent else "inproj_context",
    )(*args)


def _with_ones(v):
    return jnp.concatenate([v, jnp.ones((v.shape[0], V7X_LANES), v.dtype)], axis=1)


def _softmax_av(scores, values, sink=None):
    m = functools.reduce(jnp.maximum, [jnp.max(s, axis=-1, keepdims=True) for s in scores])
    if sink is not None:
        m = jnp.maximum(m, sink)
    acc = functools.reduce(jnp.add, [_dot(jnp.exp(s - m).astype(BF16), _with_ones(v))
                                     for s, v in zip(scores, values)])
    denom = acc[:, HEAD_DIM:]
    if sink is not None:
        denom = denom + jnp.exp(sink - m)
    return acc[:, :HEAD_DIM] / denom


def _cache_head(c_ref, head, n_heads):
    return c_ref[pl.ds(head, PAST_LEN, stride=n_heads), :].astype(BF16)


def _sink_column(sink_ref, first_head, n_heads, rows):
    return jnp.concatenate([jnp.full((rows, 1), sink_ref[first_head + g], F32) for g in range(n_heads)], axis=0)


def _ctx_attn_kernel(sink_ref, qa_ref, ka_ref, va_ref, qb_ref, kb_ref, vb_ref, oa_ref, ob_ref):
    for req in range(qa_ref.shape[0] // SEQ):
        rows = slice(req * SEQ, (req + 1) * SEQ)
        for kv in range(N_KV_A):
            cols = slice(kv * HEAD_DIM, (kv + 1) * HEAD_DIM)
            q = jnp.concatenate([qa_ref[rows, (kv * GROUP_A + g) * HEAD_DIM:(kv * GROUP_A + g + 1) * HEAD_DIM]
                                 for g in range(GROUP_A)], axis=0)
            sink = _sink_column(sink_ref, kv * GROUP_A, GROUP_A, SEQ)
            o = _softmax_av([_dot_nt(q, ka_ref[rows, cols])], [va_ref[rows, cols]], sink)
            for g in range(GROUP_A):
                hd = kv * GROUP_A + g
                oa_ref[rows, hd * HEAD_DIM:(hd + 1) * HEAD_DIM] = o[g * SEQ:(g + 1) * SEQ].astype(BF16)
        for hd in range(N_HEADS_B):
            cols = slice(hd * HEAD_DIM, (hd + 1) * HEAD_DIM)
            o = _softmax_av([_dot_nt(qb_ref[rows, cols], kb_ref[rows, cols])], [vb_ref[rows, cols]])
            ob_ref[rows, cols] = o.astype(BF16)


def _context_attention(sink, qa, ka, va, qb, kb, vb):
    row = lambda b: (b, 0)
    rows = CTX_ATTN_REQUESTS * SEQ
    widths = [Q_A_COLS, KV_A_COLS, KV_A_COLS, Q_B_COLS, Q_B_COLS, Q_B_COLS]
    return pl.pallas_call(
        _ctx_attn_kernel,
        grid=(BATCH // CTX_ATTN_REQUESTS,),
        in_specs=[pl.BlockSpec(memory_space=pltpu.SMEM)] + [pl.BlockSpec((rows, w), row) for w in widths],
        out_specs=[pl.BlockSpec((rows, Q_A_COLS), row), pl.BlockSpec((rows, Q_B_COLS), row)],
        out_shape=[jax.ShapeDtypeStruct((N_CTX_TOK, Q_A_COLS), BF16),
                   jax.ShapeDtypeStruct((N_CTX_TOK, Q_B_COLS), BF16)],
        compiler_params=_params(32, 1),
        name="context_attention",
    )(sink, qa, ka, va, qb, kb, vb)


def _window_attn_kernel(sink_ref, q_ref, k_ref, v_ref, ck_ref, cv_ref, o_ref):
    kv = pl.program_id(1)
    span = 3 * BLOCK_A
    rows = GROUP_A * BLOCK_A
    sink = jnp.concatenate([jnp.full((BLOCK_A, 1), sink_ref[kv * GROUP_A + g], F32) for g in range(GROUP_A)],
                           axis=0)
    ck = _cache_head(ck_ref, kv, N_KV_A)
    cv = _cache_head(cv_ref, kv, N_KV_A)
    band = (lax.broadcasted_iota(jnp.int32, (rows, span), 1)
            - (lax.broadcasted_iota(jnp.int32, (rows, span), 0) & (BLOCK_A - 1)))
    masks = {}
    for n in range(DEC_SEQ // BLOCK_A):
        start = min(max((n - 1) * BLOCK_A, 0), DEC_SEQ - span)
        offset = start - n * BLOCK_A
        if offset not in masks:
            masks[offset] = jnp.where(jnp.abs(band + offset) <= WINDOW_A, 0.0, NEG_INF)
        q = jnp.concatenate([q_ref[n * BLOCK_A:(n + 1) * BLOCK_A, g * HEAD_DIM:(g + 1) * HEAD_DIM]
                             for g in range(GROUP_A)], axis=0)
        s_loc = _dot_nt(q, k_ref[start:start + span, :]) + masks[offset]
        o = _softmax_av([s_loc, _dot_nt(q, ck)], [v_ref[start:start + span, :], cv], sink)
        for g in range(GROUP_A):
            o_ref[n * BLOCK_A:(n + 1) * BLOCK_A, g * HEAD_DIM:(g + 1) * HEAD_DIM] = (
                o[g * BLOCK_A:(g + 1) * BLOCK_A].astype(BF16))


def _window_attention(sink, qa, ka, va, ck, cv):
    q_spec = pl.BlockSpec((DEC_SEQ, GROUP_A * HEAD_DIM), lambda b, kv: (b, kv))
    kv_spec = pl.BlockSpec((DEC_SEQ, HEAD_DIM), lambda b, kv: (b, kv))
    c_spec = pl.BlockSpec((PAST_LEN * N_KV_A, HEAD_DIM), lambda b, kv: (b, 0))
    return pl.pallas_call(
        _window_attn_kernel,
        grid=(DEC_BATCH, N_KV_A),
        in_specs=[pl.BlockSpec(memory_space=pltpu.SMEM), q_spec, kv_spec, kv_spec, c_spec, c_spec],
        out_specs=q_spec,
        out_shape=jax.ShapeDtypeStruct((N_LAT_TOK, Q_A_COLS), BF16),
        compiler_params=_params(32, 2),
        name="window_attention",
    )(sink, qa, ka, va, ck, cv)


def _na_row_start(rq):
    return min(max(rq - NA_ROWS // 2, 0), GRID_ROWS - NA_ROWS)


def _na_window(group):
    first = _na_row_start(group * NA_GROUP_ROWS)
    last = _na_row_start(group * NA_GROUP_ROWS + NA_GROUP_ROWS - 1) + NA_ROWS - 1
    n_rows = last - first + 1
    n_rows += n_rows % 2
    assert first % 2 == 0 and n_rows <= NA_WIN_ROWS and first + n_rows <= GRID_ROWS
    return first, n_rows


def _na_build_bias(rb_ref, bias_ref, head):
    shape = (GRID_W, 2 * GRID_W)
    lane = lax.broadcasted_iota(jnp.int32, shape, 1)
    qcol = lax.broadcasted_iota(jnp.int32, shape, 0)
    kcol = lane & (GRID_W - 1)
    left = lane < GRID_W
    dcol = kcol - qcol + (NA_COLS - 1)
    wstart = jnp.clip(qcol - NA_COLS // 2, 0, GRID_W - NA_COLS)
    col_ok = (kcol >= wstart) & (kcol < wstart + NA_COLS)
    base = head * (NA_REL_ROWS * NA_REL_COLS)
    toeplitz = []
    for a in range(NA_REL_ROWS):
        acc = jnp.zeros(shape, F32)
        for bb in range(NA_REL_COLS):
            acc = jnp.where(dcol == bb, rb_ref[base + a * NA_REL_COLS + bb], acc)
        toeplitz.append(acc)
    neg = jnp.full(shape, NEG_INF, F32)
    for g in range(GRID_ROWS // NA_GROUP_ROWS):
        wrow, n_rows = _na_window(g)
        for qi in range(NA_GROUP_ROWS):
            rq = g * NA_GROUP_ROWS + qi
            row0 = _na_row_start(rq)
            for p in range(n_rows // 2):
                rk0 = wrow + 2 * p
                ok0 = row0 <= rk0 < row0 + NA_ROWS
                ok1 = row0 <= rk0 + 1 < row0 + NA_ROWS
                a0 = rk0 - rq + NA_ROWS - 1
                if ok0 and ok1:
                    blk = jnp.where(col_ok, jnp.where(left, toeplitz[a0], toeplitz[a0 + 1]), neg)
                elif ok0:
                    blk = jnp.where(col_ok & left, toeplitz[a0], neg)
                elif ok1:
                    blk = jnp.where(col_ok & jnp.logical_not(left), toeplitz[a0 + 1], neg)
                else:
                    blk = neg
                bias_ref[g, qi * GRID_W:(qi + 1) * GRID_W, p * 2 * GRID_W:(p + 1) * 2 * GRID_W] = blk


def _na_attn_kernel(rb_ref, q_ref, k_ref, v_ref, ck_ref, cv_ref, o_ref, bias_ref):
    head = pl.program_id(0)

    @pl.when(pl.program_id(1) == 0)
    def _():
        _na_build_bias(rb_ref, bias_ref, head)

    ck = _cache_head(ck_ref, head, N_HEADS_B)
    cv = _cache_head(cv_ref, head, N_HEADS_B)
    q_rows = NA_GROUP_ROWS * GRID_W
    for g in range(GRID_ROWS // NA_GROUP_ROWS):
        first, n_rows = _na_window(g)
        k0, k_rows = first * GRID_W, n_rows * GRID_W
        q = q_ref[g * q_rows:(g + 1) * q_rows, :]
        s_loc = _dot_nt(q, k_ref[k0:k0 + k_rows, :]) + bias_ref[g, :, :k_rows]
        s_ctx = _dot_nt(q, ck)
        o = _softmax_av([s_loc, s_ctx], [v_ref[k0:k0 + k_rows, :], cv])
        o_ref[g * q_rows:(g + 1) * q_rows, :] = o.astype(BF16)


def _neighborhood_attention(rel_bias_flat, qb, kb, vb, ck, cv):
    tok_spec = pl.BlockSpec((DEC_SEQ, HEAD_DIM), lambda h, b: (b, h))
    c_spec = pl.BlockSpec((PAST_LEN * N_HEADS_B, HEAD_DIM), lambda h, b: (b, 0))
    n_groups = GRID_ROWS // NA_GROUP_ROWS
    return pl.pallas_call(
        _na_attn_kernel,
        grid=(N_HEADS_B, DEC_BATCH),
        in_specs=[pl.BlockSpec(memory_space=pltpu.SMEM), tok_spec, tok_spec, tok_spec, c_spec, c_spec],
        out_specs=tok_spec,
        out_shape=jax.ShapeDtypeStruct((N_LAT_TOK, Q_B_COLS), BF16),
        scratch_shapes=[pltpu.VMEM((n_groups, NA_GROUP_ROWS * GRID_W, NA_WIN_ROWS * GRID_W), F32)],
        compiler_params=_params(32, 2),
        name="neighborhood_attention",
    )(rel_bias_flat, qb, kb, vb, ck, cv)


def _outproj_kernel(oa_ref, ob_ref, wo_ref, x_ref, gate1_ref, shift2_ref, scale2_ref, g_ref, b_ref, wr_ref,
                    x1_ref, h2_ref, aff_ref):
    lanes = aff_ref.shape[2]
    r0 = 0
    for n_rows in OUTPROJ_SUBTILES:
        rows = slice(r0, r0 + n_rows)
        attn = _dot(oa_ref[rows, :], wo_ref[:Q_A_COLS, :]) + _dot(ob_ref[rows, :], wo_ref[Q_A_COLS:, :])
        x1 = _layer_norm(DEEPNORM_ALPHA * x_ref[rows, :] + gate1_ref[...] * attn, g_ref[...], b_ref[...])
        x1_ref[rows, :] = x1
        h2 = (x1 * (1.0 + scale2_ref[...]) + shift2_ref[...]).astype(BF16)
        h2_ref[rows, :] = h2
        lt = _dot(h2, wr_ref[...]).T[:N_EXPERTS, :]
        e = jnp.exp(lt - jnp.max(lt, axis=0, keepdims=True))
        req, l0 = divmod(r0, lanes)
        aff_ref[req, :, l0:l0 + n_rows] = e / jnp.sum(e, axis=0, keepdims=True)
        r0 += n_rows


def _output_projection(oa, ob, wo_bf16, x, mod3, ln_g, ln_b, wr_pad, seq, latent):
    n_tok = x.shape[0]
    tm = OUTPROJ_TILE
    row = lambda i: (i, 0)
    const = lambda i: (0, 0)
    if seq >= tm:
        tiles_per_request = seq // tm
        aff_spec = pl.BlockSpec((1, N_EXPERTS, tm), lambda i: (i // tiles_per_request, 0, i % tiles_per_request))
    else:
        aff_spec = pl.BlockSpec((tm // seq, N_EXPERTS, seq), lambda i: (i, 0, 0))
    return pl.pallas_call(
        _outproj_kernel,
        grid=(n_tok // tm,),
        in_specs=[pl.BlockSpec((tm, Q_A_COLS), row), pl.BlockSpec((tm, Q_B_COLS), row),
                  pl.BlockSpec((Q_A_COLS + Q_B_COLS, D_MODEL), const, pipeline_mode=pl.Buffered(1)),
                  pl.BlockSpec((tm, D_MODEL), row),
                  _mod_spec(2, latent, tm), _mod_spec(3, latent, tm), _mod_spec(4, latent, tm),
                  pl.BlockSpec((1, D_MODEL), const), pl.BlockSpec((1, D_MODEL), const),
                  pl.BlockSpec((D_MODEL, V7X_LANES), const)],
        out_specs=[pl.BlockSpec((tm, D_MODEL), row), pl.BlockSpec((tm, D_MODEL), row), aff_spec],
        out_shape=[jax.ShapeDtypeStruct((n_tok, D_MODEL), F32), jax.ShapeDtypeStruct((n_tok, D_MODEL), BF16),
                   jax.ShapeDtypeStruct((n_tok // seq, N_EXPERTS, seq), F32)],
        compiler_params=_params(56, 1),
        name="outproj_latent" if latent else "outproj_context",
    )(oa, ob, wo_bf16, x, mod3, mod3, mod3, ln_g, ln_b, wr_pad)


def _pad_rows_to_lanes(a):
    return jnp.concatenate([a, jnp.zeros((V7X_LANES - a.shape[0], a.shape[1]), a.dtype)], axis=0)


def _kth_largest(a, k):
    n = a.shape[1]
    lane = lax.broadcasted_iota(jnp.int32, (1, n), 1)
    x = a
    size = 2
    while size <= n:
        descending = (lane & size) == 0
        j = size // 2
        while j >= 1:
            lower = (lane & j) == 0
            partner = jnp.where(lower, pltpu.roll(x, n - j, 1), pltpu.roll(x, j, 1))
            x = jnp.where(lower == descending, jnp.maximum(x, partner), jnp.minimum(x, partner))
            j //= 2
        size *= 2
    return x[:, k - 1:k]


def _route_kernel(aff_ref, slot_ref, *, cap):
    n_req, n_exp, seq = aff_ref.shape
    key = aff_ref[...].reshape(n_req * n_exp, seq)
    tau = _kth_largest(key, cap)
    above = key > tau
    tied = key == tau
    need = cap - jnp.sum(jnp.where(above, 1.0, 0.0), axis=-1, keepdims=True)
    before = jnp.where(lax.broadcasted_iota(jnp.int32, (seq, seq), 0) < lax.broadcasted_iota(jnp.int32, (seq, seq), 1),
                       1.0, 0.0).astype(BF16)
    tied_before = _dot(jnp.where(tied, 1.0, 0.0).astype(BF16), before)
    chosen = above | (tied & (tied_before < need))
    slot = _dot(jnp.where(chosen, 1.0, 0.0).astype(BF16), before)
    slot_ref[...] = jnp.where(chosen, slot, float(cap)).reshape(n_req, n_exp, seq)


def _route(aff, cap, requests_per_step):
    n_req, _, seq = aff.shape
    spec = pl.BlockSpec((requests_per_step, N_EXPERTS, seq), lambda r: (r, 0, 0))
    return pl.pallas_call(
        functools.partial(_route_kernel, cap=cap),
        grid=(n_req // requests_per_step,),
        in_specs=[spec], out_specs=spec,
        out_shape=jax.ShapeDtypeStruct(aff.shape, F32),
        compiler_params=_params(32, 1),
        name=f"route_s{seq}",
    )(aff)


def _dispatch_kernel(slot_ref, aff_ref, h2_ref, xs_ref, gate_ref, slot_t_ref, *, cap, experts_per_dot):
    n_req, _, seq = slot_ref.shape
    slot = lax.broadcasted_iota(jnp.int32, (cap, seq), 0).astype(F32)
    for r in range(n_req):
        slot_of = slot_ref[r]
        aff = aff_ref[r]
        rows = slice(r * seq, (r + 1) * seq)
        slot_t_ref[rows, :] = _pad_rows_to_lanes(slot_of).T
        h2 = h2_ref[rows, :]
        for c in range(N_EXPERTS // experts_per_dot):
            chunk = range(c * experts_per_dot, (c + 1) * experts_per_dot)
            onehot = jnp.concatenate([jnp.where(slot_of[e:e + 1, :] == slot, 1.0, 0.0) for e in chunk], axis=0)
            xs = _dot(onehot.astype(BF16), h2)
            for i, e in enumerate(chunk):
                xs_ref[r, e] = xs[i * cap:(i + 1) * cap].astype(BF16)
                gate = jnp.sum(onehot[i * cap:(i + 1) * cap] * aff[e:e + 1, :], axis=1, keepdims=True)
                gate_ref[r, e] = jnp.broadcast_to(gate, (cap, V7X_LANES))


def _dispatch(slot, aff, h2, cap, experts_per_dot, requests_per_step):
    n_req, _, seq = aff.shape
    g = requests_per_step
    req_spec = pl.BlockSpec((g, N_EXPERTS, seq), lambda r: (r, 0, 0))
    return pl.pallas_call(
        functools.partial(_dispatch_kernel, cap=cap, experts_per_dot=experts_per_dot),
        grid=(n_req // g,),
        in_specs=[req_spec, req_spec, pl.BlockSpec((g * seq, D_MODEL), lambda r: (r, 0))],
        out_specs=[pl.BlockSpec((g, N_EXPERTS, cap, D_MODEL), lambda r: (r, 0, 0, 0)),
                   pl.BlockSpec((g, N_EXPERTS, cap, V7X_LANES), lambda r: (r, 0, 0, 0)),
                   pl.BlockSpec((g * seq, V7X_LANES), lambda r: (r, 0))],
        out_shape=[jax.ShapeDtypeStruct((n_req, N_EXPERTS, cap, D_MODEL), BF16),
                   jax.ShapeDtypeStruct((n_req, N_EXPERTS, cap, V7X_LANES), F32),
                   jax.ShapeDtypeStruct((n_req * seq, V7X_LANES), F32)],
        compiler_params=_params(48, 1),
        name=f"dispatch_s{seq}",
    )(slot, aff, h2)


def _expert_rows(ref):
    n_req, cap, width = ref.shape
    return ref[...].reshape(n_req * cap, width)


def _ffn_up_kernel(xc_ref, xl_ref, wg_ref, wu_ref, hc_ref, hl_ref):
    wg = wg_ref[...].astype(BF16)
    wu = wu_ref[...].astype(BF16)
    for x_ref, h_ref in ((xc_ref, hc_ref), (xl_ref, hl_ref)):
        x = _expert_rows(x_ref)
        h_ref[...] = (jax.nn.silu(_dot(x, wg)) * _dot(x, wu)).astype(BF16)


def _ffn_down_kernel(hc_ref, hl_ref, wd_ref, gc_ref, gl_ref, yc_ref, yl_ref):
    wd = wd_ref[...].astype(BF16)
    for h_ref, g_ref, y_ref in ((hc_ref, gc_ref, yc_ref), (hl_ref, gl_ref, yl_ref)):
        gate = jnp.concatenate([_expert_rows(g_ref)] * (wd.shape[1] // V7X_LANES), axis=1)
        y_ref[...] = (_dot(h_ref[...], wd) * gate).astype(BF16).reshape(y_ref.shape)


def _expert_ffn(xs_ctx, xs_lat, gate_ctx, gate_lat, w_gate, w_up, w_down):
    def per_expert(a, width):
        n_req, _, cap, _ = a.shape
        return pl.BlockSpec((n_req, None, cap, width), lambda e, j: (0, e, 0, 0 if width == a.shape[3] else j))

    rows_c = xs_ctx.shape[0] * xs_ctx.shape[2]
    rows_l = xs_lat.shape[0] * xs_lat.shape[2]
    tn = FFN_COL_TILE
    hid_c = pl.BlockSpec((None, rows_c, tn), lambda e, j: (e, 0, j))
    hid_l = pl.BlockSpec((None, rows_l, tn), lambda e, j: (e, 0, j))
    hidden_c, hidden_l = pl.pallas_call(
        _ffn_up_kernel,
        grid=(N_EXPERTS, D_FF // tn),
        in_specs=[per_expert(xs_ctx, D_MODEL), per_expert(xs_lat, D_MODEL),
                  pl.BlockSpec((None, D_MODEL, tn), lambda e, j: (e, 0, j)),
                  pl.BlockSpec((None, D_MODEL, tn), lambda e, j: (e, 0, j))],
        out_specs=[hid_c, hid_l],
        out_shape=[jax.ShapeDtypeStruct((N_EXPERTS, rows_c, D_FF), BF16),
                   jax.ShapeDtypeStruct((N_EXPERTS, rows_l, D_FF), BF16)],
        compiler_params=_params(56, 2),
        name="ffn_up",
    )(xs_ctx, xs_lat, w_gate, w_up)
    tn = FFN_DOWN_COL_TILE
    return pl.pallas_call(
        _ffn_down_kernel,
        grid=(N_EXPERTS, D_MODEL // tn),
        in_specs=[pl.BlockSpec((None, rows_c, D_FF), lambda e, j: (e, 0, 0)),
                  pl.BlockSpec((None, rows_l, D_FF), lambda e, j: (e, 0, 0)),
                  pl.BlockSpec((None, D_FF, tn), lambda e, j: (e, 0, j)),
                  per_expert(gate_ctx, V7X_LANES), per_expert(gate_lat, V7X_LANES)],
        out_specs=[per_expert(xs_ctx, tn), per_expert(xs_lat, tn)],
        out_shape=[jax.ShapeDtypeStruct(xs_ctx.shape, BF16), jax.ShapeDtypeStruct(xs_lat.shape, BF16)],
        compiler_params=_params(56, 2),
        name="ffn_down",
    )(hidden_c, hidden_l, w_down, gate_ctx, gate_lat)


def _combine_kernel(slot_t_ref, y_ref, x1_ref, gate2_ref, g_ref, b_ref, o_ref, *, cap, experts_per_dot):
    n_req = y_ref.shape[0]
    ts = slot_t_ref.shape[0] // n_req
    cols = experts_per_dot * cap
    shift = cap.bit_length() - 1
    erow = lax.broadcasted_iota(jnp.int32, (V7X_LANES, cols), 0)
    ecol = lax.shift_right_logical(lax.broadcasted_iota(jnp.int32, (V7X_LANES, cols), 1), shift)
    slot = (lax.broadcasted_iota(jnp.int32, (ts, cols), 1) & (cap - 1)).astype(F32)
    for r in range(n_req):
        rows = slice(r * ts, (r + 1) * ts)
        rk = slot_t_ref[rows, :].astype(BF16)
        ffn = None
        for c in range(N_EXPERTS // experts_per_dot):
            expand = jnp.where(erow == ecol + c * experts_per_dot, 1.0, 0.0).astype(BF16)
            onehot_t = jnp.where(_dot(rk, expand) == slot, 1.0, 0.0).astype(BF16)
            y = y_ref[r, c * experts_per_dot:(c + 1) * experts_per_dot].reshape(cols, D_MODEL)
            part = _dot(onehot_t, y)
            ffn = part if ffn is None else ffn + part
        o_ref[rows, :] = _layer_norm(DEEPNORM_ALPHA * x1_ref[rows, :] + gate2_ref[...] * ffn,
                                     g_ref[...], b_ref[...])


def _combine(slot_t, y, x1, mod3, ln_g, ln_b, n_req, seq, cap, experts_per_dot, requests_per_step, latent):
    ts = TOKEN_TILE
    steps = seq // ts
    g = requests_per_step
    assert g == 1 or (steps == 1 and not latent)
    row = lambda r, t: (r * steps + t, 0)
    const = lambda r, t: (0, 0)
    mod_row = (lambda r, t: ((1 + r) * N_MOD + 5, 0, 0)) if latent else (lambda r, t: (5, 0, 0))
    return pl.pallas_call(
        functools.partial(_combine_kernel, cap=cap, experts_per_dot=experts_per_dot),
        grid=(n_req // g, steps),
        in_specs=[pl.BlockSpec((g * ts, V7X_LANES), row),
                  pl.BlockSpec((g, N_EXPERTS, cap, D_MODEL), lambda r, t: (r, 0, 0, 0)),
                  pl.BlockSpec((g * ts, D_MODEL), row),
                  pl.BlockSpec((None, 1, D_MODEL), mod_row),
                  pl.BlockSpec((1, D_MODEL), const), pl.BlockSpec((1, D_MODEL), const)],
        out_specs=pl.BlockSpec((g * ts, D_MODEL), row),
        out_shape=jax.ShapeDtypeStruct((n_req * seq, D_MODEL), F32),
        compiler_params=_params(48, 2),
        name=f"combine_s{seq}",
    )(slot_t, y, x1, mod3, ln_g, ln_b)


def _rope_tables():
    t = jnp.arange(DEC_SEQ)
    row = (t // GRID_W).astype(F32)
    col = (t % GRID_W).astype(F32)
    n_freq = HEAD_DIM // 4
    inv_freq = ROPE_THETA ** (-jnp.arange(n_freq, dtype=F32) / n_freq)
    ang = jnp.concatenate([row[:, None] * inv_freq, col[:, None] * inv_freq], axis=-1)
    cos = jnp.repeat(jnp.cos(ang), 2, axis=-1)
    sin = jnp.stack([-jnp.sin(ang), jnp.sin(ang)], axis=-1).reshape(DEC_SEQ, HEAD_DIM)
    return cos, sin


def kernel(x_prompt, x_sample, cache_k_a, cache_v_a, cache_k_b, cache_v_b, c, c_ctx, w_mod, b_mod, w_in, w_o,
           sink_a, rel_bias_b, w_router, w_gate, w_up, w_down, ln1_g, ln1_b, ln2_g, ln2_b):
    assert DEPTH == 1 and w_in.shape == (DEPTH, D_MODEL, IN_COLS)
    x_ctx = x_prompt.reshape(N_CTX_TOK, D_MODEL)
    x_lat = x_sample.reshape(N_LAT_TOK, D_MODEL)

    cond = jnp.concatenate([c_ctx[None, :], c, jnp.zeros((V7X_SUBLANES - N_COND, D_MODEL), F32)], axis=0)
    mod = _modulation(cond, w_mod[0], b_mod)
    mod3 = mod.reshape(V7X_SUBLANES * N_MOD, 1, D_MODEL)

    w_in_bf16 = w_in[0].astype(BF16)
    wo_bf16 = w_o[0].astype(BF16)
    wr_pad = jnp.pad(w_router[0], ((0, 0), (0, V7X_LANES - N_EXPERTS))).astype(BF16)
    sink = sink_a[0]
    ln1 = (ln1_g, ln1_b)
    ln2 = (ln2_g, ln2_b)

    qa, ka, va, qb, kb, vb, ka_f, va_f, kb_f, vb_f = _input_projection(x_ctx, mod3, w_in_bf16, None, False)
    oa, ob = _context_attention(sink, qa, ka, va, qb, kb, vb)
    x1_ctx, h2_ctx, aff_ctx = _output_projection(oa, ob, wo_bf16, x_ctx, mod3, *ln1, wr_pad, SEQ, False)
    slot_ctx = _route(aff_ctx, CAP_CTX, ROUTE_REQUESTS_CTX)
    xs_ctx, gate_ctx, slot_t_ctx = _dispatch(slot_ctx, aff_ctx, h2_ctx, CAP_CTX, N_EXPERTS, CTX_DISPATCH_REQUESTS)

    qa, ka, va, qb, kb, vb = _input_projection(x_lat, mod3, w_in_bf16, _rope_tables(), True)
    oa = _window_attention(sink, qa, ka, va,
                           cache_k_a.reshape(-1, HEAD_DIM), cache_v_a.reshape(-1, HEAD_DIM))
    ob = _neighborhood_attention(rel_bias_b.reshape(-1), qb, kb, vb,
                                 cache_k_b.reshape(-1, HEAD_DIM), cache_v_b.reshape(-1, HEAD_DIM))
    x1_lat, h2_lat, aff_lat = _output_projection(oa, ob, wo_bf16, x_lat, mod3, *ln1, wr_pad, DEC_SEQ, True)
    slot_lat = _route(aff_lat, CAP_LAT, DEC_BATCH)
    xs_lat, gate_lat, slot_t_lat = _dispatch(slot_lat, aff_lat, h2_lat, CAP_LAT, 4, 1)

    y_ctx, y_lat = _expert_ffn(xs_ctx, xs_lat, gate_ctx, gate_lat, w_gate[0], w_up[0], w_down[0])

    y_p = _combine(slot_t_ctx, y_ctx, x1_ctx, mod3, *ln2, BATCH, SEQ, CAP_CTX, N_EXPERTS, CTX_COMBINE_REQUESTS, False)
    y_s = _combine(slot_t_lat, y_lat, x1_lat, mod3, *ln2, DEC_BATCH, DEC_SEQ, CAP_LAT, 4, 1, True)

    kv_a_shape = (BATCH, DEPTH, SEQ, N_KV_A, HEAD_DIM)
    kv_b_shape = (BATCH, DEPTH, SEQ, N_HEADS_B, HEAD_DIM)
    return (y_p.reshape(BATCH, SEQ, D_MODEL), y_s.reshape(DEC_BATCH, DEC_SEQ, D_MODEL),
            ka_f.reshape(kv_a_shape), va_f.reshape(kv_a_shape), kb_f.reshape(kv_b_shape), vb_f.reshape(kv_b_shape))
```

```python
import functools

import jax
import jax.numpy as jnp
from jax import lax
from jax.experimental import pallas as pl
from jax.experimental.pallas import tpu as pltpu

D_MODEL = 2048
BATCH = 32
SEQ = 256
DEC_BATCH = 4
DEC_SEQ = 1024
PAST_LEN = 256
GRID_W = 64
HEAD_DIM = 128
N_HEADS_A = 8
N_KV_A = 2
GROUP_A = N_HEADS_A // N_KV_A
N_HEADS_B = 8
WINDOW_A = 128
BLOCK_A = 128
NA_ROWS = 8
NA_COLS = 16
N_EXPERTS = 16
EC_CAPACITY = 2
D_FF = D_MODEL
ROPE_THETA = 10000.0
LN_EPS = 1e-5
NEG_INF = -1e30
DEPTH = 1
Q_A_COLS = N_HEADS_A * HEAD_DIM
KV_A_COLS = N_KV_A * HEAD_DIM
Q_B_COLS = N_HEADS_B * HEAD_DIM
IN_COLS = Q_A_COLS + 2 * KV_A_COLS + 3 * Q_B_COLS
COL_STARTS = (0, Q_A_COLS, Q_A_COLS + KV_A_COLS, Q_A_COLS + 2 * KV_A_COLS,
              Q_A_COLS + 2 * KV_A_COLS + Q_B_COLS, Q_A_COLS + 2 * KV_A_COLS + 2 * Q_B_COLS, IN_COLS)
DEEPNORM_ALPHA = (2.0 * DEPTH) ** 0.25
ATTN_SCALE = HEAD_DIM ** -0.5

N_CTX_TOK = BATCH * SEQ
N_LAT_TOK = DEC_BATCH * DEC_SEQ
GRID_ROWS = DEC_SEQ // GRID_W
CAP_CTX = EC_CAPACITY * SEQ // N_EXPERTS
CAP_LAT = EC_CAPACITY * DEC_SEQ // N_EXPERTS
N_COND = 1 + DEC_BATCH
N_MOD = 6

V7X_LANES = 128
V7X_SUBLANES = 8
V7X_VMEM_BYTES = 64 * 1024 * 1024
MIB = 1024 * 1024

F32 = jnp.float32
BF16 = jnp.bfloat16
NT_DIMS = (((1,), (1,)), ((), ()))

TOKEN_TILE = 256
MOD_COL_TILE = 1024
FFN_COL_TILE = 512
FFN_DOWN_COL_TILE = 1024
INPROJ_TILE = 512
OUTPROJ_TILE = 512
OUTPROJ_SUBTILES = (256, 256)
CTX_ATTN_REQUESTS = 1
CTX_DISPATCH_REQUESTS = 4
COMBINE_TILE = 512
ROUTE_REQUESTS_CTX = 8
NA_GROUP_ROWS = 4
NA_WIN_ROWS = 12
NA_REL_ROWS = 2 * NA_ROWS - 1
NA_REL_COLS = 2 * NA_COLS - 1


def _params(vmem_mib, n_axes):
    return pltpu.CompilerParams(dimension_semantics=("arbitrary",) * n_axes,
                                vmem_limit_bytes=vmem_mib * MIB)


def _hbm_array(shape, dtype):
    return pltpu.HBM(tuple(shape), dtype)


def _in_hbm(*arrays):
    return [pltpu.with_memory_space_constraint(a, pltpu.HBM) for a in arrays]


def _dot(a, b):
    return jnp.dot(a, b, preferred_element_type=F32)


def _dot_nt(a, b):
    return lax.dot_general(a, b, NT_DIMS, preferred_element_type=F32)


def _layer_norm(z, g, b):
    mu = jnp.mean(z, axis=-1, keepdims=True)
    zc = z - mu
    var = jnp.mean(zc * zc, axis=-1, keepdims=True)
    return zc * lax.rsqrt(var + LN_EPS) * g + b


def _mod_kernel(cond_ref, w_ref, b_ref, o_ref):
    s = jax.nn.silu(cond_ref[...]).astype(BF16)
    o_ref[...] = _dot(s, w_ref[...].astype(BF16)) + b_ref[...]


def _modulation(cond, w_mod, b_mod):
    n_out = N_MOD * D_MODEL
    return pl.pallas_call(
        _mod_kernel,
        grid=(n_out // MOD_COL_TILE,),
        in_specs=[pl.BlockSpec((V7X_SUBLANES, D_MODEL), lambda j: (0, 0)),
                  pl.BlockSpec((D_MODEL, MOD_COL_TILE), lambda j: (0, j)),
                  pl.BlockSpec((1, MOD_COL_TILE), lambda j: (0, j))],
        out_specs=pl.BlockSpec((V7X_SUBLANES, MOD_COL_TILE), lambda j: (0, j)),
        out_shape=_hbm_array((V7X_SUBLANES, n_out), F32),
        compiler_params=_params(40, 1),
        name="modulation",
    )(cond, w_mod, b_mod)


def _mod_spec(which, latent, tile=TOKEN_TILE):
    steps_per_request = DEC_SEQ // tile
    if latent:
        return pl.BlockSpec((None, 1, D_MODEL), lambda i: ((1 + i // steps_per_request) * N_MOD + which, 0, 0))
    return pl.BlockSpec((None, 1, D_MODEL), lambda i: (which, 0, 0))


def _rope(x, cos, sin, even):
    swapped = jnp.where(even, pltpu.roll(x, HEAD_DIM - 1, 1), pltpu.roll(x, 1, 1))
    return x * cos + swapped * sin


def _inproj_kernel(*refs, latent):
    if latent:
        (x_ref, shift_ref, scale_ref, w_ref, cos_ref, sin_ref, wo_ref,
         qa_o, ka_o, va_o, qb_o, kb_o, vb_o, wo_o) = refs
        wo_o[...] = wo_ref[...].astype(BF16)
    else:
        (x_ref, shift_ref, scale_ref, w_ref,
         qa_o, ka_o, va_o, qb_o, kb_o, vb_o, kaf_o, vaf_o, kbf_o, vbf_o) = refs
    h = (x_ref[...] * (1.0 + scale_ref[...]) + shift_ref[...]).astype(BF16)

    def proj(piece):
        return _dot(h, w_ref[:, COL_STARTS[piece]:COL_STARTS[piece + 1]])

    if latent:
        cos = cos_ref[...]
        sin = sin_ref[...]
        even = (lax.broadcasted_iota(jnp.int32, cos.shape, 1) & 1) == 0

    def heads(y, o_ref, n_heads, scale, rope):
        for hd in range(n_heads):
            cols = slice(hd * HEAD_DIM, (hd + 1) * HEAD_DIM)
            yh = y[:, cols]
            if rope:
                yh = _rope(yh, cos, sin, even)
            if scale != 1.0:
                yh = yh * scale
            o_ref[:, cols] = yh.astype(BF16)

    qa = proj(0)
    heads(qa, qa_o, N_HEADS_A, ATTN_SCALE, latent)
    ka = proj(1)
    heads(ka, ka_o, N_KV_A, 1.0, latent)
    va = proj(2)
    va_o[...] = va.astype(BF16)
    qb = proj(3)
    qb_o[...] = (qb * ATTN_SCALE).astype(BF16)
    kb = proj(4)
    kb_o[...] = kb.astype(BF16)
    vb = proj(5)
    vb_o[...] = vb.astype(BF16)
    if not latent:
        for o_ref, y in ((kaf_o, ka), (vaf_o, va)):
            for hd in range(N_KV_A):
                o_ref[pl.ds(hd, y.shape[0], stride=N_KV_A), :] = y[:, hd * HEAD_DIM:(hd + 1) * HEAD_DIM]
        kbf_o[...] = kb.reshape(kbf_o.shape)
        vbf_o[...] = vb.reshape(vbf_o.shape)


def _input_projection(x, mod3, w_in_bf16, rope_tables, w_o, latent):
    n_tok = x.shape[0]
    tm = INPROJ_TILE
    row = lambda i: (i, 0)
    in_specs = [pl.BlockSpec((tm, D_MODEL), row), _mod_spec(0, latent, tm), _mod_spec(1, latent, tm),
                pl.BlockSpec((D_MODEL, IN_COLS), lambda i: (0, 0), pipeline_mode=pl.Buffered(1))]
    args = [x, mod3, mod3, w_in_bf16]
    widths = [Q_A_COLS, KV_A_COLS, KV_A_COLS, Q_B_COLS, Q_B_COLS, Q_B_COLS]
    out_specs = [pl.BlockSpec((tm, w), row) for w in widths]
    out_shape = [_hbm_array((n_tok, w), BF16) for w in widths]
    if latent:
        steps_per_request = DEC_SEQ // tm
        in_specs += [pl.BlockSpec((tm, HEAD_DIM), lambda i: (i % steps_per_request, 0))] * 2
        args += list(rope_tables)
        wo_rows = w_o.shape[0] // (n_tok // tm)
        in_specs += [pl.BlockSpec((wo_rows, D_MODEL), row)]
        args += [w_o]
        out_specs += [pl.BlockSpec((wo_rows, D_MODEL), row)]
        out_shape += [_hbm_array(w_o.shape, BF16)]
    else:
        out_specs += [pl.BlockSpec((tm * N_KV_A, HEAD_DIM), row)] * 2
        out_shape += [_hbm_array((n_tok * N_KV_A, HEAD_DIM), F32)] * 2
        out_specs += [pl.BlockSpec((tm * N_HEADS_B, HEAD_DIM), row)] * 2
        out_shape += [_hbm_array((n_tok * N_HEADS_B, HEAD_DIM), F32)] * 2
    return pl.pallas_call(
        functools.partial(_inproj_kernel, latent=latent),
        grid=(n_tok // tm,),
        in_specs=in_specs, out_specs=out_specs, out_shape=out_shape,
        compiler_params=_params(58, 1),
        name="inproj_latent" if latent else "inproj_context",
    )(*args)


def _with_ones(v):
    return jnp.concatenate([v, jnp.ones((v.shape[0], V7X_LANES), v.dtype)], axis=1)


def _softmax_av(scores, values, sink=None):
    m = functools.reduce(jnp.maximum, [jnp.max(s, axis=-1, keepdims=True) for s in scores])
    if sink is not None:
        m = jnp.maximum(m, sink)
    acc = functools.reduce(jnp.add, [_dot(jnp.exp(s - m).astype(BF16), _with_ones(v))
                                     for s, v in zip(scores, values)])
    denom = acc[:, HEAD_DIM:]
    if sink is not None:
        denom = denom + jnp.exp(sink - m)
    return acc[:, :HEAD_DIM] / denom


def _cache_head(c_ref, head, n_heads):
    return c_ref[pl.ds(head, PAST_LEN, stride=n_heads), :].astype(BF16)


def _sink_column(sink_ref, first_head, n_heads, rows):
    return jnp.concatenate([jnp.full((rows, 1), sink_ref[first_head + g], F32) for g in range(n_heads)], axis=0)


def _ctx_attn_kernel(sink_ref, qa_ref, ka_ref, va_ref, qb_ref, kb_ref, vb_ref, oa_ref, ob_ref):
    for req in range(qa_ref.shape[0] // SEQ):
        rows = slice(req * SEQ, (req + 1) * SEQ)
        for kv in range(N_KV_A):
            cols = slice(kv * HEAD_DIM, (kv + 1) * HEAD_DIM)
            q = jnp.concatenate([qa_ref[rows, (kv * GROUP_A + g) * HEAD_DIM:(kv * GROUP_A + g + 1) * HEAD_DIM]
                                 for g in range(GROUP_A)], axis=0)
            sink = _sink_column(sink_ref, kv * GROUP_A, GROUP_A, SEQ)
            o = _softmax_av([_dot_nt(q, ka_ref[rows, cols])], [va_ref[rows, cols]], sink)
            for g in range(GROUP_A):
                hd = kv * GROUP_A + g
                oa_ref[rows, hd * HEAD_DIM:(hd + 1) * HEAD_DIM] = o[g * SEQ:(g + 1) * SEQ].astype(BF16)
        for hd in range(N_HEADS_B):
            cols = slice(hd * HEAD_DIM, (hd + 1) * HEAD_DIM)
            o = _softmax_av([_dot_nt(qb_ref[rows, cols], kb_ref[rows, cols])], [vb_ref[rows, cols]])
            ob_ref[rows, cols] = o.astype(BF16)


def _context_attention(sink, qa, ka, va, qb, kb, vb):
    row = lambda b: (b, 0)
    rows = CTX_ATTN_REQUESTS * SEQ
    widths = [Q_A_COLS, KV_A_COLS, KV_A_COLS, Q_B_COLS, Q_B_COLS, Q_B_COLS]
    return pl.pallas_call(
        _ctx_attn_kernel,
        grid=(BATCH // CTX_ATTN_REQUESTS,),
        in_specs=[pl.BlockSpec(memory_space=pltpu.SMEM)] + [pl.BlockSpec((rows, w), row) for w in widths],
        out_specs=[pl.BlockSpec((rows, Q_A_COLS), row), pl.BlockSpec((rows, Q_B_COLS), row)],
        out_shape=[_hbm_array((N_CTX_TOK, Q_A_COLS), BF16),
                   _hbm_array((N_CTX_TOK, Q_B_COLS), BF16)],
        compiler_params=_params(32, 1),
        name="context_attention",
    )(sink, *_in_hbm(qa, ka, va, qb, kb, vb))


def _window_attn_kernel(sink_ref, q_ref, k_ref, v_ref, ck_ref, cv_ref, o_ref):
    kv = pl.program_id(1)
    span = 3 * BLOCK_A
    rows = GROUP_A * BLOCK_A
    sink = jnp.concatenate([jnp.full((BLOCK_A, 1), sink_ref[kv * GROUP_A + g], F32) for g in range(GROUP_A)],
                           axis=0)
    ck = _cache_head(ck_ref, kv, N_KV_A)
    cv = _cache_head(cv_ref, kv, N_KV_A)
    band = (lax.broadcasted_iota(jnp.int32, (rows, span), 1)
            - (lax.broadcasted_iota(jnp.int32, (rows, span), 0) & (BLOCK_A - 1)))
    masks = {}
    for n in range(DEC_SEQ // BLOCK_A):
        start = min(max((n - 1) * BLOCK_A, 0), DEC_SEQ - span)
        offset = start - n * BLOCK_A
        if offset not in masks:
            masks[offset] = jnp.where(jnp.abs(band + offset) <= WINDOW_A, 0.0, NEG_INF)
        q = jnp.concatenate([q_ref[n * BLOCK_A:(n + 1) * BLOCK_A, g * HEAD_DIM:(g + 1) * HEAD_DIM]
                             for g in range(GROUP_A)], axis=0)
        s_loc = _dot_nt(q, k_ref[start:start + span, :]) + masks[offset]
        o = _softmax_av([s_loc, _dot_nt(q, ck)], [v_ref[start:start + span, :], cv], sink)
        for g in range(GROUP_A):
            o_ref[n * BLOCK_A:(n + 1) * BLOCK_A, g * HEAD_DIM:(g + 1) * HEAD_DIM] = (
                o[g * BLOCK_A:(g + 1) * BLOCK_A].astype(BF16))


def _window_attention(sink, qa, ka, va, ck, cv):
    q_spec = pl.BlockSpec((DEC_SEQ, GROUP_A * HEAD_DIM), lambda b, kv: (b, kv))
    kv_spec = pl.BlockSpec((DEC_SEQ, HEAD_DIM), lambda b, kv: (b, kv))
    c_spec = pl.BlockSpec((PAST_LEN * N_KV_A, HEAD_DIM), lambda b, kv: (b, 0))
    return pl.pallas_call(
        _window_attn_kernel,
        grid=(DEC_BATCH, N_KV_A),
        in_specs=[pl.BlockSpec(memory_space=pltpu.SMEM), q_spec, kv_spec, kv_spec, c_spec, c_spec],
        out_specs=q_spec,
        out_shape=_hbm_array((N_LAT_TOK, Q_A_COLS), BF16),
        compiler_params=_params(32, 2),
        name="window_attention",
    )(sink, *_in_hbm(qa, ka, va, ck, cv))


def _na_row_start(rq):
    return min(max(rq - NA_ROWS // 2, 0), GRID_ROWS - NA_ROWS)


def _na_window(group):
    first = _na_row_start(group * NA_GROUP_ROWS)
    last = _na_row_start(group * NA_GROUP_ROWS + NA_GROUP_ROWS - 1) + NA_ROWS - 1
    n_rows = last - first + 1
    n_rows += n_rows % 2
    assert first % 2 == 0 and n_rows <= NA_WIN_ROWS and first + n_rows <= GRID_ROWS
    return first, n_rows


def _na_build_bias(rb_ref, bias_ref, head):
    shape = (GRID_W, 2 * GRID_W)
    lane = lax.broadcasted_iota(jnp.int32, shape, 1)
    qcol = lax.broadcasted_iota(jnp.int32, shape, 0)
    kcol = lane & (GRID_W - 1)
    left = lane < GRID_W
    dcol = kcol - qcol + (NA_COLS - 1)
    wstart = jnp.clip(qcol - NA_COLS // 2, 0, GRID_W - NA_COLS)
    col_ok = (kcol >= wstart) & (kcol < wstart + NA_COLS)
    base = head * (NA_REL_ROWS * NA_REL_COLS)
    toeplitz = []
    for a in range(NA_REL_ROWS):
        acc = jnp.zeros(shape, F32)
        for bb in range(NA_REL_COLS):
            acc = jnp.where(dcol == bb, rb_ref[base + a * NA_REL_COLS + bb], acc)
        toeplitz.append(acc)
    neg = jnp.full(shape, NEG_INF, F32)
    for g in range(GRID_ROWS // NA_GROUP_ROWS):
        wrow, n_rows = _na_window(g)
        for qi in range(NA_GROUP_ROWS):
            rq = g * NA_GROUP_ROWS + qi
            row0 = _na_row_start(rq)
            for p in range(n_rows // 2):
                rk0 = wrow + 2 * p
                ok0 = row0 <= rk0 < row0 + NA_ROWS
                ok1 = row0 <= rk0 + 1 < row0 + NA_ROWS
                a0 = rk0 - rq + NA_ROWS - 1
                if ok0 and ok1:
                    blk = jnp.where(col_ok, jnp.where(left, toeplitz[a0], toeplitz[a0 + 1]), neg)
                elif ok0:
                    blk = jnp.where(col_ok & left, toeplitz[a0], neg)
                elif ok1:
                    blk = jnp.where(col_ok & jnp.logical_not(left), toeplitz[a0 + 1], neg)
                else:
                    blk = neg
                bias_ref[g, qi * GRID_W:(qi + 1) * GRID_W, p * 2 * GRID_W:(p + 1) * 2 * GRID_W] = blk


def _na_attn_kernel(rb_ref, q_ref, k_ref, v_ref, ck_ref, cv_ref, o_ref, bias_ref):
    head = pl.program_id(0)

    @pl.when(pl.program_id(1) == 0)
    def _():
        _na_build_bias(rb_ref, bias_ref, head)

    ck = _cache_head(ck_ref, head, N_HEADS_B)
    cv = _cache_head(cv_ref, head, N_HEADS_B)
    q_rows = NA_GROUP_ROWS * GRID_W
    for g in range(GRID_ROWS // NA_GROUP_ROWS):
        first, n_rows = _na_window(g)
        k0, k_rows = first * GRID_W, n_rows * GRID_W
        q = q_ref[g * q_rows:(g + 1) * q_rows, :]
        s_loc = _dot_nt(q, k_ref[k0:k0 + k_rows, :]) + bias_ref[g, :, :k_rows]
        s_ctx = _dot_nt(q, ck)
        o = _softmax_av([s_loc, s_ctx], [v_ref[k0:k0 + k_rows, :], cv])
        o_ref[g * q_rows:(g + 1) * q_rows, :] = o.astype(BF16)


def _neighborhood_attention(rel_bias_flat, qb, kb, vb, ck, cv):
    tok_spec = pl.BlockSpec((DEC_SEQ, HEAD_DIM), lambda h, b: (b, h))
    c_spec = pl.BlockSpec((PAST_LEN * N_HEADS_B, HEAD_DIM), lambda h, b: (b, 0))
    n_groups = GRID_ROWS // NA_GROUP_ROWS
    return pl.pallas_call(
        _na_attn_kernel,
        grid=(N_HEADS_B, DEC_BATCH),
        in_specs=[pl.BlockSpec(memory_space=pltpu.SMEM), tok_spec, tok_spec, tok_spec, c_spec, c_spec],
        out_specs=tok_spec,
        out_shape=_hbm_array((N_LAT_TOK, Q_B_COLS), BF16),
        scratch_shapes=[pltpu.VMEM((n_groups, NA_GROUP_ROWS * GRID_W, NA_WIN_ROWS * GRID_W), F32)],
        compiler_params=_params(32, 2),
        name="neighborhood_attention",
    )(rel_bias_flat, *_in_hbm(qb, kb, vb, ck, cv))


def _outproj_kernel(oa_ref, ob_ref, wo_ref, x_ref, gate1_ref, shift2_ref, scale2_ref, g_ref, b_ref, wr_ref,
                    x1_ref, h2_ref, aff_ref):
    lanes = aff_ref.shape[2]
    r0 = 0
    for n_rows in OUTPROJ_SUBTILES:
        rows = slice(r0, r0 + n_rows)
        attn = _dot(oa_ref[rows, :], wo_ref[:Q_A_COLS, :]) + _dot(ob_ref[rows, :], wo_ref[Q_A_COLS:, :])
        x1 = _layer_norm(DEEPNORM_ALPHA * x_ref[rows, :] + gate1_ref[...] * attn, g_ref[...], b_ref[...])
        x1_ref[rows, :] = x1
        h2 = (x1 * (1.0 + scale2_ref[...]) + shift2_ref[...]).astype(BF16)
        h2_ref[rows, :] = h2
        lt = _dot(h2, wr_ref[...]).T[:N_EXPERTS, :]
        e = jnp.exp(lt - jnp.max(lt, axis=0, keepdims=True))
        req, l0 = divmod(r0, lanes)
        aff_ref[req, :, l0:l0 + n_rows] = e / jnp.sum(e, axis=0, keepdims=True)
        r0 += n_rows


def _output_projection(oa, ob, wo_bf16, x, mod3, ln_g, ln_b, wr_pad, seq, latent):
    n_tok = x.shape[0]
    tm = OUTPROJ_TILE
    row = lambda i: (i, 0)
    const = lambda i: (0, 0)
    if seq >= tm:
        tiles_per_request = seq // tm
        aff_spec = pl.BlockSpec((1, N_EXPERTS, tm), lambda i: (i // tiles_per_request, 0, i % tiles_per_request))
    else:
        aff_spec = pl.BlockSpec((tm // seq, N_EXPERTS, seq), lambda i: (i, 0, 0))
    return pl.pallas_call(
        _outproj_kernel,
        grid=(n_tok // tm,),
        in_specs=[pl.BlockSpec((tm, Q_A_COLS), row), pl.BlockSpec((tm, Q_B_COLS), row),
                  pl.BlockSpec((Q_A_COLS + Q_B_COLS, D_MODEL), const, pipeline_mode=pl.Buffered(1)),
                  pl.BlockSpec((tm, D_MODEL), row),
                  _mod_spec(2, latent, tm), _mod_spec(3, latent, tm), _mod_spec(4, latent, tm),
                  pl.BlockSpec((1, D_MODEL), const), pl.BlockSpec((1, D_MODEL), const),
                  pl.BlockSpec((D_MODEL, V7X_LANES), const)],
        out_specs=[pl.BlockSpec((tm, D_MODEL), row), pl.BlockSpec((tm, D_MODEL), row), aff_spec],
        out_shape=[_hbm_array((n_tok, D_MODEL), F32), _hbm_array((n_tok, D_MODEL), BF16),
                   _hbm_array((n_tok // seq, N_EXPERTS, seq), F32)],
        compiler_params=_params(56, 1),
        name="outproj_latent" if latent else "outproj_context",
    )(*_in_hbm(oa, ob, wo_bf16, x), mod3, mod3, mod3, ln_g, ln_b, wr_pad)


def _pad_rows_to_lanes(a):
    return jnp.concatenate([a, jnp.zeros((V7X_LANES - a.shape[0], a.shape[1]), a.dtype)], axis=0)


def _kth_largest(a, k):
    n = a.shape[1]
    lane = lax.broadcasted_iota(jnp.int32, (1, n), 1)
    x = a
    size = 2
    while size <= n:
        descending = (lane & size) == 0
        j = size // 2
        while j >= 1:
            lower = (lane & j) == 0
            partner = jnp.where(lower, pltpu.roll(x, n - j, 1), pltpu.roll(x, j, 1))
            x = jnp.where(lower == descending, jnp.maximum(x, partner), jnp.minimum(x, partner))
            j //= 2
        size *= 2
    return x[:, k - 1:k]


def _route_kernel(aff_ref, slot_ref, *, cap):
    n_req, n_exp, seq = aff_ref.shape
    key = aff_ref[...].reshape(n_req * n_exp, seq)
    tau = _kth_largest(key, cap)
    above = key > tau
    tied = key == tau
    need = cap - jnp.sum(jnp.where(above, 1.0, 0.0), axis=-1, keepdims=True)
    before = jnp.where(lax.broadcasted_iota(jnp.int32, (seq, seq), 0) < lax.broadcasted_iota(jnp.int32, (seq, seq), 1),
                       1.0, 0.0).astype(BF16)
    tied_before = _dot(jnp.where(tied, 1.0, 0.0).astype(BF16), before)
    chosen = above | (tied & (tied_before < need))
    slot = _dot(jnp.where(chosen, 1.0, 0.0).astype(BF16), before)
    slot_ref[...] = jnp.where(chosen, slot, float(cap)).reshape(n_req, n_exp, seq)


def _route(aff, cap, requests_per_step):
    n_req, _, seq = aff.shape
    spec = pl.BlockSpec((requests_per_step, N_EXPERTS, seq), lambda r: (r, 0, 0))
    return pl.pallas_call(
        functools.partial(_route_kernel, cap=cap),
        grid=(n_req // requests_per_step,),
        in_specs=[spec], out_specs=spec,
        out_shape=_hbm_array(aff.shape, F32),
        compiler_params=_params(32, 1),
        name=f"route_s{seq}",
    )(*_in_hbm(aff))


def _dispatch_kernel(slot_ref, aff_ref, h2_ref, xs_ref, gate_ref, slot_t_ref, *, cap, experts_per_dot):
    n_req, _, seq = slot_ref.shape
    slot = lax.broadcasted_iota(jnp.int32, (cap, seq), 0).astype(F32)
    for r in range(n_req):
        slot_of = slot_ref[r]
        aff = aff_ref[r]
        rows = slice(r * seq, (r + 1) * seq)
        slot_t_ref[rows, :] = _pad_rows_to_lanes(slot_of).T
        h2 = h2_ref[rows, :]
        for c in range(N_EXPERTS // experts_per_dot):
            chunk = range(c * experts_per_dot, (c + 1) * experts_per_dot)
            onehot = jnp.concatenate([jnp.where(slot_of[e:e + 1, :] == slot, 1.0, 0.0) for e in chunk], axis=0)
            xs = _dot(onehot.astype(BF16), h2)
            for i, e in enumerate(chunk):
                xs_ref[r, e] = xs[i * cap:(i + 1) * cap].astype(BF16)
                gate = jnp.sum(onehot[i * cap:(i + 1) * cap] * aff[e:e + 1, :], axis=1, keepdims=True)
                gate_ref[r, e] = jnp.broadcast_to(gate, (cap, V7X_LANES))


def _dispatch(slot, aff, h2, cap, experts_per_dot, requests_per_step):
    n_req, _, seq = aff.shape
    g = requests_per_step
    req_spec = pl.BlockSpec((g, N_EXPERTS, seq), lambda r: (r, 0, 0))
    return pl.pallas_call(
        functools.partial(_dispatch_kernel, cap=cap, experts_per_dot=experts_per_dot),
        grid=(n_req // g,),
        in_specs=[req_spec, req_spec, pl.BlockSpec((g * seq, D_MODEL), lambda r: (r, 0))],
        out_specs=[pl.BlockSpec((g, N_EXPERTS, cap, D_MODEL), lambda r: (r, 0, 0, 0)),
                   pl.BlockSpec((g, N_EXPERTS, cap, V7X_LANES), lambda r: (r, 0, 0, 0)),
                   pl.BlockSpec((g * seq, V7X_LANES), lambda r: (r, 0))],
        out_shape=[_hbm_array((n_req, N_EXPERTS, cap, D_MODEL), BF16),
                   _hbm_array((n_req, N_EXPERTS, cap, V7X_LANES), F32),
                   _hbm_array((n_req * seq, V7X_LANES), F32)],
        compiler_params=_params(48, 1),
        name=f"dispatch_s{seq}",
    )(*_in_hbm(slot, aff, h2))


def _expert_rows(ref):
    n_req, cap, width = ref.shape
    return ref[...].reshape(n_req * cap, width)


def _ffn_up_kernel(xc_ref, xl_ref, wg_ref, wu_ref, hc_ref, hl_ref):
    wg = wg_ref[...].astype(BF16)
    wu = wu_ref[...].astype(BF16)
    for x_ref, h_ref in ((xc_ref, hc_ref), (xl_ref, hl_ref)):
        x = _expert_rows(x_ref)
        h_ref[...] = (jax.nn.silu(_dot(x, wg)) * _dot(x, wu)).astype(BF16)


def _ffn_down_kernel(hc_ref, hl_ref, wd_ref, gc_ref, gl_ref, yc_ref, yl_ref):
    wd = wd_ref[...].astype(BF16)
    for h_ref, g_ref, y_ref in ((hc_ref, gc_ref, yc_ref), (hl_ref, gl_ref, yl_ref)):
        gate = jnp.concatenate([_expert_rows(g_ref)] * (wd.shape[1] // V7X_LANES), axis=1)
        y_ref[...] = (_dot(h_ref[...], wd) * gate).astype(BF16).reshape(y_ref.shape)


def _expert_ffn(xs_ctx, xs_lat, gate_ctx, gate_lat, w_gate, w_up, w_down):
    def per_expert(a, width):
        n_req, _, cap, _ = a.shape
        return pl.BlockSpec((n_req, None, cap, width), lambda e, j: (0, e, 0, 0 if width == a.shape[3] else j))

    rows_c = xs_ctx.shape[0] * xs_ctx.shape[2]
    rows_l = xs_lat.shape[0] * xs_lat.shape[2]
    tn = FFN_COL_TILE
    hid_c = pl.BlockSpec((None, rows_c, tn), lambda e, j: (e, 0, j))
    hid_l = pl.BlockSpec((None, rows_l, tn), lambda e, j: (e, 0, j))
    hidden_c, hidden_l = pl.pallas_call(
        _ffn_up_kernel,
        grid=(N_EXPERTS, D_FF // tn),
        in_specs=[per_expert(xs_ctx, D_MODEL), per_expert(xs_lat, D_MODEL),
                  pl.BlockSpec((None, D_MODEL, tn), lambda e, j: (e, 0, j)),
                  pl.BlockSpec((None, D_MODEL, tn), lambda e, j: (e, 0, j))],
        out_specs=[hid_c, hid_l],
        out_shape=[_hbm_array((N_EXPERTS, rows_c, D_FF), BF16),
                   _hbm_array((N_EXPERTS, rows_l, D_FF), BF16)],
        compiler_params=_params(56, 2),
        name="ffn_up",
    )(*_in_hbm(xs_ctx, xs_lat), w_gate, w_up)
    tn = FFN_DOWN_COL_TILE
    return pl.pallas_call(
        _ffn_down_kernel,
        grid=(N_EXPERTS, D_MODEL // tn),
        in_specs=[pl.BlockSpec((None, rows_c, D_FF), lambda e, j: (e, 0, 0)),
                  pl.BlockSpec((None, rows_l, D_FF), lambda e, j: (e, 0, 0)),
                  pl.BlockSpec((None, D_FF, tn), lambda e, j: (e, 0, j)),
                  per_expert(gate_ctx, V7X_LANES), per_expert(gate_lat, V7X_LANES)],
        out_specs=[per_expert(xs_ctx, tn), per_expert(xs_lat, tn)],
        out_shape=[_hbm_array(xs_ctx.shape, BF16), _hbm_array(xs_lat.shape, BF16)],
        compiler_params=_params(56, 2),
        name="ffn_down",
    )(*_in_hbm(hidden_c, hidden_l), w_down, *_in_hbm(gate_ctx, gate_lat))


def _combine_kernel(slot_t_ref, y_ref, x1_ref, gate2_ref, g_ref, b_ref, o_ref, *, cap, experts_per_dot):
    ts = TOKEN_TILE
    n_sub = slot_t_ref.shape[0] // ts
    cols = experts_per_dot * cap
    shift = cap.bit_length() - 1
    erow = lax.broadcasted_iota(jnp.int32, (V7X_LANES, cols), 0)
    ecol = lax.shift_right_logical(lax.broadcasted_iota(jnp.int32, (V7X_LANES, cols), 1), shift)
    slot = (lax.broadcasted_iota(jnp.int32, (ts, cols), 1) & (cap - 1)).astype(F32)
    for r in range(n_sub):
        rows = slice(r * ts, (r + 1) * ts)
        req = r if y_ref.shape[0] > 1 else 0
        rk = slot_t_ref[rows, :].astype(BF16)
        ffn = None
        for c in range(N_EXPERTS // experts_per_dot):
            expand = jnp.where(erow == ecol + c * experts_per_dot, 1.0, 0.0).astype(BF16)
            onehot_t = jnp.where(_dot(rk, expand) == slot, 1.0, 0.0).astype(BF16)
            y = y_ref[req, c * experts_per_dot:(c + 1) * experts_per_dot].reshape(cols, D_MODEL)
            part = _dot(onehot_t, y)
            ffn = part if ffn is None else ffn + part
        o_ref[rows, :] = _layer_norm(DEEPNORM_ALPHA * x1_ref[rows, :] + gate2_ref[...] * ffn,
                                     g_ref[...], b_ref[...])


def _combine(slot_t, y, x1, mod3, ln_g, ln_b, n_req, seq, cap, experts_per_dot, latent):
    ts = COMBINE_TILE
    steps = max(seq // ts, 1)
    g = max(ts // seq, 1)
    assert g == 1 or not latent
    ts //= g
    row = lambda r, t: (r * steps + t, 0)
    const = lambda r, t: (0, 0)
    mod_row = (lambda r, t: ((1 + r) * N_MOD + 5, 0, 0)) if latent else (lambda r, t: (5, 0, 0))
    return pl.pallas_call(
        functools.partial(_combine_kernel, cap=cap, experts_per_dot=experts_per_dot),
        grid=(n_req // g, steps),
        in_specs=[pl.BlockSpec((g * ts, V7X_LANES), row),
                  pl.BlockSpec((g, N_EXPERTS, cap, D_MODEL), lambda r, t: (r, 0, 0, 0)),
                  pl.BlockSpec((g * ts, D_MODEL), row),
                  pl.BlockSpec((None, 1, D_MODEL), mod_row),
                  pl.BlockSpec((1, D_MODEL), const), pl.BlockSpec((1, D_MODEL), const)],
        out_specs=pl.BlockSpec((g * ts, D_MODEL), row),
        out_shape=_hbm_array((n_req * seq, D_MODEL), F32),
        compiler_params=_params(48, 2),
        name=f"combine_s{seq}",
    )(*_in_hbm(slot_t, y, x1), mod3, ln_g, ln_b)


def _rope_tables():
    t = jnp.arange(DEC_SEQ)
    row = (t // GRID_W).astype(F32)
    col = (t % GRID_W).astype(F32)
    n_freq = HEAD_DIM // 4
    inv_freq = ROPE_THETA ** (-jnp.arange(n_freq, dtype=F32) / n_freq)
    ang = jnp.concatenate([row[:, None] * inv_freq, col[:, None] * inv_freq], axis=-1)
    cos = jnp.repeat(jnp.cos(ang), 2, axis=-1)
    sin = jnp.stack([-jnp.sin(ang), jnp.sin(ang)], axis=-1).reshape(DEC_SEQ, HEAD_DIM)
    return cos, sin


def kernel(x_prompt, x_sample, cache_k_a, cache_v_a, cache_k_b, cache_v_b, c, c_ctx, w_mod, b_mod, w_in, w_o,
           sink_a, rel_bias_b, w_router, w_gate, w_up, w_down, ln1_g, ln1_b, ln2_g, ln2_b):
    assert DEPTH == 1 and w_in.shape == (DEPTH, D_MODEL, IN_COLS)
    x_ctx = x_prompt.reshape(N_CTX_TOK, D_MODEL)
    x_lat = x_sample.reshape(N_LAT_TOK, D_MODEL)

    cond = jnp.concatenate([c_ctx[None, :], c, jnp.zeros((V7X_SUBLANES - N_COND, D_MODEL), F32)], axis=0)
    mod = _modulation(cond, w_mod[0], b_mod)
    mod3 = mod.reshape(V7X_SUBLANES * N_MOD, 1, D_MODEL)

    w_in_bf16 = w_in[0].astype(BF16)
    wr_pad = jnp.pad(w_router[0], ((0, 0), (0, V7X_LANES - N_EXPERTS))).astype(BF16)
    sink = sink_a[0]
    ln1 = (ln1_g, ln1_b)
    ln2 = (ln2_g, ln2_b)

    qa, ka, va, qb, kb, vb, wo_bf16 = _input_projection(x_lat, mod3, w_in_bf16, _rope_tables(), w_o[0], True)
    oa = _window_attention(sink, qa, ka, va,
                           cache_k_a.reshape(-1, HEAD_DIM), cache_v_a.reshape(-1, HEAD_DIM))
    ob = _neighborhood_attention(rel_bias_b.reshape(-1), qb, kb, vb,
                                 cache_k_b.reshape(-1, HEAD_DIM), cache_v_b.reshape(-1, HEAD_DIM))
    x1_lat, h2_lat, aff_lat = _output_projection(oa, ob, wo_bf16, x_lat, mod3, *ln1, wr_pad, DEC_SEQ, True)
    slot_lat = _route(aff_lat, CAP_LAT, DEC_BATCH)
    xs_lat, gate_lat, slot_t_lat = _dispatch(slot_lat, aff_lat, h2_lat, CAP_LAT, 4, 1)

    qa, ka, va, qb, kb, vb, ka_f, va_f, kb_f, vb_f = _input_projection(x_ctx, mod3, w_in_bf16, None, None, False)
    oa, ob = _context_attention(sink, qa, ka, va, qb, kb, vb)
    x1_ctx, h2_ctx, aff_ctx = _output_projection(oa, ob, wo_bf16, x_ctx, mod3, *ln1, wr_pad, SEQ, False)
    slot_ctx = _route(aff_ctx, CAP_CTX, ROUTE_REQUESTS_CTX)
    xs_ctx, gate_ctx, slot_t_ctx = _dispatch(slot_ctx, aff_ctx, h2_ctx, CAP_CTX, N_EXPERTS, CTX_DISPATCH_REQUESTS)

    y_ctx, y_lat = _expert_ffn(xs_ctx, xs_lat, gate_ctx, gate_lat, w_gate[0], w_up[0], w_down[0])

    y_p = _combine(slot_t_ctx, y_ctx, x1_ctx, mod3, *ln2, BATCH, SEQ, CAP_CTX, N_EXPERTS, False)
    y_s = _combine(slot_t_lat, y_lat, x1_lat, mod3, *ln2, DEC_BATCH, DEC_SEQ, CAP_LAT, 4, True)

    kv_a_shape = (BATCH, DEPTH, SEQ, N_KV_A, HEAD_DIM)
    kv_b_shape = (BATCH, DEPTH, SEQ, N_HEADS_B, HEAD_DIM)
    return (y_p.reshape(BATCH, SEQ, D_MODEL), y_s.reshape(DEC_BATCH, DEC_SEQ, D_MODEL),
            ka_f.reshape(kv_a_shape), va_f.reshape(kv_a_shape), kb_f.reshape(kv_b_shape), vb_f.reshape(kv_b_shape))
```

```python
import functools

import jax
import jax.numpy as jnp
from jax import lax
from jax.experimental import pallas as pl
from jax.experimental.pallas import tpu as pltpu

D_MODEL = 2048
BATCH = 32
SEQ = 256
DEC_BATCH = 4
DEC_SEQ = 1024
PAST_LEN = 256
GRID_W = 64
HEAD_DIM = 128
N_HEADS_A = 8
N_KV_A = 2
GROUP_A = N_HEADS_A // N_KV_A
N_HEADS_B = 8
WINDOW_A = 128
BLOCK_A = 128
NA_ROWS = 8
NA_COLS = 16
N_EXPERTS = 16
EC_CAPACITY = 2
D_FF = D_MODEL
ROPE_THETA = 10000.0
LN_EPS = 1e-5
NEG_INF = -1e30
DEPTH = 1
Q_A_COLS = N_HEADS_A * HEAD_DIM
KV_A_COLS = N_KV_A * HEAD_DIM
Q_B_COLS = N_HEADS_B * HEAD_DIM
IN_COLS = Q_A_COLS + 2 * KV_A_COLS + 3 * Q_B_COLS
COL_STARTS = (0, Q_A_COLS, Q_A_COLS + KV_A_COLS, Q_A_COLS + 2 * KV_A_COLS,
              Q_A_COLS + 2 * KV_A_COLS + Q_B_COLS, Q_A_COLS + 2 * KV_A_COLS + 2 * Q_B_COLS, IN_COLS)
DEEPNORM_ALPHA = (2.0 * DEPTH) ** 0.25
ATTN_SCALE = HEAD_DIM ** -0.5

N_CTX_TOK = BATCH * SEQ
N_LAT_TOK = DEC_BATCH * DEC_SEQ
GRID_ROWS = DEC_SEQ // GRID_W
CAP_CTX = EC_CAPACITY * SEQ // N_EXPERTS
CAP_LAT = EC_CAPACITY * DEC_SEQ // N_EXPERTS
N_COND = 1 + DEC_BATCH
N_MOD = 6

V7X_LANES = 128
V7X_SUBLANES = 8
V7X_VMEM_BYTES = 64 * 1024 * 1024
MIB = 1024 * 1024

F32 = jnp.float32
BF16 = jnp.bfloat16
NT_DIMS = (((1,), (1,)), ((), ()))

TOKEN_TILE = 256
MOD_COL_TILE = 1024
FFN_COL_TILE = 512
FFN_DOWN_COL_TILE = 1024
INPROJ_TILE = 512
OUTPROJ_TILE = 512
OUTPROJ_SUBTILES = (256, 256)
CTX_ATTN_REQUESTS = 1
CTX_DISPATCH_REQUESTS = 4
COMBINE_TILE = 512
NA_GROUP_ROWS = 4
NA_WIN_ROWS = 12
NA_REL_ROWS = 2 * NA_ROWS - 1
NA_REL_COLS = 2 * NA_COLS - 1


def _params(vmem_mib, n_axes):
    assert vmem_mib * MIB < V7X_VMEM_BYTES
    return pltpu.CompilerParams(dimension_semantics=("arbitrary",) * n_axes,
                                vmem_limit_bytes=vmem_mib * MIB)


def _hbm_array(shape, dtype):
    return pltpu.HBM(tuple(shape), dtype)


def _in_hbm(*arrays):
    return [pltpu.with_memory_space_constraint(a, pltpu.HBM) for a in arrays]


def _dot(a, b):
    return jnp.dot(a, b, preferred_element_type=F32)


def _dot_nt(a, b):
    return lax.dot_general(a, b, NT_DIMS, preferred_element_type=F32)


def _layer_norm(z, g, b):
    mu = jnp.mean(z, axis=-1, keepdims=True)
    zc = z - mu
    var = jnp.mean(zc * zc, axis=-1, keepdims=True)
    return zc * lax.rsqrt(var + LN_EPS) * g + b


def _mod_kernel(cond_ref, w_ref, b_ref, o_ref):
    s = jax.nn.silu(cond_ref[...]).astype(BF16)
    o_ref[...] = _dot(s, w_ref[...].astype(BF16)) + b_ref[...]


def _modulation(cond, w_mod, b_mod):
    n_out = N_MOD * D_MODEL
    return pl.pallas_call(
        _mod_kernel,
        grid=(n_out // MOD_COL_TILE,),
        in_specs=[pl.BlockSpec((V7X_SUBLANES, D_MODEL), lambda j: (0, 0)),
                  pl.BlockSpec((D_MODEL, MOD_COL_TILE), lambda j: (0, j)),
                  pl.BlockSpec((1, MOD_COL_TILE), lambda j: (0, j))],
        out_specs=pl.BlockSpec((V7X_SUBLANES, MOD_COL_TILE), lambda j: (0, j)),
        out_shape=_hbm_array((V7X_SUBLANES, n_out), F32),
        compiler_params=_params(40, 1),
        name="modulation",
    )(cond, w_mod, b_mod)


def _mod_spec(which, latent, tile):
    steps_per_request = DEC_SEQ // tile
    if latent:
        return pl.BlockSpec((None, 1, D_MODEL), lambda i: ((1 + i // steps_per_request) * N_MOD + which, 0, 0))
    return pl.BlockSpec((None, 1, D_MODEL), lambda i: (which, 0, 0))


def _rope(x, cos, sin, even):
    swapped = jnp.where(even, pltpu.roll(x, HEAD_DIM - 1, 1), pltpu.roll(x, 1, 1))
    return x * cos + swapped * sin


def _inproj_kernel(*refs, latent):
    if latent:
        (x_ref, shift_ref, scale_ref, w_ref, cos_ref, sin_ref, wo_ref,
         qa_o, ka_o, va_o, qb_o, kb_o, vb_o, wo_o) = refs
        wo_o[...] = wo_ref[...].astype(BF16)
    else:
        (x_ref, shift_ref, scale_ref, w_ref,
         qa_o, ka_o, va_o, qb_o, kb_o, vb_o, kaf_o, vaf_o, kbf_o, vbf_o) = refs
    h = (x_ref[...] * (1.0 + scale_ref[...]) + shift_ref[...]).astype(BF16)

    def proj(piece):
        return _dot(h, w_ref[:, COL_STARTS[piece]:COL_STARTS[piece + 1]])

    if latent:
        cos = cos_ref[...]
        sin = sin_ref[...]
        even = (lax.broadcasted_iota(jnp.int32, cos.shape, 1) & 1) == 0

    def heads(y, o_ref, n_heads, scale, rope):
        for hd in range(n_heads):
            cols = slice(hd * HEAD_DIM, (hd + 1) * HEAD_DIM)
            yh = y[:, cols]
            if rope:
                yh = _rope(yh, cos, sin, even)
            if scale != 1.0:
                yh = yh * scale
            o_ref[:, cols] = yh.astype(BF16)

    qa = proj(0)
    heads(qa, qa_o, N_HEADS_A, ATTN_SCALE, latent)
    ka = proj(1)
    heads(ka, ka_o, N_KV_A, 1.0, latent)
    va = proj(2)
    va_o[...] = va.astype(BF16)
    qb = proj(3)
    qb_o[...] = (qb * ATTN_SCALE).astype(BF16)
    kb = proj(4)
    kb_o[...] = kb.astype(BF16)
    vb = proj(5)
    vb_o[...] = vb.astype(BF16)
    if not latent:
        for o_ref, y in ((kaf_o, ka), (vaf_o, va)):
            for hd in range(N_KV_A):
                o_ref[pl.ds(hd, y.shape[0], stride=N_KV_A), :] = y[:, hd * HEAD_DIM:(hd + 1) * HEAD_DIM]
        kbf_o[...] = kb.reshape(kbf_o.shape)
        vbf_o[...] = vb.reshape(vbf_o.shape)


def _input_projection(x, mod3, w_in_bf16, rope_tables, w_o, latent):
    n_tok = x.shape[0]
    tm = INPROJ_TILE
    row = lambda i: (i, 0)
    in_specs = [pl.BlockSpec((tm, D_MODEL), row), _mod_spec(0, latent, tm), _mod_spec(1, latent, tm),
                pl.BlockSpec((D_MODEL, IN_COLS), lambda i: (0, 0), pipeline_mode=pl.Buffered(1))]
    args = [x, mod3, mod3, w_in_bf16]
    widths = [Q_A_COLS, KV_A_COLS, KV_A_COLS, Q_B_COLS, Q_B_COLS, Q_B_COLS]
    out_specs = [pl.BlockSpec((tm, w), row) for w in widths]
    out_shape = [_hbm_array((n_tok, w), BF16) for w in widths]
    if latent:
        steps_per_request = DEC_SEQ // tm
        in_specs += [pl.BlockSpec((tm, HEAD_DIM), lambda i: (i % steps_per_request, 0))] * 2
        args += list(rope_tables)
        wo_rows = w_o.shape[0] // (n_tok // tm)
        in_specs += [pl.BlockSpec((wo_rows, D_MODEL), row)]
        args += [w_o]
        out_specs += [pl.BlockSpec((wo_rows, D_MODEL), row)]
        out_shape += [_hbm_array(w_o.shape, BF16)]
    else:
        out_specs += [pl.BlockSpec((tm * N_KV_A, HEAD_DIM), row)] * 2
        out_shape += [_hbm_array((n_tok * N_KV_A, HEAD_DIM), F32)] * 2
        out_specs += [pl.BlockSpec((tm * N_HEADS_B, HEAD_DIM), row)] * 2
        out_shape += [_hbm_array((n_tok * N_HEADS_B, HEAD_DIM), F32)] * 2
    return pl.pallas_call(
        functools.partial(_inproj_kernel, latent=latent),
        grid=(n_tok // tm,),
        in_specs=in_specs, out_specs=out_specs, out_shape=out_shape,
        compiler_params=_params(58, 1),
        name="inproj_latent" if latent else "inproj_context",
    )(*args)


def _with_ones(v):
    return jnp.concatenate([v, jnp.ones((v.shape[0], V7X_LANES), v.dtype)], axis=1)


def _softmax_av(scores, values, sink=None):
    m = functools.reduce(jnp.maximum, [jnp.max(s, axis=-1, keepdims=True) for s in scores])
    if sink is not None:
        m = jnp.maximum(m, sink)
    acc = functools.reduce(jnp.add, [_dot(jnp.exp(s - m).astype(BF16), _with_ones(v))
                                     for s, v in zip(scores, values)])
    denom = acc[:, HEAD_DIM:]
    if sink is not None:
        denom = denom + jnp.exp(sink - m)
    return acc[:, :HEAD_DIM] / denom


def _cache_head(c_ref, head, n_heads):
    return c_ref[pl.ds(head, PAST_LEN, stride=n_heads), :].astype(BF16)


def _sink_column(sink_ref, first_head, n_heads, rows):
    return jnp.concatenate([jnp.full((rows, 1), sink_ref[first_head + g], F32) for g in range(n_heads)], axis=0)


def _ctx_attn_kernel(sink_ref, qa_ref, ka_ref, va_ref, qb_ref, kb_ref, vb_ref, oa_ref, ob_ref):
    for req in range(qa_ref.shape[0] // SEQ):
        rows = slice(req * SEQ, (req + 1) * SEQ)
        for kv in range(N_KV_A):
            cols = slice(kv * HEAD_DIM, (kv + 1) * HEAD_DIM)
            q = jnp.concatenate([qa_ref[rows, (kv * GROUP_A + g) * HEAD_DIM:(kv * GROUP_A + g + 1) * HEAD_DIM]
                                 for g in range(GROUP_A)], axis=0)
            sink = _sink_column(sink_ref, kv * GROUP_A, GROUP_A, SEQ)
            o = _softmax_av([_dot_nt(q, ka_ref[rows, cols])], [va_ref[rows, cols]], sink)
            for g in range(GROUP_A):
                hd = kv * GROUP_A + g
                oa_ref[rows, hd * HEAD_DIM:(hd + 1) * HEAD_DIM] = o[g * SEQ:(g + 1) * SEQ].astype(BF16)
        for hd in range(N_HEADS_B):
            cols = slice(hd * HEAD_DIM, (hd + 1) * HEAD_DIM)
            o = _softmax_av([_dot_nt(qb_ref[rows, cols], kb_ref[rows, cols])], [vb_ref[rows, cols]])
            ob_ref[rows, cols] = o.astype(BF16)


def _context_attention(sink, qa, ka, va, qb, kb, vb):
    row = lambda b: (b, 0)
    rows = CTX_ATTN_REQUESTS * SEQ
    widths = [Q_A_COLS, KV_A_COLS, KV_A_COLS, Q_B_COLS, Q_B_COLS, Q_B_COLS]
    return pl.pallas_call(
        _ctx_attn_kernel,
        grid=(BATCH // CTX_ATTN_REQUESTS,),
        in_specs=[pl.BlockSpec(memory_space=pltpu.SMEM)] + [pl.BlockSpec((rows, w), row) for w in widths],
        out_specs=[pl.BlockSpec((rows, Q_A_COLS), row), pl.BlockSpec((rows, Q_B_COLS), row)],
        out_shape=[_hbm_array((N_CTX_TOK, Q_A_COLS), BF16),
                   _hbm_array((N_CTX_TOK, Q_B_COLS), BF16)],
        compiler_params=_params(32, 1),
        name="context_attention",
    )(sink, *_in_hbm(qa, ka, va, qb, kb, vb))


def _window_attn_kernel(sink_ref, q_ref, k_ref, v_ref, ck_ref, cv_ref, o_ref):
    kv = pl.program_id(1)
    span = 3 * BLOCK_A
    rows = GROUP_A * BLOCK_A
    sink = jnp.concatenate([jnp.full((BLOCK_A, 1), sink_ref[kv * GROUP_A + g], F32) for g in range(GROUP_A)],
                           axis=0)
    ck = _cache_head(ck_ref, kv, N_KV_A)
    cv = _cache_head(cv_ref, kv, N_KV_A)
    band = (lax.broadcasted_iota(jnp.int32, (rows, span), 1)
            - (lax.broadcasted_iota(jnp.int32, (rows, span), 0) & (BLOCK_A - 1)))
    masks = {}
    for n in range(DEC_SEQ // BLOCK_A):
        start = min(max((n - 1) * BLOCK_A, 0), DEC_SEQ - span)
        offset = start - n * BLOCK_A
        if offset not in masks:
            masks[offset] = jnp.where(jnp.abs(band + offset) <= WINDOW_A, 0.0, NEG_INF)
        q = jnp.concatenate([q_ref[n * BLOCK_A:(n + 1) * BLOCK_A, g * HEAD_DIM:(g + 1) * HEAD_DIM]
                             for g in range(GROUP_A)], axis=0)
        s_loc = _dot_nt(q, k_ref[start:start + span, :]) + masks[offset]
        o = _softmax_av([s_loc, _dot_nt(q, ck)], [v_ref[start:start + span, :], cv], sink)
        for g in range(GROUP_A):
            o_ref[n * BLOCK_A:(n + 1) * BLOCK_A, g * HEAD_DIM:(g + 1) * HEAD_DIM] = (
                o[g * BLOCK_A:(g + 1) * BLOCK_A].astype(BF16))


def _window_attention(sink, qa, ka, va, ck, cv):
    q_spec = pl.BlockSpec((DEC_SEQ, GROUP_A * HEAD_DIM), lambda b, kv: (b, kv))
    kv_spec = pl.BlockSpec((DEC_SEQ, HEAD_DIM), lambda b, kv: (b, kv))
    c_spec = pl.BlockSpec((PAST_LEN * N_KV_A, HEAD_DIM), lambda b, kv: (b, 0))
    return pl.pallas_call(
        _window_attn_kernel,
        grid=(DEC_BATCH, N_KV_A),
        in_specs=[pl.BlockSpec(memory_space=pltpu.SMEM), q_spec, kv_spec, kv_spec, c_spec, c_spec],
        out_specs=q_spec,
        out_shape=_hbm_array((N_LAT_TOK, Q_A_COLS), BF16),
        compiler_params=_params(32, 2),
        name="window_attention",
    )(sink, *_in_hbm(qa, ka, va, ck, cv))


def _na_row_start(rq):
    return min(max(rq - NA_ROWS // 2, 0), GRID_ROWS - NA_ROWS)


def _na_window(group):
    first = _na_row_start(group * NA_GROUP_ROWS)
    last = _na_row_start(group * NA_GROUP_ROWS + NA_GROUP_ROWS - 1) + NA_ROWS - 1
    n_rows = last - first + 1
    n_rows += n_rows % 2
    assert first % 2 == 0 and n_rows <= NA_WIN_ROWS and first + n_rows <= GRID_ROWS
    return first, n_rows


def _na_build_bias(rb_ref, bias_ref, head):
    shape = (GRID_W, 2 * GRID_W)
    lane = lax.broadcasted_iota(jnp.int32, shape, 1)
    qcol = lax.broadcasted_iota(jnp.int32, shape, 0)
    kcol = lane & (GRID_W - 1)
    left = lane < GRID_W
    dcol = kcol - qcol + (NA_COLS - 1)
    wstart = jnp.clip(qcol - NA_COLS // 2, 0, GRID_W - NA_COLS)
    col_ok = (kcol >= wstart) & (kcol < wstart + NA_COLS)
    base = head * (NA_REL_ROWS * NA_REL_COLS)
    toeplitz = []
    for a in range(NA_REL_ROWS):
        acc = jnp.zeros(shape, F32)
        for bb in range(NA_REL_COLS):
            acc = jnp.where(dcol == bb, rb_ref[base + a * NA_REL_COLS + bb], acc)
        toeplitz.append(acc)
    neg = jnp.full(shape, NEG_INF, F32)
    for g in range(GRID_ROWS // NA_GROUP_ROWS):
        wrow, n_rows = _na_window(g)
        for qi in range(NA_GROUP_ROWS):
            rq = g * NA_GROUP_ROWS + qi
            row0 = _na_row_start(rq)
            for p in range(n_rows // 2):
                rk0 = wrow + 2 * p
                ok0 = row0 <= rk0 < row0 + NA_ROWS
                ok1 = row0 <= rk0 + 1 < row0 + NA_ROWS
                a0 = rk0 - rq + NA_ROWS - 1
                if ok0 and ok1:
                    blk = jnp.where(col_ok, jnp.where(left, toeplitz[a0], toeplitz[a0 + 1]), neg)
                elif ok0:
                    blk = jnp.where(col_ok & left, toeplitz[a0], neg)
                elif ok1:
                    blk = jnp.where(col_ok & jnp.logical_not(left), toeplitz[a0 + 1], neg)
                else:
                    blk = neg
                bias_ref[g, qi * GRID_W:(qi + 1) * GRID_W, p * 2 * GRID_W:(p + 1) * 2 * GRID_W] = blk


def _na_attn_kernel(rb_ref, q_ref, k_ref, v_ref, ck_ref, cv_ref, o_ref, bias_ref):
    head = pl.program_id(0)

    @pl.when(pl.program_id(1) == 0)
    def _():
        _na_build_bias(rb_ref, bias_ref, head)

    ck = _cache_head(ck_ref, head, N_HEADS_B)
    cv = _cache_head(cv_ref, head, N_HEADS_B)
    q_rows = NA_GROUP_ROWS * GRID_W
    for g in range(GRID_ROWS // NA_GROUP_ROWS):
        first, n_rows = _na_window(g)
        k0, k_rows = first * GRID_W, n_rows * GRID_W
        q = q_ref[g * q_rows:(g + 1) * q_rows, :]
        s_loc = _dot_nt(q, k_ref[k0:k0 + k_rows, :]) + bias_ref[g, :, :k_rows]
        s_ctx = _dot_nt(q, ck)
        o = _softmax_av([s_loc, s_ctx], [v_ref[k0:k0 + k_rows, :], cv])
        o_ref[g * q_rows:(g + 1) * q_rows, :] = o.astype(BF16)


def _neighborhood_attention(rel_bias_flat, qb, kb, vb, ck, cv):
    tok_spec = pl.BlockSpec((DEC_SEQ, HEAD_DIM), lambda h, b: (b, h))
    c_spec = pl.BlockSpec((PAST_LEN * N_HEADS_B, HEAD_DIM), lambda h, b: (b, 0))
    n_groups = GRID_ROWS // NA_GROUP_ROWS
    return pl.pallas_call(
        _na_attn_kernel,
        grid=(N_HEADS_B, DEC_BATCH),
        in_specs=[pl.BlockSpec(memory_space=pltpu.SMEM), tok_spec, tok_spec, tok_spec, c_spec, c_spec],
        out_specs=tok_spec,
        out_shape=_hbm_array((N_LAT_TOK, Q_B_COLS), BF16),
        scratch_shapes=[pltpu.VMEM((n_groups, NA_GROUP_ROWS * GRID_W, NA_WIN_ROWS * GRID_W), F32)],
        compiler_params=_params(32, 2),
        name="neighborhood_attention",
    )(rel_bias_flat, *_in_hbm(qb, kb, vb, ck, cv))


def _outproj_kernel(oa_ref, ob_ref, wo_ref, x_ref, gate1_ref, shift2_ref, scale2_ref, g_ref, b_ref, wr_ref,
                    x1_ref, h2_ref, aff_ref):
    lanes = aff_ref.shape[2]
    r0 = 0
    for n_rows in OUTPROJ_SUBTILES:
        rows = slice(r0, r0 + n_rows)
        attn = _dot(oa_ref[rows, :], wo_ref[:Q_A_COLS, :]) + _dot(ob_ref[rows, :], wo_ref[Q_A_COLS:, :])
        x1 = _layer_norm(DEEPNORM_ALPHA * x_ref[rows, :] + gate1_ref[...] * attn, g_ref[...], b_ref[...])
        x1_ref[rows, :] = x1
        h2 = (x1 * (1.0 + scale2_ref[...]) + shift2_ref[...]).astype(BF16)
        h2_ref[rows, :] = h2
        lt = _dot(h2, wr_ref[...]).T[:N_EXPERTS, :]
        e = jnp.exp(lt - jnp.max(lt, axis=0, keepdims=True))
        req, l0 = divmod(r0, lanes)
        aff_ref[req, :, l0:l0 + n_rows] = e / jnp.sum(e, axis=0, keepdims=True)
        r0 += n_rows


def _output_projection(oa, ob, wo_bf16, x, mod3, ln_g, ln_b, wr_pad, seq, latent):
    n_tok = x.shape[0]
    tm = OUTPROJ_TILE
    row = lambda i: (i, 0)
    const = lambda i: (0, 0)
    if seq >= tm:
        tiles_per_request = seq // tm
        aff_spec = pl.BlockSpec((1, N_EXPERTS, tm), lambda i: (i // tiles_per_request, 0, i % tiles_per_request))
    else:
        aff_spec = pl.BlockSpec((tm // seq, N_EXPERTS, seq), lambda i: (i, 0, 0))
    return pl.pallas_call(
        _outproj_kernel,
        grid=(n_tok // tm,),
        in_specs=[pl.BlockSpec((tm, Q_A_COLS), row), pl.BlockSpec((tm, Q_B_COLS), row),
                  pl.BlockSpec((Q_A_COLS + Q_B_COLS, D_MODEL), const, pipeline_mode=pl.Buffered(1)),
                  pl.BlockSpec((tm, D_MODEL), row),
                  _mod_spec(2, latent, tm), _mod_spec(3, latent, tm), _mod_spec(4, latent, tm),
                  pl.BlockSpec((1, D_MODEL), const), pl.BlockSpec((1, D_MODEL), const),
                  pl.BlockSpec((D_MODEL, V7X_LANES), const)],
        out_specs=[pl.BlockSpec((tm, D_MODEL), row), pl.BlockSpec((tm, D_MODEL), row), aff_spec],
        out_shape=[_hbm_array((n_tok, D_MODEL), F32), _hbm_array((n_tok, D_MODEL), BF16),
                   _hbm_array((n_tok // seq, N_EXPERTS, seq), F32)],
        compiler_params=_params(56, 1),
        name="outproj_latent" if latent else "outproj_context",
    )(*_in_hbm(oa, ob, wo_bf16, x), mod3, mod3, mod3, ln_g, ln_b, wr_pad)


def _pad_rows_to_lanes(a):
    return jnp.concatenate([a, jnp.zeros((V7X_LANES - a.shape[0], a.shape[1]), a.dtype)], axis=0)


def _kth_largest(a, k):
    n = a.shape[1]
    lane = lax.broadcasted_iota(jnp.int32, (1, n), 1)
    x = a
    size = 2
    while size <= n:
        descending = (lane & size) == 0
        j = size // 2
        while j >= 1:
            lower = (lane & j) == 0
            partner = jnp.where(lower, pltpu.roll(x, n - j, 1), pltpu.roll(x, j, 1))
            x = jnp.where(lower == descending, jnp.maximum(x, partner), jnp.minimum(x, partner))
            j //= 2
        size *= 2
    return x[:, k - 1:k]


def _route_kernel(aff_ref, slot_ref, *, cap):
    n_req, n_exp, seq = aff_ref.shape
    key = aff_ref[...].reshape(n_req * n_exp, seq)
    tau = _kth_largest(key, cap)
    above = key > tau
    tied = key == tau
    need = cap - jnp.sum(jnp.where(above, 1.0, 0.0), axis=-1, keepdims=True)
    before = jnp.where(lax.broadcasted_iota(jnp.int32, (seq, seq), 0) < lax.broadcasted_iota(jnp.int32, (seq, seq), 1),
                       1.0, 0.0).astype(BF16)
    tied_before = _dot(jnp.where(tied, 1.0, 0.0).astype(BF16), before)
    chosen = above | (tied & (tied_before < need))
    slot = _dot(jnp.where(chosen, 1.0, 0.0).astype(BF16), before)
    slot_ref[...] = jnp.where(chosen, slot, float(cap)).reshape(n_req, n_exp, seq)


def _route(aff, cap):
    n_req, _, seq = aff.shape
    spec = pl.BlockSpec((n_req, N_EXPERTS, seq), lambda r: (0, 0, 0))
    return pl.pallas_call(
        functools.partial(_route_kernel, cap=cap),
        grid=(1,),
        in_specs=[spec], out_specs=spec,
        out_shape=_hbm_array(aff.shape, F32),
        compiler_params=_params(32, 1),
        name=f"route_s{seq}",
    )(*_in_hbm(aff))


def _dispatch_kernel(slot_ref, aff_ref, h2_ref, xs_ref, gate_ref, slot_t_ref, *, cap, experts_per_dot):
    n_req, _, seq = slot_ref.shape
    slot = lax.broadcasted_iota(jnp.int32, (cap, seq), 0).astype(F32)
    for r in range(n_req):
        slot_of = slot_ref[r]
        aff = aff_ref[r]
        rows = slice(r * seq, (r + 1) * seq)
        slot_t_ref[rows, :] = _pad_rows_to_lanes(slot_of).T
        h2 = h2_ref[rows, :]
        for c in range(N_EXPERTS // experts_per_dot):
            chunk = range(c * experts_per_dot, (c + 1) * experts_per_dot)
            onehot = jnp.concatenate([jnp.where(slot_of[e:e + 1, :] == slot, 1.0, 0.0) for e in chunk], axis=0)
            xs = _dot(onehot.astype(BF16), h2)
            for i, e in enumerate(chunk):
                xs_ref[r, e] = xs[i * cap:(i + 1) * cap].astype(BF16)
                gate = jnp.sum(onehot[i * cap:(i + 1) * cap] * aff[e:e + 1, :], axis=1, keepdims=True)
                gate_ref[r, e] = jnp.broadcast_to(gate, (cap, V7X_LANES))


def _dispatch(slot, aff, h2, cap, experts_per_dot, requests_per_step):
    n_req, _, seq = aff.shape
    g = requests_per_step
    req_spec = pl.BlockSpec((g, N_EXPERTS, seq), lambda r: (r, 0, 0))
    return pl.pallas_call(
        functools.partial(_dispatch_kernel, cap=cap, experts_per_dot=experts_per_dot),
        grid=(n_req // g,),
        in_specs=[req_spec, req_spec, pl.BlockSpec((g * seq, D_MODEL), lambda r: (r, 0))],
        out_specs=[pl.BlockSpec((g, N_EXPERTS, cap, D_MODEL), lambda r: (r, 0, 0, 0)),
                   pl.BlockSpec((g, N_EXPERTS, cap, V7X_LANES), lambda r: (r, 0, 0, 0)),
                   pl.BlockSpec((g * seq, V7X_LANES), lambda r: (r, 0))],
        out_shape=[_hbm_array((n_req, N_EXPERTS, cap, D_MODEL), BF16),
                   _hbm_array((n_req, N_EXPERTS, cap, V7X_LANES), F32),
                   _hbm_array((n_req * seq, V7X_LANES), F32)],
        compiler_params=_params(48, 1),
        name=f"dispatch_s{seq}",
    )(*_in_hbm(slot, aff, h2))


def _expert_rows(ref):
    n_req, cap, width = ref.shape
    return ref[...].reshape(n_req * cap, width)


def _ffn_up_kernel(xc_ref, xl_ref, wg_ref, wu_ref, hc_ref, hl_ref):
    wg = wg_ref[...].astype(BF16)
    wu = wu_ref[...].astype(BF16)
    for x_ref, h_ref in ((xc_ref, hc_ref), (xl_ref, hl_ref)):
        x = _expert_rows(x_ref)
        h_ref[...] = (jax.nn.silu(_dot(x, wg)) * _dot(x, wu)).astype(BF16)


def _ffn_down_kernel(hc_ref, hl_ref, wd_ref, gc_ref, gl_ref, yc_ref, yl_ref):
    wd = wd_ref[...].astype(BF16)
    for h_ref, g_ref, y_ref in ((hc_ref, gc_ref, yc_ref), (hl_ref, gl_ref, yl_ref)):
        gate = jnp.concatenate([_expert_rows(g_ref)] * (wd.shape[1] // V7X_LANES), axis=1)
        y_ref[...] = (_dot(h_ref[...], wd) * gate).astype(BF16).reshape(y_ref.shape)


def _expert_ffn(xs_ctx, xs_lat, gate_ctx, gate_lat, w_gate, w_up, w_down):
    def per_expert(a, width):
        n_req, _, cap, _ = a.shape
        return pl.BlockSpec((n_req, None, cap, width), lambda e, j: (0, e, 0, 0 if width == a.shape[3] else j))

    rows_c = xs_ctx.shape[0] * xs_ctx.shape[2]
    rows_l = xs_lat.shape[0] * xs_lat.shape[2]
    tn = FFN_COL_TILE
    hid_c = pl.BlockSpec((None, rows_c, tn), lambda e, j: (e, 0, j))
    hid_l = pl.BlockSpec((None, rows_l, tn), lambda e, j: (e, 0, j))
    hidden_c, hidden_l = pl.pallas_call(
        _ffn_up_kernel,
        grid=(N_EXPERTS, D_FF // tn),
        in_specs=[per_expert(xs_ctx, D_MODEL), per_expert(xs_lat, D_MODEL),
                  pl.BlockSpec((None, D_MODEL, tn), lambda e, j: (e, 0, j)),
                  pl.BlockSpec((None, D_MODEL, tn), lambda e, j: (e, 0, j))],
        out_specs=[hid_c, hid_l],
        out_shape=[_hbm_array((N_EXPERTS, rows_c, D_FF), BF16),
                   _hbm_array((N_EXPERTS, rows_l, D_FF), BF16)],
        compiler_params=_params(56, 2),
        name="ffn_up",
    )(*_in_hbm(xs_ctx, xs_lat), w_gate, w_up)
    tn = FFN_DOWN_COL_TILE
    return pl.pallas_call(
        _ffn_down_kernel,
        grid=(N_EXPERTS, D_MODEL // tn),
        in_specs=[pl.BlockSpec((None, rows_c, D_FF), lambda e, j: (e, 0, 0)),
                  pl.BlockSpec((None, rows_l, D_FF), lambda e, j: (e, 0, 0)),
                  pl.BlockSpec((None, D_FF, tn), lambda e, j: (e, 0, j)),
                  per_expert(gate_ctx, V7X_LANES), per_expert(gate_lat, V7X_LANES)],
        out_specs=[per_expert(xs_ctx, tn), per_expert(xs_lat, tn)],
        out_shape=[_hbm_array(xs_ctx.shape, BF16), _hbm_array(xs_lat.shape, BF16)],
        compiler_params=_params(56, 2),
        name="ffn_down",
    )(*_in_hbm(hidden_c, hidden_l), w_down, *_in_hbm(gate_ctx, gate_lat))


def _combine_kernel(slot_t_ref, y_ref, x1_ref, gate2_ref, g_ref, b_ref, o_ref, *, cap):
    ts = TOKEN_TILE
    n_sub = slot_t_ref.shape[0] // ts
    cols = N_EXPERTS * cap
    shift = cap.bit_length() - 1
    erow = lax.broadcasted_iota(jnp.int32, (V7X_LANES, cols), 0)
    ecol = lax.shift_right_logical(lax.broadcasted_iota(jnp.int32, (V7X_LANES, cols), 1), shift)
    expand = jnp.where(erow == ecol, 1.0, 0.0).astype(BF16)
    slot = (lax.broadcasted_iota(jnp.int32, (ts, cols), 1) & (cap - 1)).astype(F32)
    for r in range(n_sub):
        rows = slice(r * ts, (r + 1) * ts)
        req = r if y_ref.shape[0] > 1 else 0
        rk = slot_t_ref[rows, :].astype(BF16)
        onehot_t = jnp.where(_dot(rk, expand) == slot, 1.0, 0.0).astype(BF16)
        ffn = _dot(onehot_t, y_ref[req].reshape(cols, D_MODEL))
        o_ref[rows, :] = _layer_norm(DEEPNORM_ALPHA * x1_ref[rows, :] + gate2_ref[...] * ffn,
                                     g_ref[...], b_ref[...])


def _combine(slot_t, y, x1, mod3, ln_g, ln_b, n_req, seq, cap, latent):
    ts = COMBINE_TILE
    steps = max(seq // ts, 1)
    g = max(ts // seq, 1)
    assert g == 1 or not latent
    ts //= g
    row = lambda r, t: (r * steps + t, 0)
    const = lambda r, t: (0, 0)
    mod_row = (lambda r, t: ((1 + r) * N_MOD + 5, 0, 0)) if latent else (lambda r, t: (5, 0, 0))
    return pl.pallas_call(
        functools.partial(_combine_kernel, cap=cap),
        grid=(n_req // g, steps),
        in_specs=[pl.BlockSpec((g * ts, V7X_LANES), row),
                  pl.BlockSpec((g, N_EXPERTS, cap, D_MODEL), lambda r, t: (r, 0, 0, 0)),
                  pl.BlockSpec((g * ts, D_MODEL), row),
                  pl.BlockSpec((None, 1, D_MODEL), mod_row),
                  pl.BlockSpec((1, D_MODEL), const), pl.BlockSpec((1, D_MODEL), const)],
        out_specs=pl.BlockSpec((g * ts, D_MODEL), row),
        out_shape=_hbm_array((n_req * seq, D_MODEL), F32),
        compiler_params=_params(48, 2),
        name=f"combine_s{seq}",
    )(*_in_hbm(slot_t, y, x1), mod3, ln_g, ln_b)


def _rope_tables():
    t = jnp.arange(DEC_SEQ)
    row = (t // GRID_W).astype(F32)
    col = (t % GRID_W).astype(F32)
    n_freq = HEAD_DIM // 4
    inv_freq = ROPE_THETA ** (-jnp.arange(n_freq, dtype=F32) / n_freq)
    ang = jnp.concatenate([row[:, None] * inv_freq, col[:, None] * inv_freq], axis=-1)
    cos = jnp.repeat(jnp.cos(ang), 2, axis=-1)
    sin = jnp.stack([-jnp.sin(ang), jnp.sin(ang)], axis=-1).reshape(DEC_SEQ, HEAD_DIM)
    return cos, sin


def kernel(x_prompt, x_sample, cache_k_a, cache_v_a, cache_k_b, cache_v_b, c, c_ctx, w_mod, b_mod, w_in, w_o,
           sink_a, rel_bias_b, w_router, w_gate, w_up, w_down, ln1_g, ln1_b, ln2_g, ln2_b):
    assert DEPTH == 1 and w_in.shape == (DEPTH, D_MODEL, IN_COLS)
    x_ctx = x_prompt.reshape(N_CTX_TOK, D_MODEL)
    x_lat = x_sample.reshape(N_LAT_TOK, D_MODEL)

    cond = jnp.concatenate([c_ctx[None, :], c, jnp.zeros((V7X_SUBLANES - N_COND, D_MODEL), F32)], axis=0)
    mod = _modulation(cond, w_mod[0], b_mod)
    mod3 = mod.reshape(V7X_SUBLANES * N_MOD, 1, D_MODEL)

    w_in_bf16 = w_in[0].astype(BF16)
    wr_pad = jnp.pad(w_router[0], ((0, 0), (0, V7X_LANES - N_EXPERTS))).astype(BF16)
    sink = sink_a[0]
    ln1 = (ln1_g, ln1_b)
    ln2 = (ln2_g, ln2_b)

    qa, ka, va, qb, kb, vb, wo_bf16 = _input_projection(x_lat, mod3, w_in_bf16, _rope_tables(), w_o[0], True)
    oa = _window_attention(sink, qa, ka, va,
                           cache_k_a.reshape(-1, HEAD_DIM), cache_v_a.reshape(-1, HEAD_DIM))
    ob = _neighborhood_attention(rel_bias_b.reshape(-1), qb, kb, vb,
                                 cache_k_b.reshape(-1, HEAD_DIM), cache_v_b.reshape(-1, HEAD_DIM))
    x1_lat, h2_lat, aff_lat = _output_projection(oa, ob, wo_bf16, x_lat, mod3, *ln1, wr_pad, DEC_SEQ, True)
    slot_lat = _route(aff_lat, CAP_LAT)
    xs_lat, gate_lat, slot_t_lat = _dispatch(slot_lat, aff_lat, h2_lat, CAP_LAT, 8, 1)

    qa, ka, va, qb, kb, vb, ka_f, va_f, kb_f, vb_f = _input_projection(x_ctx, mod3, w_in_bf16, None, None, False)
    oa, ob = _context_attention(sink, qa, ka, va, qb, kb, vb)
    x1_ctx, h2_ctx, aff_ctx = _output_projection(oa, ob, wo_bf16, x_ctx, mod3, *ln1, wr_pad, SEQ, False)
    slot_ctx = _route(aff_ctx, CAP_CTX)
    xs_ctx, gate_ctx, slot_t_ctx = _dispatch(slot_ctx, aff_ctx, h2_ctx, CAP_CTX, N_EXPERTS, CTX_DISPATCH_REQUESTS)

    y_ctx, y_lat = _expert_ffn(xs_ctx, xs_lat, gate_ctx, gate_lat, w_gate[0], w_up[0], w_down[0])

    y_p = _combine(slot_t_ctx, y_ctx, x1_ctx, mod3, *ln2, BATCH, SEQ, CAP_CTX, False)
    y_s = _combine(slot_t_lat, y_lat, x1_lat, mod3, *ln2, DEC_BATCH, DEC_SEQ, CAP_LAT, True)

    kv_a_shape = (BATCH, DEPTH, SEQ, N_KV_A, HEAD_DIM)
    kv_b_shape = (BATCH, DEPTH, SEQ, N_HEADS_B, HEAD_DIM)
    return (y_p.reshape(BATCH, SEQ, D_MODEL), y_s.reshape(DEC_BATCH, DEC_SEQ, D_MODEL),
            ka_f.reshape(kv_a_shape), va_f.reshape(kv_a_shape), kb_f.reshape(kv_b_shape), vb_f.reshape(kv_b_shape))
```

```python
import functools

import jax
import jax.numpy as jnp
from jax import lax
from jax.experimental import pallas as pl
from jax.experimental.pallas import tpu as pltpu

D_MODEL = 2048
BATCH = 32
SEQ = 256
DEC_BATCH = 4
DEC_SEQ = 1024
PAST_LEN = 256
GRID_W = 64
HEAD_DIM = 128
N_HEADS_A = 8
N_KV_A = 2
GROUP_A = N_HEADS_A // N_KV_A
N_HEADS_B = 8
WINDOW_A = 128
NA_ROWS = 8
NA_COLS = 16
N_EXPERTS = 16
EC_CAPACITY = 2
D_FF = D_MODEL
ROPE_THETA = 10000.0
LN_EPS = 1e-5
NEG_INF = -1e30
DEPTH = 1
Q_A_COLS = N_HEADS_A * HEAD_DIM
KV_A_COLS = N_KV_A * HEAD_DIM
Q_B_COLS = N_HEADS_B * HEAD_DIM
IN_COLS = Q_A_COLS + 2 * KV_A_COLS + 3 * Q_B_COLS
COL_STARTS = (0, Q_A_COLS, Q_A_COLS + KV_A_COLS, Q_A_COLS + 2 * KV_A_COLS,
              Q_A_COLS + 2 * KV_A_COLS + Q_B_COLS, Q_A_COLS + 2 * KV_A_COLS + 2 * Q_B_COLS, IN_COLS)
DEEPNORM_ALPHA = (2.0 * DEPTH) ** 0.25
ATTN_SCALE = HEAD_DIM ** -0.5

N_CTX_TOK = BATCH * SEQ
N_LAT_TOK = DEC_BATCH * DEC_SEQ
GRID_ROWS = DEC_SEQ // GRID_W
CAP_CTX = EC_CAPACITY * SEQ // N_EXPERTS
CAP_LAT = EC_CAPACITY * DEC_SEQ // N_EXPERTS
N_COND = 1 + DEC_BATCH
N_MOD = 6

V7X_LANES = 128
V7X_SUBLANES = 8
V7X_VMEM_BYTES = 64 * 1024 * 1024
MIB = 1024 * 1024

F32 = jnp.float32
BF16 = jnp.bfloat16
NT_DIMS = (((1,), (1,)), ((), ()))

TOKEN_TILE = 256
MOD_COL_TILE = 1024
FFN_COL_TILE = 512
FFN_DOWN_COL_TILE = 1024
INPROJ_TILE = 512
OUTPROJ_TILE = 512
OUTPROJ_SUBTILES = (256, 256)
CTX_ATTN_REQUESTS = 1
CTX_DISPATCH_REQUESTS = 4
COMBINE_TILE = 512
WINDOW_Q_BLOCK = 128
NA_GROUP_ROWS = 4
NA_WIN_ROWS = 12
NA_REL_ROWS = 2 * NA_ROWS - 1
NA_REL_COLS = 2 * NA_COLS - 1


def _params(vmem_mib, n_axes):
    assert vmem_mib * MIB < V7X_VMEM_BYTES
    return pltpu.CompilerParams(dimension_semantics=("arbitrary",) * n_axes,
                                vmem_limit_bytes=vmem_mib * MIB)


def _hbm_array(shape, dtype):
    return pltpu.HBM(tuple(shape), dtype)


def _in_hbm(*arrays):
    return [pltpu.with_memory_space_constraint(a, pltpu.HBM) for a in arrays]


def _dot(a, b):
    return jnp.dot(a, b, preferred_element_type=F32)


def _dot_nt(a, b):
    return lax.dot_general(a, b, NT_DIMS, preferred_element_type=F32)


def _layer_norm(z, g, b):
    mu = jnp.mean(z, axis=-1, keepdims=True)
    zc = z - mu
    var = jnp.mean(zc * zc, axis=-1, keepdims=True)
    return zc * lax.rsqrt(var + LN_EPS) * g + b


def _mod_kernel(cond_ref, w_ref, b_ref, o_ref):
    s = jax.nn.silu(cond_ref[...]).astype(BF16)
    o_ref[...] = _dot(s, w_ref[...].astype(BF16)) + b_ref[...]


def _modulation(cond, w_mod, b_mod):
    n_out = N_MOD * D_MODEL
    return pl.pallas_call(
        _mod_kernel,
        grid=(n_out // MOD_COL_TILE,),
        in_specs=[pl.BlockSpec((V7X_SUBLANES, D_MODEL), lambda j: (0, 0)),
                  pl.BlockSpec((D_MODEL, MOD_COL_TILE), lambda j: (0, j)),
                  pl.BlockSpec((1, MOD_COL_TILE), lambda j: (0, j))],
        out_specs=pl.BlockSpec((V7X_SUBLANES, MOD_COL_TILE), lambda j: (0, j)),
        out_shape=_hbm_array((V7X_SUBLANES, n_out), F32),
        compiler_params=_params(40, 1),
        name="modulation",
    )(cond, w_mod, b_mod)


def _mod_spec(which, latent, tile):
    steps_per_request = DEC_SEQ // tile
    if latent:
        return pl.BlockSpec((None, 1, D_MODEL), lambda i: ((1 + i // steps_per_request) * N_MOD + which, 0, 0))
    return pl.BlockSpec((None, 1, D_MODEL), lambda i: (which, 0, 0))


def _rope(x, cos, sin, even):
    swapped = jnp.where(even, pltpu.roll(x, HEAD_DIM - 1, 1), pltpu.roll(x, 1, 1))
    return x * cos + swapped * sin


def _inproj_kernel(*refs, latent):
    if latent:
        (x_ref, shift_ref, scale_ref, w_ref, cos_ref, sin_ref, wo_ref,
         qa_o, ka_o, va_o, qb_o, kb_o, vb_o, wo_o) = refs
        wo_o[...] = wo_ref[...].astype(BF16)
    else:
        (x_ref, shift_ref, scale_ref, w_ref,
         qa_o, ka_o, va_o, qb_o, kb_o, vb_o, kaf_o, vaf_o, kbf_o, vbf_o) = refs
    h = (x_ref[...] * (1.0 + scale_ref[...]) + shift_ref[...]).astype(BF16)

    def proj(piece):
        return _dot(h, w_ref[:, COL_STARTS[piece]:COL_STARTS[piece + 1]])

    if latent:
        cos = cos_ref[...]
        sin = sin_ref[...]
        even = (lax.broadcasted_iota(jnp.int32, cos.shape, 1) & 1) == 0

    def heads(y, o_ref, n_heads, scale, rope):
        for hd in range(n_heads):
            cols = slice(hd * HEAD_DIM, (hd + 1) * HEAD_DIM)
            yh = y[:, cols]
            if rope:
                yh = _rope(yh, cos, sin, even)
            if scale != 1.0:
                yh = yh * scale
            o_ref[:, cols] = yh.astype(BF16)

    qa = proj(0)
    heads(qa, qa_o, N_HEADS_A, ATTN_SCALE, latent)
    ka = proj(1)
    heads(ka, ka_o, N_KV_A, 1.0, latent)
    va = proj(2)
    va_o[...] = va.astype(BF16)
    qb = proj(3)
    qb_o[...] = (qb * ATTN_SCALE).astype(BF16)
    kb = proj(4)
    kb_o[...] = kb.astype(BF16)
    vb = proj(5)
    vb_o[...] = vb.astype(BF16)
    if not latent:
        for o_ref, y in ((kaf_o, ka), (vaf_o, va)):
            for hd in range(N_KV_A):
                o_ref[pl.ds(hd, y.shape[0], stride=N_KV_A), :] = y[:, hd * HEAD_DIM:(hd + 1) * HEAD_DIM]
        kbf_o[...] = kb.reshape(kbf_o.shape)
        vbf_o[...] = vb.reshape(vbf_o.shape)


def _input_projection(x, mod3, w_in_bf16, rope_tables, w_o, latent):
    n_tok = x.shape[0]
    tm = INPROJ_TILE
    row = lambda i: (i, 0)
    in_specs = [pl.BlockSpec((tm, D_MODEL), row), _mod_spec(0, latent, tm), _mod_spec(1, latent, tm),
                pl.BlockSpec((D_MODEL, IN_COLS), lambda i: (0, 0), pipeline_mode=pl.Buffered(1))]
    args = [x, mod3, mod3, w_in_bf16]
    widths = [Q_A_COLS, KV_A_COLS, KV_A_COLS, Q_B_COLS, Q_B_COLS, Q_B_COLS]
    out_specs = [pl.BlockSpec((tm, w), row) for w in widths]
    out_shape = [_hbm_array((n_tok, w), BF16) for w in widths]
    if latent:
        steps_per_request = DEC_SEQ // tm
        in_specs += [pl.BlockSpec((tm, HEAD_DIM), lambda i: (i % steps_per_request, 0))] * 2
        args += list(rope_tables)
        wo_rows = w_o.shape[0] // (n_tok // tm)
        in_specs += [pl.BlockSpec((wo_rows, D_MODEL), row)]
        args += [w_o]
        out_specs += [pl.BlockSpec((wo_rows, D_MODEL), row)]
        out_shape += [_hbm_array(w_o.shape, BF16)]
    else:
        out_specs += [pl.BlockSpec((tm * N_KV_A, HEAD_DIM), row)] * 2
        out_shape += [_hbm_array((n_tok * N_KV_A, HEAD_DIM), F32)] * 2
        out_specs += [pl.BlockSpec((tm * N_HEADS_B, HEAD_DIM), row)] * 2
        out_shape += [_hbm_array((n_tok * N_HEADS_B, HEAD_DIM), F32)] * 2
    return pl.pallas_call(
        functools.partial(_inproj_kernel, latent=latent),
        grid=(n_tok // tm,),
        in_specs=in_specs, out_specs=out_specs, out_shape=out_shape,
        compiler_params=_params(58, 1),
        name="inproj_latent" if latent else "inproj_context",
    )(*args)


def _with_ones(v):
    return jnp.concatenate([v, jnp.ones((v.shape[0], V7X_LANES), v.dtype)], axis=1)


def _softmax_av(scores, values_with_ones, sink=None):
    m = functools.reduce(jnp.maximum, [jnp.max(s, axis=-1, keepdims=True) for s in scores])
    if sink is not None:
        m = jnp.maximum(m, sink)
    acc = functools.reduce(jnp.add, [_dot(jnp.exp(s - m).astype(BF16), v)
                                     for s, v in zip(scores, values_with_ones)])
    denom = acc[:, HEAD_DIM:]
    if sink is not None:
        denom = denom + jnp.exp(sink - m)
    return acc[:, :HEAD_DIM] / denom


def _cache_head(c_ref, head, n_heads):
    return c_ref[pl.ds(head, PAST_LEN, stride=n_heads), :].astype(BF16)


def _ctx_attn_kernel(sink_ref, qa_ref, ka_ref, va_ref, qb_ref, kb_ref, vb_ref, oa_ref, ob_ref):
    for req in range(qa_ref.shape[0] // SEQ):
        rows = slice(req * SEQ, (req + 1) * SEQ)
        for kv in range(N_KV_A):
            kv_cols = slice(kv * HEAD_DIM, (kv + 1) * HEAD_DIM)
            k = ka_ref[rows, kv_cols]
            v = _with_ones(va_ref[rows, kv_cols])
            for hd in range(kv * GROUP_A, (kv + 1) * GROUP_A):
                cols = slice(hd * HEAD_DIM, (hd + 1) * HEAD_DIM)
                sink = jnp.full((SEQ, 1), sink_ref[hd], F32)
                oa_ref[rows, cols] = _softmax_av([_dot_nt(qa_ref[rows, cols], k)], [v], sink).astype(BF16)
        for hd in range(N_HEADS_B):
            cols = slice(hd * HEAD_DIM, (hd + 1) * HEAD_DIM)
            o = _softmax_av([_dot_nt(qb_ref[rows, cols], kb_ref[rows, cols])], [_with_ones(vb_ref[rows, cols])])
            ob_ref[rows, cols] = o.astype(BF16)


def _context_attention(sink, qa, ka, va, qb, kb, vb):
    row = lambda b: (b, 0)
    rows = CTX_ATTN_REQUESTS * SEQ
    widths = [Q_A_COLS, KV_A_COLS, KV_A_COLS, Q_B_COLS, Q_B_COLS, Q_B_COLS]
    return pl.pallas_call(
        _ctx_attn_kernel,
        grid=(BATCH // CTX_ATTN_REQUESTS,),
        in_specs=[pl.BlockSpec(memory_space=pltpu.SMEM)] + [pl.BlockSpec((rows, w), row) for w in widths],
        out_specs=[pl.BlockSpec((rows, Q_A_COLS), row), pl.BlockSpec((rows, Q_B_COLS), row)],
        out_shape=[_hbm_array((N_CTX_TOK, Q_A_COLS), BF16),
                   _hbm_array((N_CTX_TOK, Q_B_COLS), BF16)],
        compiler_params=_params(32, 1),
        name="context_attention",
    )(sink, *_in_hbm(qa, ka, va, qb, kb, vb))


def _window_attn_kernel(sink_ref, q_ref, k_ref, v_ref, ck_ref, cv_ref, o_ref):
    kv = pl.program_id(1)
    qb = WINDOW_Q_BLOCK
    span = qb + 2 * WINDOW_A
    ck = _cache_head(ck_ref, kv, N_KV_A)
    cv = _with_ones(_cache_head(cv_ref, kv, N_KV_A))
    band = (lax.broadcasted_iota(jnp.int32, (qb, span), 1)
            - lax.broadcasted_iota(jnp.int32, (qb, span), 0))
    masks = {}
    for n in range(DEC_SEQ // qb):
        start = min(max(n * qb - WINDOW_A, 0), DEC_SEQ - span)
        offset = start - n * qb
        if offset not in masks:
            masks[offset] = jnp.where(jnp.abs(band + offset) <= WINDOW_A, 0.0, NEG_INF)
        rows = slice(n * qb, (n + 1) * qb)
        k_win = k_ref[start:start + span, :]
        v_win = _with_ones(v_ref[start:start + span, :])
        for g in range(GROUP_A):
            cols = slice(g * HEAD_DIM, (g + 1) * HEAD_DIM)
            q = q_ref[rows, cols]
            sink = jnp.full((qb, 1), sink_ref[kv * GROUP_A + g], F32)
            o = _softmax_av([_dot_nt(q, k_win) + masks[offset], _dot_nt(q, ck)], [v_win, cv], sink)
            o_ref[rows, cols] = o.astype(BF16)


def _window_attention(sink, qa, ka, va, ck, cv):
    q_spec = pl.BlockSpec((DEC_SEQ, GROUP_A * HEAD_DIM), lambda b, kv: (b, kv))
    kv_spec = pl.BlockSpec((DEC_SEQ, HEAD_DIM), lambda b, kv: (b, kv))
    c_spec = pl.BlockSpec((PAST_LEN * N_KV_A, HEAD_DIM), lambda b, kv: (b, 0))
    return pl.pallas_call(
        _window_attn_kernel,
        grid=(DEC_BATCH, N_KV_A),
        in_specs=[pl.BlockSpec(memory_space=pltpu.SMEM), q_spec, kv_spec, kv_spec, c_spec, c_spec],
        out_specs=q_spec,
        out_shape=_hbm_array((N_LAT_TOK, Q_A_COLS), BF16),
        compiler_params=_params(32, 2),
        name="window_attention",
    )(sink, *_in_hbm(qa, ka, va, ck, cv))


def _na_row_start(rq):
    return min(max(rq - NA_ROWS // 2, 0), GRID_ROWS - NA_ROWS)


def _na_window(group):
    first = _na_row_start(group * NA_GROUP_ROWS)
    last = _na_row_start(group * NA_GROUP_ROWS + NA_GROUP_ROWS - 1) + NA_ROWS - 1
    n_rows = last - first + 1
    n_rows += n_rows % 2
    assert first % 2 == 0 and n_rows <= NA_WIN_ROWS and first + n_rows <= GRID_ROWS
    return first, n_rows


def _na_build_bias(rb_ref, bias_ref, head):
    shape = (GRID_W, 2 * GRID_W)
    lane = lax.broadcasted_iota(jnp.int32, shape, 1)
    qcol = lax.broadcasted_iota(jnp.int32, shape, 0)
    kcol = lane & (GRID_W - 1)
    left = lane < GRID_W
    dcol = kcol - qcol + (NA_COLS - 1)
    wstart = jnp.clip(qcol - NA_COLS // 2, 0, GRID_W - NA_COLS)
    col_ok = (kcol >= wstart) & (kcol < wstart + NA_COLS)
    base = head * (NA_REL_ROWS * NA_REL_COLS)
    toeplitz = []
    for a in range(NA_REL_ROWS):
        acc = jnp.zeros(shape, F32)
        for bb in range(NA_REL_COLS):
            acc = jnp.where(dcol == bb, rb_ref[base + a * NA_REL_COLS + bb], acc)
        toeplitz.append(acc)
    neg = jnp.full(shape, NEG_INF, F32)
    for g in range(GRID_ROWS // NA_GROUP_ROWS):
        wrow, n_rows = _na_window(g)
        for qi in range(NA_GROUP_ROWS):
            rq = g * NA_GROUP_ROWS + qi
            row0 = _na_row_start(rq)
            for p in range(n_rows // 2):
                rk0 = wrow + 2 * p
                ok0 = row0 <= rk0 < row0 + NA_ROWS
                ok1 = row0 <= rk0 + 1 < row0 + NA_ROWS
                a0 = rk0 - rq + NA_ROWS - 1
                if ok0 and ok1:
                    blk = jnp.where(col_ok, jnp.where(left, toeplitz[a0], toeplitz[a0 + 1]), neg)
                elif ok0:
                    blk = jnp.where(col_ok & left, toeplitz[a0], neg)
                elif ok1:
                    blk = jnp.where(col_ok & jnp.logical_not(left), toeplitz[a0 + 1], neg)
                else:
                    blk = neg
                bias_ref[g, qi * GRID_W:(qi + 1) * GRID_W, p * 2 * GRID_W:(p + 1) * 2 * GRID_W] = blk


def _na_attn_kernel(rb_ref, q_ref, k_ref, v_ref, ck_ref, cv_ref, o_ref, bias_ref):
    head = pl.program_id(0)

    @pl.when(pl.program_id(1) == 0)
    def _():
        _na_build_bias(rb_ref, bias_ref, head)

    ck = _cache_head(ck_ref, head, N_HEADS_B)
    cv = _with_ones(_cache_head(cv_ref, head, N_HEADS_B))
    q_rows = NA_GROUP_ROWS * GRID_W
    for g in range(GRID_ROWS // NA_GROUP_ROWS):
        first, n_rows = _na_window(g)
        k0, k_rows = first * GRID_W, n_rows * GRID_W
        q = q_ref[g * q_rows:(g + 1) * q_rows, :]
        s_loc = _dot_nt(q, k_ref[k0:k0 + k_rows, :]) + bias_ref[g, :, :k_rows]
        s_ctx = _dot_nt(q, ck)
        o = _softmax_av([s_loc, s_ctx], [_with_ones(v_ref[k0:k0 + k_rows, :]), cv])
        o_ref[g * q_rows:(g + 1) * q_rows, :] = o.astype(BF16)


def _neighborhood_attention(rel_bias_flat, qb, kb, vb, ck, cv):
    tok_spec = pl.BlockSpec((DEC_SEQ, HEAD_DIM), lambda h, b: (b, h))
    c_spec = pl.BlockSpec((PAST_LEN * N_HEADS_B, HEAD_DIM), lambda h, b: (b, 0))
    n_groups = GRID_ROWS // NA_GROUP_ROWS
    return pl.pallas_call(
        _na_attn_kernel,
        grid=(N_HEADS_B, DEC_BATCH),
        in_specs=[pl.BlockSpec(memory_space=pltpu.SMEM), tok_spec, tok_spec, tok_spec, c_spec, c_spec],
        out_specs=tok_spec,
        out_shape=_hbm_array((N_LAT_TOK, Q_B_COLS), BF16),
        scratch_shapes=[pltpu.VMEM((n_groups, NA_GROUP_ROWS * GRID_W, NA_WIN_ROWS * GRID_W), F32)],
        compiler_params=_params(32, 2),
        name="neighborhood_attention",
    )(rel_bias_flat, *_in_hbm(qb, kb, vb, ck, cv))


def _outproj_kernel(oa_ref, ob_ref, wo_ref, x_ref, gate1_ref, shift2_ref, scale2_ref, g_ref, b_ref, wr_ref,
                    x1_ref, h2_ref, aff_ref):
    lanes = aff_ref.shape[2]
    r0 = 0
    for n_rows in OUTPROJ_SUBTILES:
        rows = slice(r0, r0 + n_rows)
        attn = _dot(oa_ref[rows, :], wo_ref[:Q_A_COLS, :]) + _dot(ob_ref[rows, :], wo_ref[Q_A_COLS:, :])
        x1 = _layer_norm(DEEPNORM_ALPHA * x_ref[rows, :] + gate1_ref[...] * attn, g_ref[...], b_ref[...])
        x1_ref[rows, :] = x1
        h2 = (x1 * (1.0 + scale2_ref[...]) + shift2_ref[...]).astype(BF16)
        h2_ref[rows, :] = h2
        lt = _dot(h2, wr_ref[...]).T[:N_EXPERTS, :]
        e = jnp.exp(lt - jnp.max(lt, axis=0, keepdims=True))
        req, l0 = divmod(r0, lanes)
        aff_ref[req, :, l0:l0 + n_rows] = e / jnp.sum(e, axis=0, keepdims=True)
        r0 += n_rows


def _output_projection(oa, ob, wo_bf16, x, mod3, ln_g, ln_b, wr_pad, seq, latent):
    n_tok = x.shape[0]
    tm = OUTPROJ_TILE
    row = lambda i: (i, 0)
    const = lambda i: (0, 0)
    if seq >= tm:
        tiles_per_request = seq // tm
        aff_spec = pl.BlockSpec((1, N_EXPERTS, tm), lambda i: (i // tiles_per_request, 0, i % tiles_per_request))
    else:
        aff_spec = pl.BlockSpec((tm // seq, N_EXPERTS, seq), lambda i: (i, 0, 0))
    return pl.pallas_call(
        _outproj_kernel,
        grid=(n_tok // tm,),
        in_specs=[pl.BlockSpec((tm, Q_A_COLS), row), pl.BlockSpec((tm, Q_B_COLS), row),
                  pl.BlockSpec((Q_A_COLS + Q_B_COLS, D_MODEL), const, pipeline_mode=pl.Buffered(1)),
                  pl.BlockSpec((tm, D_MODEL), row),
                  _mod_spec(2, latent, tm), _mod_spec(3, latent, tm), _mod_spec(4, latent, tm),
                  pl.BlockSpec((1, D_MODEL), const), pl.BlockSpec((1, D_MODEL), const),
                  pl.BlockSpec((D_MODEL, V7X_LANES), const)],
        out_specs=[pl.BlockSpec((tm, D_MODEL), row), pl.BlockSpec((tm, D_MODEL), row), aff_spec],
        out_shape=[_hbm_array((n_tok, D_MODEL), F32), _hbm_array((n_tok, D_MODEL), BF16),
                   _hbm_array((n_tok // seq, N_EXPERTS, seq), F32)],
        compiler_params=_params(56, 1),
        name="outproj_latent" if latent else "outproj_context",
    )(*_in_hbm(oa, ob, wo_bf16, x), mod3, mod3, mod3, ln_g, ln_b, wr_pad)


def _pad_rows_to_lanes(a):
    return jnp.concatenate([a, jnp.zeros((V7X_LANES - a.shape[0], a.shape[1]), a.dtype)], axis=0)


def _kth_largest(a, k):
    n = a.shape[1]
    lane = lax.broadcasted_iota(jnp.int32, (1, n), 1)
    x = a
    size = 2
    while size <= n:
        descending = (lane & size) == 0
        j = size // 2
        while j >= 1:
            lower = (lane & j) == 0
            partner = jnp.where(lower, pltpu.roll(x, n - j, 1), pltpu.roll(x, j, 1))
            x = jnp.where(lower == descending, jnp.maximum(x, partner), jnp.minimum(x, partner))
            j //= 2
        size *= 2
    return x[:, k - 1:k]


def _route_kernel(aff_ref, slot_ref, *, cap):
    n_req, n_exp, seq = aff_ref.shape
    key = aff_ref[...].reshape(n_req * n_exp, seq)
    tau = _kth_largest(key, cap)
    above = key > tau
    tied = key == tau
    need = cap - jnp.sum(jnp.where(above, 1.0, 0.0), axis=-1, keepdims=True)
    before = jnp.where(lax.broadcasted_iota(jnp.int32, (seq, seq), 0) < lax.broadcasted_iota(jnp.int32, (seq, seq), 1),
                       1.0, 0.0).astype(BF16)
    tied_before = _dot(jnp.where(tied, 1.0, 0.0).astype(BF16), before)
    chosen = above | (tied & (tied_before < need))
    slot = _dot(jnp.where(chosen, 1.0, 0.0).astype(BF16), before)
    slot_ref[...] = jnp.where(chosen, slot, float(cap)).reshape(n_req, n_exp, seq)


def _route(aff, cap):
    n_req, _, seq = aff.shape
    spec = pl.BlockSpec((n_req, N_EXPERTS, seq), lambda r: (0, 0, 0))
    return pl.pallas_call(
        functools.partial(_route_kernel, cap=cap),
        grid=(1,),
        in_specs=[spec], out_specs=spec,
        out_shape=_hbm_array(aff.shape, F32),
        compiler_params=_params(32, 1),
        name=f"route_s{seq}",
    )(*_in_hbm(aff))


def _dispatch_kernel(slot_ref, aff_ref, h2_ref, xs_ref, gate_ref, slot_t_ref, *, cap, experts_per_dot):
    n_req, _, seq = slot_ref.shape
    slot = lax.broadcasted_iota(jnp.int32, (cap, seq), 0).astype(F32)
    for r in range(n_req):
        slot_of = slot_ref[r]
        aff = aff_ref[r]
        rows = slice(r * seq, (r + 1) * seq)
        slot_t_ref[rows, :] = _pad_rows_to_lanes(slot_of).T
        h2 = h2_ref[rows, :]
        for c in range(N_EXPERTS // experts_per_dot):
            chunk = range(c * experts_per_dot, (c + 1) * experts_per_dot)
            onehot = jnp.concatenate([jnp.where(slot_of[e:e + 1, :] == slot, 1.0, 0.0) for e in chunk], axis=0)
            xs = _dot(onehot.astype(BF16), h2)
            for i, e in enumerate(chunk):
                xs_ref[r, e] = xs[i * cap:(i + 1) * cap].astype(BF16)
                gate = jnp.sum(onehot[i * cap:(i + 1) * cap] * aff[e:e + 1, :], axis=1, keepdims=True)
                gate_ref[r, e] = jnp.broadcast_to(gate, (cap, V7X_LANES))


def _dispatch(slot, aff, h2, cap, experts_per_dot, requests_per_step):
    n_req, _, seq = aff.shape
    g = requests_per_step
    req_spec = pl.BlockSpec((g, N_EXPERTS, seq), lambda r: (r, 0, 0))
    return pl.pallas_call(
        functools.partial(_dispatch_kernel, cap=cap, experts_per_dot=experts_per_dot),
        grid=(n_req // g,),
        in_specs=[req_spec, req_spec, pl.BlockSpec((g * seq, D_MODEL), lambda r: (r, 0))],
        out_specs=[pl.BlockSpec((g, N_EXPERTS, cap, D_MODEL), lambda r: (r, 0, 0, 0)),
                   pl.BlockSpec((g, N_EXPERTS, cap, V7X_LANES), lambda r: (r, 0, 0, 0)),
                   pl.BlockSpec((g * seq, V7X_LANES), lambda r: (r, 0))],
        out_shape=[_hbm_array((n_req, N_EXPERTS, cap, D_MODEL), BF16),
                   _hbm_array((n_req, N_EXPERTS, cap, V7X_LANES), F32),
                   _hbm_array((n_req * seq, V7X_LANES), F32)],
        compiler_params=_params(48, 1),
        name=f"dispatch_s{seq}",
    )(*_in_hbm(slot, aff, h2))


def _expert_rows(ref):
    n_req, cap, width = ref.shape
    return ref[...].reshape(n_req * cap, width)


def _ffn_up_kernel(xc_ref, xl_ref, wg_ref, wu_ref, hc_ref, hl_ref):
    wg = wg_ref[...].astype(BF16)
    wu = wu_ref[...].astype(BF16)
    for x_ref, h_ref in ((xc_ref, hc_ref), (xl_ref, hl_ref)):
        x = _expert_rows(x_ref)
        h_ref[...] = (jax.nn.silu(_dot(x, wg)) * _dot(x, wu)).astype(BF16)


def _ffn_down_kernel(hc_ref, hl_ref, wd_ref, gc_ref, gl_ref, yc_ref, yl_ref):
    wd = wd_ref[...].astype(BF16)
    for h_ref, g_ref, y_ref in ((hc_ref, gc_ref, yc_ref), (hl_ref, gl_ref, yl_ref)):
        gate = jnp.concatenate([_expert_rows(g_ref)] * (wd.shape[1] // V7X_LANES), axis=1)
        y_ref[...] = (_dot(h_ref[...], wd) * gate).astype(BF16).reshape(y_ref.shape)


def _expert_ffn(xs_ctx, xs_lat, gate_ctx, gate_lat, w_gate, w_up, w_down):
    def per_expert(a, width):
        n_req, _, cap, _ = a.shape
        return pl.BlockSpec((n_req, None, cap, width), lambda e, j: (0, e, 0, 0 if width == a.shape[3] else j))

    rows_c = xs_ctx.shape[0] * xs_ctx.shape[2]
    rows_l = xs_lat.shape[0] * xs_lat.shape[2]
    tn = FFN_COL_TILE
    hid_c = pl.BlockSpec((None, rows_c, tn), lambda e, j: (e, 0, j))
    hid_l = pl.BlockSpec((None, rows_l, tn), lambda e, j: (e, 0, j))
    hidden_c, hidden_l = pl.pallas_call(
        _ffn_up_kernel,
        grid=(N_EXPERTS, D_FF // tn),
        in_specs=[per_expert(xs_ctx, D_MODEL), per_expert(xs_lat, D_MODEL),
                  pl.BlockSpec((None, D_MODEL, tn), lambda e, j: (e, 0, j)),
                  pl.BlockSpec((None, D_MODEL, tn), lambda e, j: (e, 0, j))],
        out_specs=[hid_c, hid_l],
        out_shape=[_hbm_array((N_EXPERTS, rows_c, D_FF), BF16),
                   _hbm_array((N_EXPERTS, rows_l, D_FF), BF16)],
        compiler_params=_params(56, 2),
        name="ffn_up",
    )(*_in_hbm(xs_ctx, xs_lat), w_gate, w_up)
    tn = FFN_DOWN_COL_TILE
    return pl.pallas_call(
        _ffn_down_kernel,
        grid=(N_EXPERTS, D_MODEL // tn),
        in_specs=[pl.BlockSpec((None, rows_c, D_FF), lambda e, j: (e, 0, 0)),
                  pl.BlockSpec((None, rows_l, D_FF), lambda e, j: (e, 0, 0)),
                  pl.BlockSpec((None, D_FF, tn), lambda e, j: (e, 0, j)),
                  per_expert(gate_ctx, V7X_LANES), per_expert(gate_lat, V7X_LANES)],
        out_specs=[per_expert(xs_ctx, tn), per_expert(xs_lat, tn)],
        out_shape=[_hbm_array(xs_ctx.shape, BF16), _hbm_array(xs_lat.shape, BF16)],
        compiler_params=_params(56, 2),
        name="ffn_down",
    )(*_in_hbm(hidden_c, hidden_l), w_down, *_in_hbm(gate_ctx, gate_lat))


def _combine_kernel(slot_t_ref, y_ref, x1_ref, gate2_ref, g_ref, b_ref, o_ref, *, cap):
    ts = TOKEN_TILE
    n_sub = slot_t_ref.shape[0] // ts
    cols = N_EXPERTS * cap
    shift = cap.bit_length() - 1
    erow = lax.broadcasted_iota(jnp.int32, (V7X_LANES, cols), 0)
    ecol = lax.shift_right_logical(lax.broadcasted_iota(jnp.int32, (V7X_LANES, cols), 1), shift)
    expand = jnp.where(erow == ecol, 1.0, 0.0).astype(BF16)
    slot = (lax.broadcasted_iota(jnp.int32, (ts, cols), 1) & (cap - 1)).astype(F32)
    for r in range(n_sub):
        rows = slice(r * ts, (r + 1) * ts)
        req = r if y_ref.shape[0] > 1 else 0
        rk = slot_t_ref[rows, :].astype(BF16)
        onehot_t = jnp.where(_dot(rk, expand) == slot, 1.0, 0.0).astype(BF16)
        ffn = _dot(onehot_t, y_ref[req].reshape(cols, D_MODEL))
        o_ref[rows, :] = _layer_norm(DEEPNORM_ALPHA * x1_ref[rows, :] + gate2_ref[...] * ffn,
                                     g_ref[...], b_ref[...])


def _combine(slot_t, y, x1, mod3, ln_g, ln_b, n_req, seq, cap, latent):
    ts = COMBINE_TILE
    steps = max(seq // ts, 1)
    g = max(ts // seq, 1)
    assert g == 1 or not latent
    ts //= g
    row = lambda r, t: (r * steps + t, 0)
    const = lambda r, t: (0, 0)
    mod_row = (lambda r, t: ((1 + r) * N_MOD + 5, 0, 0)) if latent else (lambda r, t: (5, 0, 0))
    return pl.pallas_call(
        functools.partial(_combine_kernel, cap=cap),
        grid=(n_req // g, steps),
        in_specs=[pl.BlockSpec((g * ts, V7X_LANES), row),
                  pl.BlockSpec((g, N_EXPERTS, cap, D_MODEL), lambda r, t: (r, 0, 0, 0)),
                  pl.BlockSpec((g * ts, D_MODEL), row),
                  pl.BlockSpec((None, 1, D_MODEL), mod_row),
                  pl.BlockSpec((1, D_MODEL), const), pl.BlockSpec((1, D_MODEL), const)],
        out_specs=pl.BlockSpec((g * ts, D_MODEL), row),
        out_shape=_hbm_array((n_req * seq, D_MODEL), F32),
        compiler_params=_params(48, 2),
        name=f"combine_s{seq}",
    )(*_in_hbm(slot_t, y, x1), mod3, ln_g, ln_b)


def _rope_tables():
    t = jnp.arange(DEC_SEQ)
    row = (t // GRID_W).astype(F32)
    col = (t % GRID_W).astype(F32)
    n_freq = HEAD_DIM // 4
    inv_freq = ROPE_THETA ** (-jnp.arange(n_freq, dtype=F32) / n_freq)
    ang = jnp.concatenate([row[:, None] * inv_freq, col[:, None] * inv_freq], axis=-1)
    cos = jnp.repeat(jnp.cos(ang), 2, axis=-1)
    sin = jnp.stack([-jnp.sin(ang), jnp.sin(ang)], axis=-1).reshape(DEC_SEQ, HEAD_DIM)
    return cos, sin


def kernel(x_prompt, x_sample, cache_k_a, cache_v_a, cache_k_b, cache_v_b, c, c_ctx, w_mod, b_mod, w_in, w_o,
           sink_a, rel_bias_b, w_router, w_gate, w_up, w_down, ln1_g, ln1_b, ln2_g, ln2_b):
    assert DEPTH == 1 and w_in.shape == (DEPTH, D_MODEL, IN_COLS)
    x_ctx = x_prompt.reshape(N_CTX_TOK, D_MODEL)
    x_lat = x_sample.reshape(N_LAT_TOK, D_MODEL)

    cond = jnp.concatenate([c_ctx[None, :], c, jnp.zeros((V7X_SUBLANES - N_COND, D_MODEL), F32)], axis=0)
    mod = _modulation(cond, w_mod[0], b_mod)
    mod3 = mod.reshape(V7X_SUBLANES * N_MOD, 1, D_MODEL)

    w_in_bf16 = w_in[0].astype(BF16)
    wr_pad = jnp.pad(w_router[0], ((0, 0), (0, V7X_LANES - N_EXPERTS))).astype(BF16)
    sink = sink_a[0]
    ln1 = (ln1_g, ln1_b)
    ln2 = (ln2_g, ln2_b)

    qa, ka, va, qb, kb, vb, wo_bf16 = _input_projection(x_lat, mod3, w_in_bf16, _rope_tables(), w_o[0], True)
    oa = _window_attention(sink, qa, ka, va,
                           cache_k_a.reshape(-1, HEAD_DIM), cache_v_a.reshape(-1, HEAD_DIM))
    ob = _neighborhood_attention(rel_bias_b.reshape(-1), qb, kb, vb,
                                 cache_k_b.reshape(-1, HEAD_DIM), cache_v_b.reshape(-1, HEAD_DIM))
    x1_lat, h2_lat, aff_lat = _output_projection(oa, ob, wo_bf16, x_lat, mod3, *ln1, wr_pad, DEC_SEQ, True)
    slot_lat = _route(aff_lat, CAP_LAT)
    xs_lat, gate_lat, slot_t_lat = _dispatch(slot_lat, aff_lat, h2_lat, CAP_LAT, 8, 1)

    qa, ka, va, qb, kb, vb, ka_f, va_f, kb_f, vb_f = _input_projection(x_ctx, mod3, w_in_bf16, None, None, False)
    oa, ob = _context_attention(sink, qa, ka, va, qb, kb, vb)
    x1_ctx, h2_ctx, aff_ctx = _output_projection(oa, ob, wo_bf16, x_ctx, mod3, *ln1, wr_pad, SEQ, False)
    slot_ctx = _route(aff_ctx, CAP_CTX)
    xs_ctx, gate_ctx, slot_t_ctx = _dispatch(slot_ctx, aff_ctx, h2_ctx, CAP_CTX, N_EXPERTS, CTX_DISPATCH_REQUESTS)

    y_ctx, y_lat = _expert_ffn(xs_ctx, xs_lat, gate_ctx, gate_lat, w_gate[0], w_up[0], w_down[0])

    y_p = _combine(slot_t_ctx, y_ctx, x1_ctx, mod3, *ln2, BATCH, SEQ, CAP_CTX, False)
    y_s = _combine(slot_t_lat, y_lat, x1_lat, mod3, *ln2, DEC_BATCH, DEC_SEQ, CAP_LAT, True)

    kv_a_shape = (BATCH, DEPTH, SEQ, N_KV_A, HEAD_DIM)
    kv_b_shape = (BATCH, DEPTH, SEQ, N_HEADS_B, HEAD_DIM)
    return (y_p.reshape(BATCH, SEQ, D_MODEL), y_s.reshape(DEC_BATCH, DEC_SEQ, D_MODEL),
            ka_f.reshape(kv_a_shape), va_f.reshape(kv_a_shape), kb_f.reshape(kv_b_shape), vb_f.reshape(kv_b_shape))
```

```python
import functools

import jax
import jax.numpy as jnp
from jax import lax
from jax.experimental import pallas as pl
from jax.experimental.pallas import tpu as pltpu

D_MODEL = 2048
BATCH = 32
SEQ = 256
DEC_BATCH = 4
DEC_SEQ = 1024
PAST_LEN = 256
GRID_W = 64
HEAD_DIM = 128
N_HEADS_A = 8
N_KV_A = 2
GROUP_A = N_HEADS_A // N_KV_A
N_HEADS_B = 8
WINDOW_A = 128
NA_ROWS = 8
NA_COLS = 16
N_EXPERTS = 16
EC_CAPACITY = 2
D_FF = D_MODEL
ROPE_THETA = 10000.0
LN_EPS = 1e-5
NEG_INF = -1e30
DEPTH = 1
Q_A_COLS = N_HEADS_A * HEAD_DIM
KV_A_COLS = N_KV_A * HEAD_DIM
Q_B_COLS = N_HEADS_B * HEAD_DIM
IN_COLS = Q_A_COLS + 2 * KV_A_COLS + 3 * Q_B_COLS
COL_STARTS = (0, Q_A_COLS, Q_A_COLS + KV_A_COLS, Q_A_COLS + 2 * KV_A_COLS,
              Q_A_COLS + 2 * KV_A_COLS + Q_B_COLS, Q_A_COLS + 2 * KV_A_COLS + 2 * Q_B_COLS, IN_COLS)
DEEPNORM_ALPHA = (2.0 * DEPTH) ** 0.25
ATTN_SCALE = HEAD_DIM ** -0.5

N_CTX_TOK = BATCH * SEQ
N_LAT_TOK = DEC_BATCH * DEC_SEQ
GRID_ROWS = DEC_SEQ // GRID_W
CAP_CTX = EC_CAPACITY * SEQ // N_EXPERTS
CAP_LAT = EC_CAPACITY * DEC_SEQ // N_EXPERTS
N_COND = 1 + DEC_BATCH
N_MOD = 6

V7X_LANES = 128
V7X_SUBLANES = 8
V7X_VMEM_BYTES = 64 * 1024 * 1024
MIB = 1024 * 1024

F32 = jnp.float32
BF16 = jnp.bfloat16
NT_DIMS = (((1,), (1,)), ((), ()))

TOKEN_TILE = 256
MOD_COL_TILE = 1024
FFN_COL_TILE = 512
FFN_ROW_CHUNK = 256
FFN_DOWN_COL_TILE = 1024
INPROJ_TILE = 512
OUTPROJ_TILE = 512
OUTPROJ_SUBTILES = (256, 256)
CTX_ATTN_REQUESTS = 4
CTX_DISPATCH_REQUESTS = 4
COMBINE_TILE = 512
WINDOW_Q_BLOCK = 128
NA_REQUESTS = 4
NA_GROUP_ROWS = 4
NA_WIN_ROWS = 12
NA_REL_ROWS = 2 * NA_ROWS - 1
NA_REL_COLS = 2 * NA_COLS - 1


def _params(vmem_mib, n_axes):
    assert vmem_mib * MIB < V7X_VMEM_BYTES
    return pltpu.CompilerParams(dimension_semantics=("arbitrary",) * n_axes,
                                vmem_limit_bytes=vmem_mib * MIB)


def _hbm_array(shape, dtype):
    return pltpu.HBM(tuple(shape), dtype)


def _in_hbm(*arrays):
    return [pltpu.with_memory_space_constraint(a, pltpu.HBM) for a in arrays]


def _dot(a, b):
    return jnp.dot(a, b, preferred_element_type=F32)


def _dot_nt(a, b):
    return lax.dot_general(a, b, NT_DIMS, preferred_element_type=F32)


def _layer_norm(z, g, b):
    mu = jnp.mean(z, axis=-1, keepdims=True)
    zc = z - mu
    var = jnp.mean(zc * zc, axis=-1, keepdims=True)
    return zc * lax.rsqrt(var + LN_EPS) * g + b


def _mod_kernel(cond_ref, w_ref, b_ref, o_ref):
    s = jax.nn.silu(cond_ref[...]).astype(BF16)
    o_ref[...] = _dot(s, w_ref[...].astype(BF16)) + b_ref[...]


def _modulation(cond, w_mod, b_mod):
    n_out = N_MOD * D_MODEL
    return pl.pallas_call(
        _mod_kernel,
        grid=(n_out // MOD_COL_TILE,),
        in_specs=[pl.BlockSpec((V7X_SUBLANES, D_MODEL), lambda j: (0, 0)),
                  pl.BlockSpec((D_MODEL, MOD_COL_TILE), lambda j: (0, j)),
                  pl.BlockSpec((1, MOD_COL_TILE), lambda j: (0, j))],
        out_specs=pl.BlockSpec((V7X_SUBLANES, MOD_COL_TILE), lambda j: (0, j)),
        out_shape=_hbm_array((V7X_SUBLANES, n_out), F32),
        compiler_params=_params(40, 1),
        name="modulation",
    )(cond, w_mod, b_mod)


def _mod_spec(which, latent, tile):
    steps_per_request = DEC_SEQ // tile
    if latent:
        return pl.BlockSpec((None, 1, D_MODEL), lambda i: ((1 + i // steps_per_request) * N_MOD + which, 0, 0))
    return pl.BlockSpec((None, 1, D_MODEL), lambda i: (which, 0, 0))


def _rope(x, cos, sin, even):
    swapped = jnp.where(even, pltpu.roll(x, HEAD_DIM - 1, 1), pltpu.roll(x, 1, 1))
    return x * cos + swapped * sin


def _inproj_kernel(*refs, latent):
    if latent:
        (x_ref, shift_ref, scale_ref, w_ref, cos_ref, sin_ref, wo_ref,
         qa_o, ka_o, va_o, qb_o, kb_o, vb_o, wo_o) = refs
        wo_o[...] = wo_ref[...].astype(BF16)
    else:
        (x_ref, shift_ref, scale_ref, w_ref,
         qa_o, ka_o, va_o, qb_o, kb_o, vb_o, kaf_o, vaf_o, kbf_o, vbf_o) = refs
    h = (x_ref[...] * (1.0 + scale_ref[...]) + shift_ref[...]).astype(BF16)

    def proj(piece):
        return _dot(h, w_ref[:, COL_STARTS[piece]:COL_STARTS[piece + 1]])

    if latent:
        cos = cos_ref[...]
        sin = sin_ref[...]
        even = (lax.broadcasted_iota(jnp.int32, cos.shape, 1) & 1) == 0

    def heads(y, o_ref, n_heads, scale, rope):
        for hd in range(n_heads):
            cols = slice(hd * HEAD_DIM, (hd + 1) * HEAD_DIM)
            yh = y[:, cols]
            if rope:
                yh = _rope(yh, cos, sin, even)
            if scale != 1.0:
                yh = yh * scale
            o_ref[:, cols] = yh.astype(BF16)

    qa = proj(0)
    heads(qa, qa_o, N_HEADS_A, ATTN_SCALE, latent)
    ka = proj(1)
    heads(ka, ka_o, N_KV_A, 1.0, latent)
    va = proj(2)
    va_o[...] = va.astype(BF16)
    qb = proj(3)
    qb_o[...] = (qb * ATTN_SCALE).astype(BF16)
    kb = proj(4)
    kb_o[...] = kb.astype(BF16)
    vb = proj(5)
    vb_o[...] = vb.astype(BF16)
    if not latent:
        for o_ref, y in ((kaf_o, ka), (vaf_o, va)):
            for hd in range(N_KV_A):
                o_ref[pl.ds(hd, y.shape[0], stride=N_KV_A), :] = y[:, hd * HEAD_DIM:(hd + 1) * HEAD_DIM]
        kbf_o[...] = kb.reshape(kbf_o.shape)
        vbf_o[...] = vb.reshape(vbf_o.shape)


def _input_projection(x, mod3, w_in_bf16, rope_tables, w_o, latent):
    n_tok = x.shape[0]
    tm = INPROJ_TILE
    row = lambda i: (i, 0)
    in_specs = [pl.BlockSpec((tm, D_MODEL), row), _mod_spec(0, latent, tm), _mod_spec(1, latent, tm),
                pl.BlockSpec((D_MODEL, IN_COLS), lambda i: (0, 0), pipeline_mode=pl.Buffered(1))]
    args = [x, mod3, mod3, w_in_bf16]
    widths = [Q_A_COLS, KV_A_COLS, KV_A_COLS, Q_B_COLS, Q_B_COLS, Q_B_COLS]
    out_specs = [pl.BlockSpec((tm, w), row) for w in widths]
    out_shape = [_hbm_array((n_tok, w), BF16) for w in widths]
    if latent:
        steps_per_request = DEC_SEQ // tm
        in_specs += [pl.BlockSpec((tm, HEAD_DIM), lambda i: (i % steps_per_request, 0))] * 2
        args += list(rope_tables)
        wo_rows = w_o.shape[0] // (n_tok // tm)
        in_specs += [pl.BlockSpec((wo_rows, D_MODEL), row)]
        args += [w_o]
        out_specs += [pl.BlockSpec((wo_rows, D_MODEL), row)]
        out_shape += [_hbm_array(w_o.shape, BF16)]
    else:
        out_specs += [pl.BlockSpec((tm * N_KV_A, HEAD_DIM), row)] * 2
        out_shape += [_hbm_array((n_tok * N_KV_A, HEAD_DIM), F32)] * 2
        out_specs += [pl.BlockSpec((tm * N_HEADS_B, HEAD_DIM), row)] * 2
        out_shape += [_hbm_array((n_tok * N_HEADS_B, HEAD_DIM), F32)] * 2
    return pl.pallas_call(
        functools.partial(_inproj_kernel, latent=latent),
        grid=(n_tok // tm,),
        in_specs=in_specs, out_specs=out_specs, out_shape=out_shape,
        compiler_params=_params(58, 1),
        name="inproj_latent" if latent else "inproj_context",
    )(*args)


def _with_ones(v):
    return jnp.concatenate([v, jnp.ones((v.shape[0], V7X_LANES), v.dtype)], axis=1)


def _softmax_av(scores, values_with_ones, sink=None):
    m = functools.reduce(jnp.maximum, [jnp.max(s, axis=-1, keepdims=True) for s in scores])
    if sink is not None:
        m = jnp.maximum(m, sink)
    acc = functools.reduce(jnp.add, [_dot(jnp.exp(s - m).astype(BF16), v)
                                     for s, v in zip(scores, values_with_ones)])
    denom = acc[:, HEAD_DIM:]
    if sink is not None:
        denom = denom + jnp.exp(sink - m)
    return acc[:, :HEAD_DIM] / denom


def _cache_head(c_ref, head, n_heads, request=0):
    return c_ref[pl.ds(request * PAST_LEN * n_heads + head, PAST_LEN, stride=n_heads), :].astype(BF16)


def _ctx_attn_kernel(sink_ref, qa_ref, ka_ref, va_ref, qb_ref, kb_ref, vb_ref, oa_ref, ob_ref):
    for req in range(qa_ref.shape[0] // SEQ):
        rows = slice(req * SEQ, (req + 1) * SEQ)
        for kv in range(N_KV_A):
            kv_cols = slice(kv * HEAD_DIM, (kv + 1) * HEAD_DIM)
            k = ka_ref[rows, kv_cols]
            v = _with_ones(va_ref[rows, kv_cols])
            for hd in range(kv * GROUP_A, (kv + 1) * GROUP_A):
                cols = slice(hd * HEAD_DIM, (hd + 1) * HEAD_DIM)
                sink = jnp.full((SEQ, 1), sink_ref[hd], F32)
                oa_ref[rows, cols] = _softmax_av([_dot_nt(qa_ref[rows, cols], k)], [v], sink).astype(BF16)
        for hd in range(N_HEADS_B):
            cols = slice(hd * HEAD_DIM, (hd + 1) * HEAD_DIM)
            o = _softmax_av([_dot_nt(qb_ref[rows, cols], kb_ref[rows, cols])], [_with_ones(vb_ref[rows, cols])])
            ob_ref[rows, cols] = o.astype(BF16)


def _context_attention(sink, qa, ka, va, qb, kb, vb):
    row = lambda b: (b, 0)
    rows = CTX_ATTN_REQUESTS * SEQ
    widths = [Q_A_COLS, KV_A_COLS, KV_A_COLS, Q_B_COLS, Q_B_COLS, Q_B_COLS]
    return pl.pallas_call(
        _ctx_attn_kernel,
        grid=(BATCH // CTX_ATTN_REQUESTS,),
        in_specs=[pl.BlockSpec(memory_space=pltpu.SMEM)] + [pl.BlockSpec((rows, w), row) for w in widths],
        out_specs=[pl.BlockSpec((rows, Q_A_COLS), row), pl.BlockSpec((rows, Q_B_COLS), row)],
        out_shape=[_hbm_array((N_CTX_TOK, Q_A_COLS), BF16),
                   _hbm_array((N_CTX_TOK, Q_B_COLS), BF16)],
        compiler_params=_params(32, 1),
        name="context_attention",
    )(sink, *_in_hbm(qa, ka, va, qb, kb, vb))


def _window_attn_kernel(sink_ref, q_ref, k_ref, v_ref, ck_ref, cv_ref, o_ref):
    kv = pl.program_id(1)
    qb = WINDOW_Q_BLOCK
    span = qb + 2 * WINDOW_A
    ck = _cache_head(ck_ref, kv, N_KV_A)
    cv = _with_ones(_cache_head(cv_ref, kv, N_KV_A))
    band = (lax.broadcasted_iota(jnp.int32, (qb, span), 1)
            - lax.broadcasted_iota(jnp.int32, (qb, span), 0))
    masks = {}
    for n in range(DEC_SEQ // qb):
        start = min(max(n * qb - WINDOW_A, 0), DEC_SEQ - span)
        offset = start - n * qb
        if offset not in masks:
            masks[offset] = jnp.where(jnp.abs(band + offset) <= WINDOW_A, 0.0, NEG_INF)
        rows = slice(n * qb, (n + 1) * qb)
        k_win = k_ref[start:start + span, :]
        v_win = _with_ones(v_ref[start:start + span, :])
        for g in range(GROUP_A):
            cols = slice(g * HEAD_DIM, (g + 1) * HEAD_DIM)
            q = q_ref[rows, cols]
            sink = jnp.full((qb, 1), sink_ref[kv * GROUP_A + g], F32)
            o = _softmax_av([_dot_nt(q, k_win) + masks[offset], _dot_nt(q, ck)], [v_win, cv], sink)
            o_ref[rows, cols] = o.astype(BF16)


def _window_attention(sink, qa, ka, va, ck, cv):
    q_spec = pl.BlockSpec((DEC_SEQ, GROUP_A * HEAD_DIM), lambda b, kv: (b, kv))
    kv_spec = pl.BlockSpec((DEC_SEQ, HEAD_DIM), lambda b, kv: (b, kv))
    c_spec = pl.BlockSpec((PAST_LEN * N_KV_A, HEAD_DIM), lambda b, kv: (b, 0))
    return pl.pallas_call(
        _window_attn_kernel,
        grid=(DEC_BATCH, N_KV_A),
        in_specs=[pl.BlockSpec(memory_space=pltpu.SMEM), q_spec, kv_spec, kv_spec, c_spec, c_spec],
        out_specs=q_spec,
        out_shape=_hbm_array((N_LAT_TOK, Q_A_COLS), BF16),
        compiler_params=_params(32, 2),
        name="window_attention",
    )(sink, *_in_hbm(qa, ka, va, ck, cv))


def _na_row_start(rq):
    return min(max(rq - NA_ROWS // 2, 0), GRID_ROWS - NA_ROWS)


def _na_window(group):
    first = _na_row_start(group * NA_GROUP_ROWS)
    last = _na_row_start(group * NA_GROUP_ROWS + NA_GROUP_ROWS - 1) + NA_ROWS - 1
    n_rows = last - first + 1
    n_rows += n_rows % 2
    assert first % 2 == 0 and n_rows <= NA_WIN_ROWS and first + n_rows <= GRID_ROWS
    return first, n_rows


def _na_build_bias(rb_ref, bias_ref, head):
    shape = (GRID_W, 2 * GRID_W)
    lane = lax.broadcasted_iota(jnp.int32, shape, 1)
    qcol = lax.broadcasted_iota(jnp.int32, shape, 0)
    kcol = lane & (GRID_W - 1)
    left = lane < GRID_W
    dcol = kcol - qcol + (NA_COLS - 1)
    wstart = jnp.clip(qcol - NA_COLS // 2, 0, GRID_W - NA_COLS)
    col_ok = (kcol >= wstart) & (kcol < wstart + NA_COLS)
    base = head * (NA_REL_ROWS * NA_REL_COLS)
    toeplitz = []
    for a in range(NA_REL_ROWS):
        acc = jnp.zeros(shape, F32)
        for bb in range(NA_REL_COLS):
            acc = jnp.where(dcol == bb, rb_ref[base + a * NA_REL_COLS + bb], acc)
        toeplitz.append(acc)
    neg = jnp.full(shape, NEG_INF, F32)
    for g in range(GRID_ROWS // NA_GROUP_ROWS):
        wrow, n_rows = _na_window(g)
        for qi in range(NA_GROUP_ROWS):
            rq = g * NA_GROUP_ROWS + qi
            row0 = _na_row_start(rq)
            for p in range(n_rows // 2):
                rk0 = wrow + 2 * p
                ok0 = row0 <= rk0 < row0 + NA_ROWS
                ok1 = row0 <= rk0 + 1 < row0 + NA_ROWS
                a0 = rk0 - rq + NA_ROWS - 1
                if ok0 and ok1:
                    blk = jnp.where(col_ok, jnp.where(left, toeplitz[a0], toeplitz[a0 + 1]), neg)
                elif ok0:
                    blk = jnp.where(col_ok & left, toeplitz[a0], neg)
                elif ok1:
                    blk = jnp.where(col_ok & jnp.logical_not(left), toeplitz[a0 + 1], neg)
                else:
                    blk = neg
                bias_ref[g, qi * GRID_W:(qi + 1) * GRID_W, p * 2 * GRID_W:(p + 1) * 2 * GRID_W] = blk


def _na_attn_kernel(rb_ref, q_ref, k_ref, v_ref, ck_ref, cv_ref, o_ref, bias_ref):
    head = pl.program_id(0)

    @pl.when(pl.program_id(1) == 0)
    def _():
        _na_build_bias(rb_ref, bias_ref, head)

    q_rows = NA_GROUP_ROWS * GRID_W
    for req in range(q_ref.shape[0] // DEC_SEQ):
        ck = _cache_head(ck_ref, head, N_HEADS_B, req)
        cv = _with_ones(_cache_head(cv_ref, head, N_HEADS_B, req))
        for g in range(GRID_ROWS // NA_GROUP_ROWS):
            first, n_rows = _na_window(g)
            k0, k_rows = req * DEC_SEQ + first * GRID_W, n_rows * GRID_W
            rows = slice(req * DEC_SEQ + g * q_rows, req * DEC_SEQ + (g + 1) * q_rows)
            q = q_ref[rows, :]
            s_loc = _dot_nt(q, k_ref[k0:k0 + k_rows, :]) + bias_ref[g, :, :k_rows]
            s_ctx = _dot_nt(q, ck)
            o = _softmax_av([s_loc, s_ctx], [_with_ones(v_ref[k0:k0 + k_rows, :]), cv])
            o_ref[rows, :] = o.astype(BF16)


def _neighborhood_attention(rel_bias_flat, qb, kb, vb, ck, cv):
    n_req = NA_REQUESTS
    tok_spec = pl.BlockSpec((n_req * DEC_SEQ, HEAD_DIM), lambda h, b: (b, h))
    c_spec = pl.BlockSpec((n_req * PAST_LEN * N_HEADS_B, HEAD_DIM), lambda h, b: (b, 0))
    n_groups = GRID_ROWS // NA_GROUP_ROWS
    return pl.pallas_call(
        _na_attn_kernel,
        grid=(N_HEADS_B, DEC_BATCH // n_req),
        in_specs=[pl.BlockSpec(memory_space=pltpu.SMEM), tok_spec, tok_spec, tok_spec, c_spec, c_spec],
        out_specs=tok_spec,
        out_shape=_hbm_array((N_LAT_TOK, Q_B_COLS), BF16),
        scratch_shapes=[pltpu.VMEM((n_groups, NA_GROUP_ROWS * GRID_W, NA_WIN_ROWS * GRID_W), F32)],
        compiler_params=_params(44, 2),
        name="neighborhood_attention",
    )(rel_bias_flat, *_in_hbm(qb, kb, vb, ck, cv))


def _outproj_kernel(oa_ref, ob_ref, wo_ref, x_ref, gate1_ref, shift2_ref, scale2_ref, g_ref, b_ref, wr_ref,
                    x1_ref, h2_ref, aff_ref):
    lanes = aff_ref.shape[2]
    r0 = 0
    for n_rows in OUTPROJ_SUBTILES:
        rows = slice(r0, r0 + n_rows)
        halves = [slice(c, c + D_MODEL // 2) for c in (0, D_MODEL // 2)]
        z = [DEEPNORM_ALPHA * x_ref[rows, cs] + gate1_ref[:, cs] * (
            _dot(oa_ref[rows, :], wo_ref[:Q_A_COLS, cs]) + _dot(ob_ref[rows, :], wo_ref[Q_A_COLS:, cs]))
             for cs in halves]
        mu = (jnp.sum(z[0], axis=-1, keepdims=True) + jnp.sum(z[1], axis=-1, keepdims=True)) / D_MODEL
        zc = [zh - mu for zh in z]
        var = (jnp.sum(zc[0] * zc[0], axis=-1, keepdims=True)
               + jnp.sum(zc[1] * zc[1], axis=-1, keepdims=True)) / D_MODEL
        rstd = lax.rsqrt(var + LN_EPS)
        logits = None
        for zh, cs in zip(zc, halves):
            x1 = zh * rstd * g_ref[:, cs] + b_ref[:, cs]
            x1_ref[rows, cs] = x1
            h2 = (x1 * (1.0 + scale2_ref[:, cs]) + shift2_ref[:, cs]).astype(BF16)
            h2_ref[rows, cs] = h2
            part = _dot(h2, wr_ref[cs, :])
            logits = part if logits is None else logits + part
        lt = logits.T[:N_EXPERTS, :]
        e = jnp.exp(lt - jnp.max(lt, axis=0, keepdims=True))
        req, l0 = divmod(r0, lanes)
        aff_ref[req, :, l0:l0 + n_rows] = e / jnp.sum(e, axis=0, keepdims=True)
        r0 += n_rows


def _output_projection(oa, ob, wo_bf16, x, mod3, ln_g, ln_b, wr_pad, seq, latent):
    n_tok = x.shape[0]
    tm = OUTPROJ_TILE
    row = lambda i: (i, 0)
    const = lambda i: (0, 0)
    if seq >= tm:
        tiles_per_request = seq // tm
        aff_spec = pl.BlockSpec((1, N_EXPERTS, tm), lambda i: (i // tiles_per_request, 0, i % tiles_per_request))
    else:
        aff_spec = pl.BlockSpec((tm // seq, N_EXPERTS, seq), lambda i: (i, 0, 0))
    return pl.pallas_call(
        _outproj_kernel,
        grid=(n_tok // tm,),
        in_specs=[pl.BlockSpec((tm, Q_A_COLS), row), pl.BlockSpec((tm, Q_B_COLS), row),
                  pl.BlockSpec((Q_A_COLS + Q_B_COLS, D_MODEL), const, pipeline_mode=pl.Buffered(1)),
                  pl.BlockSpec((tm, D_MODEL), row),
                  _mod_spec(2, latent, tm), _mod_spec(3, latent, tm), _mod_spec(4, latent, tm),
                  pl.BlockSpec((1, D_MODEL), const), pl.BlockSpec((1, D_MODEL), const),
                  pl.BlockSpec((D_MODEL, V7X_LANES), const)],
        out_specs=[pl.BlockSpec((tm, D_MODEL), row), pl.BlockSpec((tm, D_MODEL), row), aff_spec],
        out_shape=[_hbm_array((n_tok, D_MODEL), F32), _hbm_array((n_tok, D_MODEL), BF16),
                   _hbm_array((n_tok // seq, N_EXPERTS, seq), F32)],
        compiler_params=_params(56, 1),
        name="outproj_latent" if latent else "outproj_context",
    )(*_in_hbm(oa, ob, wo_bf16, x), mod3, mod3, mod3, ln_g, ln_b, wr_pad)


def _pad_rows_to_lanes(a):
    return jnp.concatenate([a, jnp.zeros((V7X_LANES - a.shape[0], a.shape[1]), a.dtype)], axis=0)


def _kth_largest(a, k):
    n = a.shape[1]
    lane = lax.broadcasted_iota(jnp.int32, (1, n), 1)
    x = a
    size = 2
    while size <= n:
        descending = (lane & size) == 0
        j = size // 2
        while j >= 1:
            lower = (lane & j) == 0
            partner = jnp.where(lower, pltpu.roll(x, n - j, 1), pltpu.roll(x, j, 1))
            x = jnp.where(lower == descending, jnp.maximum(x, partner), jnp.minimum(x, partner))
            j //= 2
        size *= 2
    return x[:, k - 1:k]


def _route_kernel(aff_ref, slot_ref, *, cap):
    n_req, n_exp, seq = aff_ref.shape
    key = aff_ref[...].reshape(n_req * n_exp, seq)
    tau = _kth_largest(key, cap)
    above = key > tau
    tied = key == tau
    need = cap - jnp.sum(jnp.where(above, 1.0, 0.0), axis=-1, keepdims=True)
    before = jnp.where(lax.broadcasted_iota(jnp.int32, (seq, seq), 0) < lax.broadcasted_iota(jnp.int32, (seq, seq), 1),
                       1.0, 0.0).astype(BF16)
    tied_before = _dot(jnp.where(tied, 1.0, 0.0).astype(BF16), before)
    chosen = above | (tied & (tied_before < need))
    slot = _dot(jnp.where(chosen, 1.0, 0.0).astype(BF16), before)
    slot_ref[...] = jnp.where(chosen, slot, float(cap)).reshape(n_req, n_exp, seq)


def _route(aff, cap):
    n_req, _, seq = aff.shape
    spec = pl.BlockSpec((n_req, N_EXPERTS, seq), lambda r: (0, 0, 0))
    return pl.pallas_call(
        functools.partial(_route_kernel, cap=cap),
        grid=(1,),
        in_specs=[spec], out_specs=spec,
        out_shape=_hbm_array(aff.shape, F32),
        compiler_params=_params(32, 1),
        name=f"route_s{seq}",
    )(*_in_hbm(aff))


def _dispatch_kernel(slot_ref, aff_ref, h2_ref, xs_ref, gate_ref, slot_t_ref, *, cap, experts_per_dot):
    n_req, _, seq = slot_ref.shape
    slot = lax.broadcasted_iota(jnp.int32, (cap, seq), 0).astype(F32)
    for r in range(n_req):
        slot_of = slot_ref[r]
        aff = aff_ref[r]
        rows = slice(r * seq, (r + 1) * seq)
        slot_t_ref[rows, :] = _pad_rows_to_lanes(slot_of).T
        h2 = h2_ref[rows, :]
        for c in range(N_EXPERTS // experts_per_dot):
            chunk = range(c * experts_per_dot, (c + 1) * experts_per_dot)
            onehot = jnp.concatenate([jnp.where(slot_of[e:e + 1, :] == slot, 1.0, 0.0) for e in chunk], axis=0)
            xs = _dot(onehot.astype(BF16), h2)
            for i, e in enumerate(chunk):
                xs_ref[r, e] = xs[i * cap:(i + 1) * cap].astype(BF16)
                gate = jnp.sum(onehot[i * cap:(i + 1) * cap] * aff[e:e + 1, :], axis=1, keepdims=True)
                gate_ref[r, e] = jnp.broadcast_to(gate, (cap, V7X_LANES))


def _dispatch(slot, aff, h2, cap, experts_per_dot, requests_per_step):
    n_req, _, seq = aff.shape
    g = requests_per_step
    req_spec = pl.BlockSpec((g, N_EXPERTS, seq), lambda r: (r, 0, 0))
    return pl.pallas_call(
        functools.partial(_dispatch_kernel, cap=cap, experts_per_dot=experts_per_dot),
        grid=(n_req // g,),
        in_specs=[req_spec, req_spec, pl.BlockSpec((g * seq, D_MODEL), lambda r: (r, 0))],
        out_specs=[pl.BlockSpec((g, N_EXPERTS, cap, D_MODEL), lambda r: (r, 0, 0, 0)),
                   pl.BlockSpec((g, N_EXPERTS, cap, V7X_LANES), lambda r: (r, 0, 0, 0)),
                   pl.BlockSpec((g * seq, V7X_LANES), lambda r: (r, 0))],
        out_shape=[_hbm_array((n_req, N_EXPERTS, cap, D_MODEL), BF16),
                   _hbm_array((n_req, N_EXPERTS, cap, V7X_LANES), F32),
                   _hbm_array((n_req * seq, V7X_LANES), F32)],
        compiler_params=_params(48, 1),
        name=f"dispatch_s{seq}",
    )(*_in_hbm(slot, aff, h2))


def _expert_rows(ref):
    n_req, cap, width = ref.shape
    return ref[...].reshape(n_req * cap, width)


def _ffn_up_kernel(xc_ref, xl_ref, wg_ref, wu_ref, hc_ref, hl_ref):
    wg = wg_ref[...].astype(BF16)
    wu = wu_ref[...].astype(BF16)
    for x_ref, h_ref in ((xc_ref, hc_ref), (xl_ref, hl_ref)):
        x = _expert_rows(x_ref)
        for r0 in range(0, x.shape[0], FFN_ROW_CHUNK):
            xr = x[r0:r0 + FFN_ROW_CHUNK]
            h_ref[r0:r0 + FFN_ROW_CHUNK, :] = (jax.nn.silu(_dot(xr, wg)) * _dot(xr, wu)).astype(BF16)


def _ffn_down_kernel(hc_ref, hl_ref, wd_ref, gc_ref, gl_ref, yc_ref, yl_ref):
    wd = wd_ref[...].astype(BF16)
    for h_ref, g_ref, y_ref in ((hc_ref, gc_ref, yc_ref), (hl_ref, gl_ref, yl_ref)):
        gate = jnp.concatenate([_expert_rows(g_ref)] * (wd.shape[1] // V7X_LANES), axis=1)
        y_ref[...] = (_dot(h_ref[...], wd) * gate).astype(BF16).reshape(y_ref.shape)


def _expert_ffn(xs_ctx, xs_lat, gate_ctx, gate_lat, w_gate, w_up, w_down):
    def per_expert(a, width):
        n_req, _, cap, _ = a.shape
        return pl.BlockSpec((n_req, None, cap, width), lambda e, j: (0, e, 0, 0 if width == a.shape[3] else j))

    rows_c = xs_ctx.shape[0] * xs_ctx.shape[2]
    rows_l = xs_lat.shape[0] * xs_lat.shape[2]
    tn = FFN_COL_TILE
    hid_c = pl.BlockSpec((None, rows_c, tn), lambda e, j: (e, 0, j))
    hid_l = pl.BlockSpec((None, rows_l, tn), lambda e, j: (e, 0, j))
    hidden_c, hidden_l = pl.pallas_call(
        _ffn_up_kernel,
        grid=(N_EXPERTS, D_FF // tn),
        in_specs=[per_expert(xs_ctx, D_MODEL), per_expert(xs_lat, D_MODEL),
                  pl.BlockSpec((None, D_MODEL, tn), lambda e, j: (e, 0, j)),
                  pl.BlockSpec((None, D_MODEL, tn), lambda e, j: (e, 0, j))],
        out_specs=[hid_c, hid_l],
        out_shape=[_hbm_array((N_EXPERTS, rows_c, D_FF), BF16),
                   _hbm_array((N_EXPERTS, rows_l, D_FF), BF16)],
        compiler_params=_params(56, 2),
        name="ffn_up",
    )(*_in_hbm(xs_ctx, xs_lat), w_gate, w_up)
    tn = FFN_DOWN_COL_TILE
    return pl.pallas_call(
        _ffn_down_kernel,
        grid=(N_EXPERTS, D_MODEL // tn),
        in_specs=[pl.BlockSpec((None, rows_c, D_FF), lambda e, j: (e, 0, 0)),
                  pl.BlockSpec((None, rows_l, D_FF), lambda e, j: (e, 0, 0)),
                  pl.BlockSpec((None, D_FF, tn), lambda e, j: (e, 0, j)),
                  per_expert(gate_ctx, V7X_LANES), per_expert(gate_lat, V7X_LANES)],
        out_specs=[per_expert(xs_ctx, tn), per_expert(xs_lat, tn)],
        out_shape=[_hbm_array(xs_ctx.shape, BF16), _hbm_array(xs_lat.shape, BF16)],
        compiler_params=_params(56, 2),
        name="ffn_down",
    )(*_in_hbm(hidden_c, hidden_l), w_down, *_in_hbm(gate_ctx, gate_lat))


def _combine_kernel(slot_t_ref, y_ref, x1_ref, gate2_ref, g_ref, b_ref, o_ref, *, cap):
    ts = TOKEN_TILE
    n_sub = slot_t_ref.shape[0] // ts
    cols = N_EXPERTS * cap
    shift = cap.bit_length() - 1
    erow = lax.broadcasted_iota(jnp.int32, (V7X_LANES, cols), 0)
    ecol = lax.shift_right_logical(lax.broadcasted_iota(jnp.int32, (V7X_LANES, cols), 1), shift)
    expand = jnp.where(erow == ecol, 1.0, 0.0).astype(BF16)
    slot = (lax.broadcasted_iota(jnp.int32, (ts, cols), 1) & (cap - 1)).astype(F32)
    for r in range(n_sub):
        rows = slice(r * ts, (r + 1) * ts)
        req = (r * ts) // (slot_t_ref.shape[0] // y_ref.shape[0])
        rk = slot_t_ref[rows, :].astype(BF16)
        onehot_t = jnp.where(_dot(rk, expand) == slot, 1.0, 0.0).astype(BF16)
        ffn = _dot(onehot_t, y_ref[req].reshape(cols, D_MODEL))
        o_ref[rows, :] = _layer_norm(DEEPNORM_ALPHA * x1_ref[rows, :] + gate2_ref[...] * ffn,
                                     g_ref[...], b_ref[...])


def _combine(slot_t, y, x1, mod3, ln_g, ln_b, n_req, seq, cap, latent):
    ts = COMBINE_TILE
    steps = max(seq // ts, 1)
    g = max(ts // seq, 1)
    assert g == 1 or not latent
    ts //= g
    row = lambda r, t: (r * steps + t, 0)
    const = lambda r, t: (0, 0)
    mod_row = (lambda r, t: ((1 + r) * N_MOD + 5, 0, 0)) if latent else (lambda r, t: (5, 0, 0))
    return pl.pallas_call(
        functools.partial(_combine_kernel, cap=cap),
        grid=(n_req // g, steps),
        in_specs=[pl.BlockSpec((g * ts, V7X_LANES), row),
                  pl.BlockSpec((g, N_EXPERTS, cap, D_MODEL), lambda r, t: (r, 0, 0, 0)),
                  pl.BlockSpec((g * ts, D_MODEL), row),
                  pl.BlockSpec((None, 1, D_MODEL), mod_row),
                  pl.BlockSpec((1, D_MODEL), const), pl.BlockSpec((1, D_MODEL), const)],
        out_specs=pl.BlockSpec((g * ts, D_MODEL), row),
        out_shape=_hbm_array((n_req * seq, D_MODEL), F32),
        compiler_params=_params(48, 2),
        name=f"combine_s{seq}",
    )(*_in_hbm(slot_t, y, x1), mod3, ln_g, ln_b)


def _rope_tables():
    t = jnp.arange(DEC_SEQ)
    row = (t // GRID_W).astype(F32)
    col = (t % GRID_W).astype(F32)
    n_freq = HEAD_DIM // 4
    inv_freq = ROPE_THETA ** (-jnp.arange(n_freq, dtype=F32) / n_freq)
    ang = jnp.concatenate([row[:, None] * inv_freq, col[:, None] * inv_freq], axis=-1)
    cos = jnp.repeat(jnp.cos(ang), 2, axis=-1)
    sin = jnp.stack([-jnp.sin(ang), jnp.sin(ang)], axis=-1).reshape(DEC_SEQ, HEAD_DIM)
    return cos, sin


def kernel(x_prompt, x_sample, cache_k_a, cache_v_a, cache_k_b, cache_v_b, c, c_ctx, w_mod, b_mod, w_in, w_o,
           sink_a, rel_bias_b, w_router, w_gate, w_up, w_down, ln1_g, ln1_b, ln2_g, ln2_b):
    assert DEPTH == 1 and w_in.shape == (DEPTH, D_MODEL, IN_COLS)
    x_ctx = x_prompt.reshape(N_CTX_TOK, D_MODEL)
    x_lat = x_sample.reshape(N_LAT_TOK, D_MODEL)

    cond = jnp.concatenate([c_ctx[None, :], c, jnp.zeros((V7X_SUBLANES - N_COND, D_MODEL), F32)], axis=0)
    mod = _modulation(cond, w_mod[0], b_mod)
    mod3 = mod.reshape(V7X_SUBLANES * N_MOD, 1, D_MODEL)

    w_in_bf16 = w_in[0].astype(BF16)
    wr_pad = jnp.pad(w_router[0], ((0, 0), (0, V7X_LANES - N_EXPERTS))).astype(BF16)
    sink = sink_a[0]
    ln1 = (ln1_g, ln1_b)
    ln2 = (ln2_g, ln2_b)

    qa, ka, va, qb, kb, vb, wo_bf16 = _input_projection(x_lat, mod3, w_in_bf16, _rope_tables(), w_o[0], True)
    oa = _window_attention(sink, qa, ka, va,
                           cache_k_a.reshape(-1, HEAD_DIM), cache_v_a.reshape(-1, HEAD_DIM))
    ob = _neighborhood_attention(rel_bias_b.reshape(-1), qb, kb, vb,
                                 cache_k_b.reshape(-1, HEAD_DIM), cache_v_b.reshape(-1, HEAD_DIM))
    x1_lat, h2_lat, aff_lat = _output_projection(oa, ob, wo_bf16, x_lat, mod3, *ln1, wr_pad, DEC_SEQ, True)
    slot_lat = _route(aff_lat, CAP_LAT)
    xs_lat, gate_lat, slot_t_lat = _dispatch(slot_lat, aff_lat, h2_lat, CAP_LAT, 8, 1)

    qa, ka, va, qb, kb, vb, ka_f, va_f, kb_f, vb_f = _input_projection(x_ctx, mod3, w_in_bf16, None, None, False)
    oa, ob = _context_attention(sink, qa, ka, va, qb, kb, vb)
    x1_ctx, h2_ctx, aff_ctx = _output_projection(oa, ob, wo_bf16, x_ctx, mod3, *ln1, wr_pad, SEQ, False)
    slot_ctx = _route(aff_ctx, CAP_CTX)
    xs_ctx, gate_ctx, slot_t_ctx = _dispatch(slot_ctx, aff_ctx, h2_ctx, CAP_CTX, N_EXPERTS, CTX_DISPATCH_REQUESTS)

    y_ctx, y_lat = _expert_ffn(xs_ctx, xs_lat, gate_ctx, gate_lat, w_gate[0], w_up[0], w_down[0])

    y_p = _combine(slot_t_ctx, y_ctx, x1_ctx, mod3, *ln2, BATCH, SEQ, CAP_CTX, False)
    y_s = _combine(slot_t_lat, y_lat, x1_lat, mod3, *ln2, DEC_BATCH, DEC_SEQ, CAP_LAT, True)

    kv_a_shape = (BATCH, DEPTH, SEQ, N_KV_A, HEAD_DIM)
    kv_b_shape = (BATCH, DEPTH, SEQ, N_HEADS_B, HEAD_DIM)
    return (y_p.reshape(BATCH, SEQ, D_MODEL), y_s.reshape(DEC_BATCH, DEC_SEQ, D_MODEL),
            ka_f.reshape(kv_a_shape), va_f.reshape(kv_a_shape), kb_f.reshape(kv_b_shape), vb_f.reshape(kv_b_shape))
```
